```python
import math
import jax, jax.numpy as jnp
from jax import lax
import numpy as np

D_MODEL = 2048
BATCH = 4
SEQ = 2048
DEPTH = 4

N_MIXERS = 4
HEAD_DIM = 64
N_HEADS = D_MODEL // HEAD_DIM
N_KV_HEADS = N_HEADS // 4
GQA_GROUP = N_HEADS // N_KV_HEADS
WINDOW = 128
Q_BLOCK = 128
DIFF_HEADS = D_MODEL // (2 * HEAD_DIM)
DIFF_V_DIM = 2 * HEAD_DIM
HGRN_HEAD_DIM = 128
HGRN_HEADS = D_MODEL // HGRN_HEAD_DIM
HGRN_CHUNK = 64
IDX_HEADS = 16
IDX_DIM = 64
TOPK_MAX = 256
D_FF = 5632
CONV_WIDTH = 3
NUM_BUCKETS = 32
MAX_DISTANCE = 128
RMS_EPS = 1e-6
N_SWA = len(range(0, DEPTH, N_MIXERS))
N_DIFF = len(range(1, DEPTH, N_MIXERS))
N_HGRN = len(range(2, DEPTH, N_MIXERS))
N_DSA = len(range(3, DEPTH, N_MIXERS))
SWA_SPLITS = (N_HEADS * HEAD_DIM, N_KV_HEADS * HEAD_DIM, N_KV_HEADS * HEAD_DIM)
SWA_IN = sum(SWA_SPLITS)
DIFF_SPLITS = (DIFF_HEADS * 2 * HEAD_DIM, DIFF_HEADS * 2 * HEAD_DIM, DIFF_HEADS * DIFF_V_DIM)
DIFF_IN = sum(DIFF_SPLITS)
HGRN_IN = 4 * HGRN_HEADS * HGRN_HEAD_DIM
DSA_SPLITS = (N_HEADS * HEAD_DIM, N_KV_HEADS * HEAD_DIM, N_KV_HEADS * HEAD_DIM,
              IDX_HEADS * IDX_DIM, IDX_DIM, IDX_HEADS)
DSA_IN = sum(DSA_SPLITS)

kernel_name = 'hybrid_swa_diff_hgrn2_dsa_convffn'


def split_cols(z, sizes):
    return jnp.split(z, [int(c) for c in np.cumsum(sizes)[:-1]], axis=-1)


def rms_norm(x, g):
    xf = x.astype(jnp.float32)
    y = xf * lax.rsqrt(jnp.mean(xf * xf, axis=-1, keepdims=True) + RMS_EPS)
    return (y * g.astype(jnp.float32)).astype(x.dtype)


def rel_position_bias(dist, table):
    n = jnp.maximum(dist, 0)
    max_exact = NUM_BUCKETS // 2
    log_ratio = jnp.log(jnp.maximum(n, 1).astype(jnp.float32) / max_exact) / math.log(MAX_DISTANCE / max_exact)
    large = jnp.minimum(max_exact + (log_ratio * (NUM_BUCKETS - max_exact)).astype(jnp.int32), NUM_BUCKETS - 1)
    bucket = jnp.where(n < max_exact, n, large)
    return table.astype(jnp.float32)[bucket]


def swa_sink_attention(h, w_in, w_out, sinks, bias_table):
    B, S, _ = h.shape
    nb = S // Q_BLOCK
    q, k, v = split_cols(h @ w_in, SWA_SPLITS)
    q = q.reshape(B, nb, Q_BLOCK, N_KV_HEADS, GQA_GROUP, HEAD_DIM)
    pad = ((0, 0), (Q_BLOCK, 0), (0, 0))
    kp = jnp.pad(k, pad).reshape(B, nb + 1, Q_BLOCK, N_KV_HEADS, HEAD_DIM)
    vp = jnp.pad(v, pad).reshape(B, nb + 1, Q_BLOCK, N_KV_HEADS, HEAD_DIM)
    kb = jnp.concatenate([kp[:, :-1], kp[:, 1:]], axis=2)
    vb = jnp.concatenate([vp[:, :-1], vp[:, 1:]], axis=2)
    qi = jnp.arange(Q_BLOCK)
    kj = jnp.arange(2 * Q_BLOCK) - Q_BLOCK
    dist = qi[:, None] - kj[None, :]
    in_band = (dist >= 0) & (dist < WINDOW)
    key_exists = (jnp.arange(nb)[:, None] > 0) | (kj[None, :] >= 0)
    mask = in_band[None] & key_exists[:, None, :]
    bias = rel_position_bias(dist, bias_table).transpose(2, 0, 1).reshape(
        N_KV_HEADS, GQA_GROUP, Q_BLOCK, 2 * Q_BLOCK)
    s = jnp.einsum('bnqhgd,bnshd->bnhgqs', q, kb).astype(jnp.float32) * HEAD_DIM ** -0.5 + bias
    s = jnp.where(mask[None, :, None, None], s, -jnp.inf)
    sink = jnp.broadcast_to(sinks.astype(jnp.float32).reshape(N_KV_HEADS, GQA_GROUP, 1, 1),
                            s.shape[:-1] + (1,))
    p = jax.nn.softmax(jnp.concatenate([s, sink], axis=-1), axis=-1)[..., :-1]
    o = jnp.einsum('bnhgqs,bnshd->bnqhgd', p.astype(vb.dtype), vb)
    return o.reshape(B, S, N_HEADS * HEAD_DIM) @ w_out


def diff_attention(h, w_in, w_out, lambdas, subln_g, bias_table, layer_idx):
    B, S, _ = h.shape
    nb = S // Q_BLOCK
    q, k, v = split_cols(h @ w_in, DIFF_SPLITS)
    q = q.reshape(B, nb, Q_BLOCK, DIFF_HEADS, 2, HEAD_DIM).transpose(1, 0, 2, 3, 4, 5)
    k = k.reshape(B, S, DIFF_HEADS, 2, HEAD_DIM)
    v = v.reshape(B, S, DIFF_HEADS, DIFF_V_DIM)
    lam_init = 0.8 - 0.6 * math.exp(-0.3 * layer_idx)
    lf = lambdas.astype(jnp.float32)
    lam = jnp.exp(jnp.sum(lf[0] * lf[1])) - jnp.exp(jnp.sum(lf[2] * lf[3])) + lam_init
    kpos = jnp.arange(S)

    def block(args):
        qb, start = args
        dist = (start + jnp.arange(Q_BLOCK))[:, None] - kpos[None, :]
        bias = rel_position_bias(dist, bias_table).reshape(Q_BLOCK, S, DIFF_HEADS, 2).transpose(2, 3, 0, 1)
        s = jnp.einsum('bqhmd,bshmd->bhmqs', qb, k).astype(jnp.float32) * HEAD_DIM ** -0.5 + bias
        p = jax.nn.softmax(jnp.where(dist >= 0, s, -jnp.inf), axis=-1)
        a = p[:, :, 0] - lam * p[:, :, 1]
        return jnp.einsum('bhqs,bshe->bqhe', a.astype(v.dtype), v)

    o = lax.map(block, (q, jnp.arange(nb) * Q_BLOCK))
    o = o.transpose(1, 0, 2, 3, 4).reshape(B, S, DIFF_HEADS, DIFF_V_DIM)
    o = rms_norm(o, subln_g) * (1.0 - lam_init)
    return o.reshape(B, S, DIFF_HEADS * DIFF_V_DIM) @ w_out


def hgrn2_recurrence(h, w_in, w_out, lb, norm_g):
    B, S, _ = h.shape
    nc = S // HGRN_CHUNK
    f32 = jnp.float32
    q, fz, i, g = jnp.split(h @ w_in, 4, axis=-1)

    def chunks(t):
        return t.astype(f32).reshape(B, nc, HGRN_CHUNK, HGRN_HEADS, HGRN_HEAD_DIM).transpose(1, 0, 3, 2, 4)

    lbf = lb.astype(f32)
    log_f = jnp.logaddexp(jnp.log(lbf), jnp.log1p(-lbf) + jax.nn.log_sigmoid(fz.astype(f32)))
    key = -jnp.expm1(log_f)
    qs, ks, vs, gs = chunks(jax.nn.silu(q.astype(f32))), chunks(key), chunks(i), chunks(log_f)
    causal = jnp.tril(jnp.ones((HGRN_CHUNK, HGRN_CHUNK), dtype=bool))

    def step(state, inp):
        qc, kc, vc, gc = inp
        b = jnp.cumsum(gc, axis=2)
        o_inter = jnp.einsum('bhtk,bhkv->bhtv', qc * jnp.exp(b), state)
        rel = jnp.where(causal[:, :, None], b[:, :, :, None, :] - b[:, :, None, :, :], -jnp.inf)
        att = jnp.einsum('bhtsk,bhsk->bhts', qc[:, :, :, None, :] * jnp.exp(rel), kc)
        o_intra = jnp.einsum('bhts,bhsv->bhtv', att, vc)
        b_end = b[:, :, -1:, :]
        state = state * jnp.exp(b_end[:, :, 0, :, None]) + jnp.einsum(
            'bhsk,bhsv->bhkv', kc * jnp.exp(b_end - b), vc)
        return state, o_inter + o_intra

    state0 = jnp.zeros((B, HGRN_HEADS, HGRN_HEAD_DIM, HGRN_HEAD_DIM), f32)
    _, o = lax.scan(step, state0, (qs, ks, vs, gs))
    o = o.transpose(1, 0, 3, 2, 4).reshape(B, S, HGRN_HEADS, HGRN_HEAD_DIM)
    gate = jax.nn.silu(g.astype(f32)).reshape(B, S, HGRN_HEADS, HGRN_HEAD_DIM)
    o = (rms_norm(o, norm_g) * gate).astype(h.dtype)
    return o.reshape(B, S, HGRN_HEADS * HGRN_HEAD_DIM) @ w_out


def dsa_sparse_attention(h, w_in, w_out, bias_table):
    B, S, _ = h.shape
    nb = S // Q_BLOCK
    n_sel = min(TOPK_MAX, S // 4)
    f32 = jnp.float32
    q, k, v, qi, ki, wi = split_cols(h @ w_in, DSA_SPLITS)
    q = q.reshape(B, nb, Q_BLOCK, N_KV_HEADS, GQA_GROUP, HEAD_DIM).transpose(1, 0, 2, 3, 4, 5)
    qi = qi.reshape(B, nb, Q_BLOCK, IDX_HEADS, IDX_DIM).transpose(1, 0, 2, 3, 4)
    wi = wi.reshape(B, nb, Q_BLOCK, IDX_HEADS).transpose(1, 0, 2, 3)
    k = k.reshape(B, S, N_KV_HEADS, HEAD_DIM)
    v = v.reshape(B, S, N_KV_HEADS, HEAD_DIM)
    kpos = jnp.arange(S)
    gather = jax.vmap(lambda t, idx: t[idx])

    def block(args):
        qb, qib, wib, start = args
        qpos = start + jnp.arange(Q_BLOCK)
        logits = jnp.einsum('bqhd,bsd->bqhs', qib, ki).astype(f32) * IDX_DIM ** -0.5
        score = jnp.einsum('bqh,bqhs->bqs', wib.astype(f32) * IDX_HEADS ** -0.5, jax.nn.relu(logits))
        score = jnp.where(qpos[:, None] >= kpos[None, :], score, -jnp.inf)
        _, idx = lax.top_k(score, n_sel)
        valid = idx <= qpos[None, :, None]
        kg, vg = gather(k, idx), gather(v, idx)
        s = jnp.einsum('bqhgd,bqkhd->bhgqk', qb, kg).astype(f32) * HEAD_DIM ** -0.5
        bias = rel_position_bias(qpos[None, :, None] - idx, bias_table).transpose(0, 3, 1, 2).reshape(
            B, N_KV_HEADS, GQA_GROUP, Q_BLOCK, n_sel)
        s = jnp.where(valid[:, None, None], s + bias, -jnp.inf)
        p = jax.nn.softmax(s, axis=-1)
        return jnp.einsum('bhgqk,bqkhd->bqhgd', p.astype(vg.dtype), vg)

    o = lax.map(block, (q, qi, wi, jnp.arange(nb) * Q_BLOCK))
    return o.transpose(1, 0, 2, 3, 4, 5).reshape(B, S, N_HEADS * HEAD_DIM) @ w_out


def conv_ffn(h, w_up, conv_w, w_down):
    S = h.shape[1]
    u = h @ w_up
    up = jnp.pad(u, ((0, 0), (CONV_WIDTH - 1, 0), (0, 0)))
    c = conv_w[0] * up[:, 0:S]
    for j in range(1, CONV_WIDTH):
        c = c + conv_w[j] * up[:, j:j + S]
    gate, val = jnp.split(c, 2, axis=-1)
    return (jax.nn.silu(gate) * val) @ w_down


def setup_inputs(seed: int = 0) -> dict:
    key = jax.random.key(seed)
    ks = jax.random.split(key, 20)
    f32 = jnp.float32

    def dense(k, shape, fan_in):
        return jax.random.normal(k, shape, f32) * fan_in ** -0.5

    return {
        'x': jax.random.normal(ks[0], (BATCH, SEQ, D_MODEL), f32),
        'rel_bias_table': 0.3 * jax.random.normal(ks[1], (NUM_BUCKETS, N_HEADS), f32),
        'hgrn_lb_logits': 0.5 * jax.random.normal(ks[2], (DEPTH, HGRN_HEADS * HGRN_HEAD_DIM), f32),
        'norm_g': 1.0 + 0.02 * jax.random.normal(ks[3], (DEPTH, 4, D_MODEL), f32),
        'ffn_w_up': dense(ks[4], (DEPTH, D_MODEL, 2 * D_FF), D_MODEL),
        'ffn_conv': dense(ks[5], (DEPTH, CONV_WIDTH, 2 * D_FF), CONV_WIDTH),
        'ffn_w_down': dense(ks[6], (DEPTH, D_FF, D_MODEL), D_FF),
        'swa_w_in': dense(ks[7], (N_SWA, D_MODEL, SWA_IN), D_MODEL),
        'swa_w_out': dense(ks[8], (N_SWA, N_HEADS * HEAD_DIM, D_MODEL), N_HEADS * HEAD_DIM),
        'swa_sinks': 0.5 * jax.random.normal(ks[9], (N_SWA, N_HEADS), f32),
        'diff_w_in': dense(ks[10], (N_DIFF, D_MODEL, DIFF_IN), D_MODEL),
        'diff_w_out': dense(ks[11], (N_DIFF, DIFF_HEADS * DIFF_V_DIM, D_MODEL), DIFF_HEADS * DIFF_V_DIM),
        'diff_lambda': 0.1 * jax.random.normal(ks[12], (N_DIFF, 4, HEAD_DIM), f32),
        'diff_subln_g': 1.0 + 0.02 * jax.random.normal(ks[13], (N_DIFF, DIFF_V_DIM), f32),
        'hgrn_w_in': dense(ks[14], (N_HGRN, D_MODEL, HGRN_IN), D_MODEL),
        'hgrn_w_out': dense(ks[15], (N_HGRN, HGRN_HEADS * HGRN_HEAD_DIM, D_MODEL), HGRN_HEADS * HGRN_HEAD_DIM),
        'hgrn_norm_g': 1.0 + 0.02 * jax.random.normal(ks[16], (N_HGRN, HGRN_HEAD_DIM), f32),
        'dsa_w_in': dense(ks[17], (N_DSA, D_MODEL, DSA_IN), D_MODEL),
        'dsa_w_out': dense(ks[18], (N_DSA, N_HEADS * HEAD_DIM, D_MODEL), N_HEADS * HEAD_DIM),
    }


def reference(x, rel_bias_table, hgrn_lb_logits, norm_g, ffn_w_up, ffn_conv, ffn_w_down,
              swa_w_in, swa_w_out, swa_sinks, diff_w_in, diff_w_out, diff_lambda, diff_subln_g,
              hgrn_w_in, hgrn_w_out, hgrn_norm_g, dsa_w_in, dsa_w_out):
    lb_soft = jax.nn.softmax(hgrn_lb_logits.astype(jnp.float32), axis=0)
    lb_all = jnp.cumsum(lb_soft, axis=0) - lb_soft[0]
    for i in range(DEPTH):
        kind, j = i % N_MIXERS, i // N_MIXERS
        h = rms_norm(x, norm_g[i, 0])
        if kind == 0:
            m = swa_sink_attention(h, swa_w_in[j], swa_w_out[j], swa_sinks[j], rel_bias_table)
        elif kind == 1:
            m = diff_attention(h, diff_w_in[j], diff_w_out[j], diff_lambda[j], diff_subln_g[j],
                               rel_bias_table, i)
        elif kind == 2:
            m = hgrn2_recurrence(h, hgrn_w_in[j], hgrn_w_out[j], lb_all[i], hgrn_norm_g[j])
        else:
            m = dsa_sparse_attention(h, dsa_w_in[j], dsa_w_out[j], rel_bias_table)
        x = x + rms_norm(m, norm_g[i, 1])
        h = rms_norm(x, norm_g[i, 2])
        x = x + rms_norm(conv_ffn(h, ffn_w_up[i], ffn_conv[i], ffn_w_down[i]), norm_g[i, 3])
    return x
```

```python
import functools
import math

import jax
import jax.numpy as jnp
import numpy as np
from jax import lax
from jax.experimental import pallas as pl
from jax.experimental.pallas import tpu as pltpu

D_MODEL = 2048
HEAD_DIM = 64
N_HEADS = 32
N_KV_HEADS = 8
WINDOW = 128
DIFF_HEADS = 16
HGRN_HEADS = 16
HGRN_HEAD_DIM = 128
HGRN_CHUNK = 64
IDX_HEADS = 16
IDX_DIM = 64
TOPK_MAX = 256
D_FF = 5632
CONV_WIDTH = 3
NUM_BUCKETS = 32
MAX_DISTANCE = 128
RMS_EPS = 1e-6

LANES = 128
TQ = 256
VMEM_LIMIT = 56 * 1024 * 1024
NEG_INF = float("-inf")
BF16 = jnp.bfloat16
F32 = jnp.float32


def _params(*sem):
    return pltpu.CompilerParams(dimension_semantics=sem, vmem_limit_bytes=VMEM_LIMIT)


def _dot(a, b):
    return jnp.dot(a, b, preferred_element_type=F32)


def _dot_nt(a, b):
    return lax.dot_general(a, b, (((1,), (1,)), ((), ())), preferred_element_type=F32)


def _dot_tn(a, b):
    return lax.dot_general(a, b, (((0,), (0,)), ((), ())), preferred_element_type=F32)


def _rms(x, g):
    return x * lax.rsqrt(jnp.mean(x * x, axis=-1, keepdims=True) + RMS_EPS) * g


def _silu(x):
    return x * (1.0 / (1.0 + jnp.exp(-x)))


def _norm_cast_kernel(x_ref, g_ref, o_ref):
    o_ref[...] = _rms(x_ref[...], g_ref[...]).astype(o_ref.dtype)


def norm_cast(x, g):
    t, d = x.shape
    tm = 512
    return pl.pallas_call(
        _norm_cast_kernel,
        grid=(t // tm,),
        in_specs=[pl.BlockSpec((tm, d), lambda i: (i, 0)), pl.BlockSpec((1, d), lambda i: (0, 0))],
        out_specs=pl.BlockSpec((tm, d), lambda i: (i, 0)),
        out_shape=jax.ShapeDtypeStruct((t, d), BF16),
        compiler_params=_params("parallel"),
        name="norm_cast",
    )(x, g.reshape(1, d))


def _mm_kernel(x_ref, w_ref, o_ref):
    o_ref[...] = _dot(x_ref[...], w_ref[...]).astype(o_ref.dtype)


def matmul(x, w, out_dtype):
    t, k = x.shape
    n = w.shape[1]
    tm = 1024
    tn = next(c for c in (1024, 512, 256, 128) if n % c == 0)
    return pl.pallas_call(
        _mm_kernel,
        grid=(n // tn, t // tm),
        in_specs=[pl.BlockSpec((tm, k), lambda j, i: (i, 0)), pl.BlockSpec((k, tn), lambda j, i: (0, j))],
        out_specs=pl.BlockSpec((tm, tn), lambda j, i: (i, j)),
        out_shape=jax.ShapeDtypeStruct((t, n), out_dtype),
        compiler_params=_params("parallel", "parallel"),
        name="in_proj",
    )(x, w)


def _proj_res_kernel(a_ref, w_ref, x_ref, go_ref, gn_ref, xo_ref, h_ref, acc_ref):
    kk = pl.program_id(1)

    @pl.when(kk == 0)
    def _():
        acc_ref[...] = jnp.zeros_like(acc_ref)

    acc_ref[...] += _dot(a_ref[...], w_ref[...])

    @pl.when(kk == pl.num_programs(1) - 1)
    def _():
        xn = x_ref[...] + _rms(acc_ref[...], go_ref[...])
        xo_ref[...] = xn
        h_ref[...] = _rms(xn, gn_ref[...]).astype(h_ref.dtype)


def proj_residual(a, w, x, g_out, g_next):
    t, k = a.shape
    d = w.shape[1]
    tm, tk = 512, 512
    return pl.pallas_call(
        _proj_res_kernel,
        grid=(t // tm, k // tk),
        in_specs=[
            pl.BlockSpec((tm, tk), lambda i, kk: (i, kk)),
            pl.BlockSpec((tk, d), lambda i, kk: (kk, 0)),
            pl.BlockSpec((tm, d), lambda i, kk: (i, 0)),
            pl.BlockSpec((1, d), lambda i, kk: (0, 0)),
            pl.BlockSpec((1, d), lambda i, kk: (0, 0)),
        ],
        out_specs=[pl.BlockSpec((tm, d), lambda i, kk: (i, 0)), pl.BlockSpec((tm, d), lambda i, kk: (i, 0))],
        out_shape=[jax.ShapeDtypeStruct((t, d), F32), jax.ShapeDtypeStruct((t, d), BF16)],
        scratch_shapes=[pltpu.VMEM((tm, d), F32)],
        compiler_params=_params("parallel", "arbitrary"),
        name="proj_residual",
    )(a, w, x, g_out.reshape(1, d), g_next.reshape(1, d))


def _ffn_up_kernel(h_ref, wg_ref, wv_ref, cg_ref, cv_ref, o_ref):
    h = h_ref[...]
    s = h.shape[0]
    row = lax.broadcasted_iota(jnp.int32, (s, 1), 0)

    def conv(u, c_ref):
        u1 = jnp.where(row >= 1, pltpu.roll(u, 1, 0), 0.0)
        u2 = jnp.where(row >= 2, pltpu.roll(u, 2, 0), 0.0)
        c = c_ref[...]
        return (c[0:1] * u2 + c[1:2] * u1) + c[2:3] * u

    gate = conv(_dot(h, wg_ref[...]), cg_ref)
    val = conv(_dot(h, wv_ref[...]), cv_ref)
    o_ref[...] = (_silu(gate) * val).astype(o_ref.dtype)


def ffn_up(h, w_up, conv_w, batch):
    t, d = h.shape
    s = t // batch
    tn = 256
    nj = D_FF // tn
    return pl.pallas_call(
        _ffn_up_kernel,
        grid=(batch, nj),
        in_specs=[
            pl.BlockSpec((s, d), lambda b, j: (b, 0)),
            pl.BlockSpec((d, tn), lambda b, j: (0, j)),
            pl.BlockSpec((d, tn), lambda b, j: (0, nj + j)),
            pl.BlockSpec((CONV_WIDTH, tn), lambda b, j: (0, j)),
            pl.BlockSpec((CONV_WIDTH, tn), lambda b, j: (0, nj + j)),
        ],
        out_specs=pl.BlockSpec((s, tn), lambda b, j: (b, j)),
        out_shape=jax.ShapeDtypeStruct((t, D_FF), BF16),
        compiler_params=_params("parallel", "parallel"),
        name="ffn_up",
    )(h, w_up, w_up, conv_w, conv_w)


def _bucket_thresholds():
    max_exact = NUM_BUCKETS // 2
    n = np.arange(1, 2 * MAX_DISTANCE, dtype=np.float64)
    large = max_exact + np.floor(np.log(n / max_exact) / math.log(MAX_DISTANCE / max_exact) * (NUM_BUCKETS - max_exact))
    bucket = np.where(n < max_exact, n, np.minimum(large, NUM_BUCKETS - 1)).astype(np.int64)
    return tuple(int(n[bucket >= b][0]) for b in range(max_exact + 1, NUM_BUCKETS))


BUCKET_THRESHOLDS = _bucket_thresholds()


def _bias_tile_kernel(tab_ref, causal_ref, window_ref):
    m = pl.program_id(0)
    r = lax.broadcasted_iota(jnp.int32, (TQ, 2 * TQ), 0)
    c = lax.broadcasted_iota(jnp.int32, (TQ, 2 * TQ), 1)
    d = r - c + TQ
    n = jnp.maximum(d, 0)
    max_exact = NUM_BUCKETS // 2
    large = jnp.full_like(n, max_exact)
    for thr in BUCKET_THRESHOLDS:
        large = large + (n >= thr).astype(jnp.int32)
    bucket = jnp.where(n < max_exact, n, large)
    val = jnp.zeros((TQ, 2 * TQ), F32)
    for b in range(NUM_BUCKETS):
        val = jnp.where(bucket == b, tab_ref[b, m], val)
    causal = jnp.where(d >= 0, val, NEG_INF)
    causal_ref[0] = causal
    window_ref[0] = jnp.where(d < WINDOW, causal, NEG_INF)


def bias_tiles(table):
    shape = jax.ShapeDtypeStruct((N_HEADS, TQ, 2 * TQ), F32)
    return pl.pallas_call(
        _bias_tile_kernel,
        grid=(N_HEADS,),
        in_specs=[pl.BlockSpec(memory_space=pltpu.SMEM)],
        out_specs=[pl.BlockSpec((1, TQ, 2 * TQ), lambda m: (m, 0, 0))] * 2,
        out_shape=[shape, shape],
        compiler_params=_params("parallel"),
        name="bias_tiles",
    )(table)


def _head_select(x, lo, first):
    zero = jnp.zeros_like(x)
    return jnp.where(lo, x, zero) if first else jnp.where(lo, zero, x)


def _swa_kernel(sink_ref, q_ref, kp_ref, ko_ref, vp_ref, vo_ref, bias_ref, o_ref):
    g = pl.program_id(0)
    i = pl.program_id(2)
    lo = lax.broadcasted_iota(jnp.int32, (TQ, LANES), 1) < HEAD_DIM
    q = q_ref[...] * HEAD_DIM ** -0.5
    parts = []
    for p in range(2):
        qp = q[:, p * LANES:(p + 1) * LANES]
        parts += [_head_select(qp, lo, True), _head_select(qp, lo, False)]
    stack = jnp.concatenate(parts, axis=0)
    kk = jnp.concatenate([kp_ref[...], ko_ref[...]], axis=0)
    s = _dot_nt(stack, kk) + bias_ref[...].reshape(4 * TQ, 2 * TQ)
    col = lax.broadcasted_iota(jnp.int32, (1, 2 * TQ), 1)
    s = jnp.where((col < TQ) & (i == 0), NEG_INF, s)
    sink = jnp.concatenate([jnp.full((TQ, 1), sink_ref[4 * g + hh], F32) for hh in range(4)], axis=0)
    m = jnp.maximum(jnp.max(s, axis=-1, keepdims=True), sink)
    e = jnp.exp(s - m)
    inv = 1.0 / (jnp.sum(e, axis=-1, keepdims=True) + jnp.exp(sink - m))
    e = e.astype(BF16)
    vv = jnp.concatenate([vp_ref[...], vo_ref[...]], axis=0)
    lo2 = lax.broadcasted_iota(jnp.int32, (2 * TQ, LANES), 1) < HEAD_DIM
    v_first, v_second = _head_select(vv, lo2, True), _head_select(vv, lo2, False)
    outs = []
    for p in range(2):
        r0 = 2 * p * TQ
        o_first = _dot(e[r0:r0 + TQ], v_first) * inv[r0:r0 + TQ]
        o_second = _dot(e[r0 + TQ:r0 + 2 * TQ], v_second) * inv[r0 + TQ:r0 + 2 * TQ]
        outs.append(o_first + o_second)
    o_ref[...] = jnp.concatenate(outs, axis=1).astype(o_ref.dtype)


def swa_attention(z, sinks, bias_window, batch):
    t = z.shape[0]
    nq = t // batch // TQ
    q_cols = N_HEADS * HEAD_DIM // (4 * HEAD_DIM)
    k_col0 = N_HEADS * HEAD_DIM // LANES
    v_col0 = k_col0 + N_KV_HEADS

    def own(col0):
        return lambda g, b, i, *_: (b * nq + i, col0 + g)

    def prev(col0):
        return lambda g, b, i, *_: (b * nq + jnp.maximum(i - 1, 0), col0 + g)

    del q_cols
    return pl.pallas_call(
        _swa_kernel,
        grid=(N_KV_HEADS, batch, nq),
        in_specs=[
            pl.BlockSpec(memory_space=pltpu.SMEM),
            pl.BlockSpec((TQ, 4 * HEAD_DIM), lambda g, b, i: (b * nq + i, g)),
            pl.BlockSpec((TQ, LANES), prev(k_col0)),
            pl.BlockSpec((TQ, LANES), own(k_col0)),
            pl.BlockSpec((TQ, LANES), prev(v_col0)),
            pl.BlockSpec((TQ, LANES), own(v_col0)),
            pl.BlockSpec((4, TQ, 2 * TQ), lambda g, b, i: (g, 0, 0)),
        ],
        out_specs=pl.BlockSpec((TQ, 4 * HEAD_DIM), lambda g, b, i: (b * nq + i, g)),
        out_shape=jax.ShapeDtypeStruct((t, N_HEADS * HEAD_DIM), BF16),
        compiler_params=_params("parallel", "parallel", "parallel"),
        name="swa_attention",
    )(sinks, z, z, z, z, z, bias_window)


def _diff_kernel(tab_ref, lam_ref, g_ref, q_ref, k_ref, v_ref, bias_ref, o_ref,
                 s_ref, m_ref, l_ref, acc_ref, *, lam_init):
    h = pl.program_id(0)
    i = pl.program_id(2)
    lo = lax.broadcasted_iota(jnp.int32, (TQ, LANES), 1) < HEAD_DIM
    q = q_ref[...] * HEAD_DIM ** -0.5
    stack = jnp.concatenate([_head_select(q, lo, True), _head_select(q, lo, False)], axis=0)
    last = NUM_BUCKETS - 1
    far_bias = jnp.concatenate([jnp.full((TQ, 1), tab_ref[last, 2 * h], F32),
                                jnp.full((TQ, 1), tab_ref[last, 2 * h + 1], F32)], axis=0)
    m_ref[...] = jnp.full_like(m_ref, NEG_INF)

    def scores(j, bias):
        kj = k_ref[pl.ds(pl.multiple_of(j * TQ, TQ), TQ), :]
        s = _dot_nt(stack, kj) + bias
        s_ref[j] = s
        m_ref[...] = jnp.maximum(m_ref[...], jnp.maximum(s[:, :LANES], s[:, LANES:]))

    def far(j, carry):
        scores(j, far_bias)
        return carry

    lax.fori_loop(0, i - 1, far, 0)

    @pl.when(i >= 1)
    def _():
        scores(i - 1, bias_ref[:, :, 0:TQ].reshape(2 * TQ, TQ))

    scores(i, bias_ref[:, :, TQ:2 * TQ].reshape(2 * TQ, TQ))

    m = jnp.max(m_ref[...], axis=-1, keepdims=True)
    l_ref[...] = jnp.zeros_like(l_ref)
    acc_ref[...] = jnp.zeros_like(acc_ref)

    def pv(j, carry):
        p = jnp.exp(s_ref[j] - m)
        l_ref[...] += p[:, :LANES] + p[:, LANES:]
        vj = v_ref[pl.ds(pl.multiple_of(j * TQ, TQ), TQ), :]
        acc_ref[...] += _dot(p.astype(BF16), vj)
        return carry

    lax.fori_loop(0, i + 1, pv, 0)

    o12 = acc_ref[...] * (1.0 / jnp.sum(l_ref[...], axis=-1, keepdims=True))
    lf = lam_ref[...]
    lam = (jnp.exp(jnp.sum(lf[0:1] * lf[1:2], axis=-1, keepdims=True))
           - jnp.exp(jnp.sum(lf[2:3] * lf[3:4], axis=-1, keepdims=True)) + lam_init)
    o = o12[:TQ] - lam * o12[TQ:]
    o_ref[...] = (_rms(o, g_ref[...]) * (1.0 - lam_init)).astype(o_ref.dtype)


def diff_attention(z, table, lambdas, subln_g, bias_causal, batch, layer_idx):
    t = z.shape[0]
    s = t // batch
    nq = s // TQ
    lam_init = 0.8 - 0.6 * math.exp(-0.3 * layer_idx)
    return pl.pallas_call(
        functools.partial(_diff_kernel, lam_init=lam_init),
        grid=(DIFF_HEADS, batch, nq),
        in_specs=[
            pl.BlockSpec(memory_space=pltpu.SMEM),
            pl.BlockSpec((4, HEAD_DIM), lambda h, b, i: (0, 0)),
            pl.BlockSpec((1, LANES), lambda h, b, i: (0, 0)),
            pl.BlockSpec((TQ, LANES), lambda h, b, i: (b * nq + i, h)),
            pl.BlockSpec((s, LANES), lambda h, b, i: (b, DIFF_HEADS + h)),
            pl.BlockSpec((s, LANES), lambda h, b, i: (b, 2 * DIFF_HEADS + h)),
            pl.BlockSpec((2, TQ, 2 * TQ), lambda h, b, i: (h, 0, 0)),
        ],
        out_specs=pl.BlockSpec((TQ, LANES), lambda h, b, i: (b * nq + i, h)),
        out_shape=jax.ShapeDtypeStruct((t, DIFF_HEADS * 2 * HEAD_DIM), BF16),
        scratch_shapes=[
            pltpu.VMEM((nq, 2 * TQ, TQ), F32),
            pltpu.VMEM((2 * TQ, LANES), F32),
            pltpu.VMEM((2 * TQ, LANES), F32),
            pltpu.VMEM((2 * TQ, LANES), F32),
        ],
        compiler_params=_params("parallel", "parallel", "arbitrary"),
        name="diff_attention",
    )(table, lambdas, subln_g.reshape(1, LANES), z, z, z, bias_causal)


HGRN_LEVELS = tuple(2 ** e for e in range(int(math.log2(HGRN_CHUNK)) - 1, -1, -1))


def _hgrn_kernel(lb_ref, g_ref, q_ref, f_ref, i_ref, og_ref, o_ref,
                 qf_ref, kf_ref, bq_ref, kd_ref, dec_ref, v_ref, st_ref):
    s_len = q_ref.shape[0]
    c_len = HGRN_CHUNK
    n_lvl = len(HGRN_LEVELS)
    rc = lax.broadcasted_iota(jnp.int32, (s_len, 1), 0) % c_len

    z = f_ref[...]
    lb = lb_ref[...]
    log_sig = jnp.minimum(z, 0.0) - jnp.log1p(jnp.exp(-jnp.abs(z)))
    a = jnp.log(lb)
    c = jnp.log1p(-lb) + log_sig
    delta = a - c
    log_f = jnp.where(jnp.isnan(delta), a + c,
                      jnp.maximum(a, c) + jnp.log1p(jnp.exp(-jnp.abs(delta))))
    key = (1.0 - lb) * (1.0 / (1.0 + jnp.exp(z)))
    qs = _silu(q_ref[...])
    v_ref[...] = i_ref[...].astype(BF16)

    b = log_f
    for d in (1, 2, 4, 8, 16, 32):
        b = b + jnp.where(rc >= d, pltpu.roll(b, d, 0), 0.0)

    bq_ref[...] = (qs * jnp.exp(b)).astype(BF16)
    qf_ref[n_lvl] = qs.astype(BF16)
    kf_ref[n_lvl] = key.astype(BF16)

    r_m = jnp.where(rc >= 1, pltpu.roll(b, 1, 0), 0.0)
    for m in (1, 2, 4, 8, 16, 32):
        if m > 1:
            half = m // 2
            r_m = jnp.where((rc // half) % 2 == 1, pltpu.roll(r_m, half, 0), r_m)
        e_m = pltpu.roll(r_m, s_len - m, 0)
        lvl = HGRN_LEVELS.index(m)
        qf_ref[lvl] = (qs * jnp.exp(jnp.minimum(b - r_m, 0.0))).astype(BF16)
        kf_ref[lvl] = (key * jnp.exp(jnp.minimum(e_m - b, 0.0))).astype(BF16)
    b_end = jnp.where(rc == c_len - 1, b, 0.0)
    for d in (1, 2, 4, 8, 16, 32):
        b_end = b_end + jnp.where(rc < c_len - d, pltpu.roll(b_end, s_len - d, 0), 0.0)
    kd_ref[...] = (key * jnp.exp(jnp.minimum(b_end - b, 0.0))).astype(BF16)
    dec_ref[...] = jnp.exp(b_end)

    ti = lax.broadcasted_iota(jnp.int32, (c_len, c_len), 0)
    si = lax.broadcasted_iota(jnp.int32, (c_len, c_len), 1)
    masks = [((ti // m) % 2 == 1) & (si // m == ti // m - 1) for m in HGRN_LEVELS] + [ti == si]

    st_ref[...] = jnp.zeros_like(st_ref)
    g = g_ref[...]

    def chunk(ci, carry):
        r0 = pl.multiple_of(ci * c_len, c_len)
        rows = pl.ds(r0, c_len)
        att = jnp.zeros((c_len, c_len), F32)
        for lvl in range(n_lvl + 1):
            att = att + jnp.where(masks[lvl], _dot_nt(qf_ref[lvl, rows, :], kf_ref[lvl, rows, :]), 0.0)
        vc = v_ref[rows, :]
        st = st_ref[...]
        o = _dot(att.astype(BF16), vc) + _dot_nt(bq_ref[rows, :], st.astype(BF16))
        st_ref[...] = st * dec_ref[pl.ds(r0, 1), :] + _dot_tn(vc, kd_ref[rows, :])
        o_ref[rows, :] = (_rms(o, g) * _silu(og_ref[rows, :])).astype(o_ref.dtype)
        return carry

    lax.fori_loop(0, s_len // c_len, chunk, 0)


def hgrn_recurrence(z, lb, norm_g, batch):
    t = z.shape[0]
    s = t // batch
    hd = HGRN_HEAD_DIM
    n_lvl = len(HGRN_LEVELS)

    def part(p):
        return pl.BlockSpec((s, hd), lambda b, h: (b, p * HGRN_HEADS + h))

    return pl.pallas_call(
        _hgrn_kernel,
        grid=(batch, HGRN_HEADS),
        in_specs=[pl.BlockSpec((1, hd), lambda b, h: (0, h)), pl.BlockSpec((1, hd), lambda b, h: (0, 0)),
                  part(0), part(1), part(2), part(3)],
        out_specs=pl.BlockSpec((s, hd), lambda b, h: (b, h)),
        out_shape=jax.ShapeDtypeStruct((t, HGRN_HEADS * hd), BF16),
        scratch_shapes=[
            pltpu.VMEM((n_lvl + 1, s, hd), BF16),
            pltpu.VMEM((n_lvl + 1, s, hd), BF16),
            pltpu.VMEM((s, hd), BF16),
            pltpu.VMEM((s, hd), BF16),
            pltpu.VMEM((s, hd), F32),
            pltpu.VMEM((s, hd), BF16),
            pltpu.VMEM((hd, hd), F32),
        ],
        compiler_params=_params("parallel", "parallel"),
        name="hgrn_recurrence",
    )(lb.reshape(1, -1), norm_g.reshape(1, hd), z, z, z, z)


def _sortable(score):
    score = jnp.where(score == 0.0, 0.0, score)
    bits = lax.bitcast_convert_type(score, jnp.int32)
    return bits ^ ((bits >> 31) & jnp.int32(0x7FFFFFFF))


def _dsa_kernel(tab_ref, q_ref, k_ref, v_ref, qi_ref, ki_ref, wi_ref, bias_ref, o_ref,
                qsel_ref, keys_ref, mask_ref, s_ref, m_ref, l_ref, acc_ref, *, n_sel):
    i = pl.program_id(1)
    g = pl.program_id(2)
    lo = lax.broadcasted_iota(jnp.int32, (TQ, LANES), 1) < HEAD_DIM

    @pl.when(g == 0)
    def _select():
        for p in range(IDX_HEADS // 2):
            qp = qi_ref[:, p * LANES:(p + 1) * LANES]
            qsel_ref[2 * p] = _head_select(qp, lo, True)
            qsel_ref[2 * p + 1] = _head_select(qp, lo, False)
        w_t = wi_ref[...].T * (IDX_HEADS ** -0.5 * IDX_DIM ** -0.5)
        key_i = lax.broadcasted_iota(jnp.int32, (TQ, TQ), 0)
        qry_i = lax.broadcasted_iota(jnp.int32, (TQ, TQ), 1)

        def score_block(j, carry):
            kij = ki_ref[pl.ds(pl.multiple_of(j * TQ, TQ), TQ), :].astype(BF16)
            score = jnp.zeros((TQ, TQ), F32)
            for hh in range(IDX_HEADS):
                logit = _dot_nt(kij, qsel_ref[hh])
                score = score + jnp.maximum(logit, 0.0) * w_t[hh:hh + 1, :]
            score = jnp.where((j < i) | (key_i <= qry_i), score, NEG_INF)
            keys_ref[j] = _sortable(score)
            return carry

        lax.fori_loop(0, i + 1, score_block, 0)

        def count(pred):
            def body(j, cnt):
                return cnt + jnp.sum(pred(keys_ref[j]).astype(jnp.int32), axis=0, keepdims=True)
            return lax.fori_loop(0, i + 1, body, jnp.zeros((1, TQ), jnp.int32))

        def search(it, thr):
            trial = thr ^ (jnp.int32(1) << (31 - it))
            cnt = count(lambda kj: kj >= trial)
            return jnp.where(cnt >= n_sel, trial, thr)

        thr = lax.fori_loop(0, 32, search, jnp.full((1, TQ), jnp.iinfo(jnp.int32).min, jnp.int32))
        need = (n_sel - count(lambda kj: kj > thr)).astype(F32)
        earlier = (qry_i < key_i).astype(BF16)

        def mask_block(j, base):
            kj = keys_ref[j]
            eq = kj == thr
            eq_f = jnp.where(eq, 1.0, 0.0)
            rank = base + _dot(earlier, eq_f.astype(BF16))
            sel = (kj > thr) | (eq & (rank < need))
            mask_ref[j] = jnp.where(sel, 0.0, NEG_INF).T
            return base + jnp.sum(eq_f, axis=0, keepdims=True)

        lax.fori_loop(0, i + 1, mask_block, jnp.zeros((1, TQ), F32))

    q = q_ref[...] * HEAD_DIM ** -0.5
    parts = []
    for p in range(2):
        qp = q[:, p * LANES:(p + 1) * LANES]
        parts += [_head_select(qp, lo, True), _head_select(qp, lo, False)]
    stack = jnp.concatenate(parts, axis=0)
    last = NUM_BUCKETS - 1
    far_bias = jnp.concatenate([jnp.full((TQ, 1), tab_ref[last, 4 * g + hh], F32) for hh in range(4)], axis=0)
    m_ref[...] = jnp.full_like(m_ref, NEG_INF)

    def scores(j, bias):
        kj = k_ref[pl.ds(pl.multiple_of(j * TQ, TQ), TQ), :]
        msk = mask_ref[j]
        s = _dot_nt(stack, kj) + bias + jnp.concatenate([msk] * 4, axis=0)
        s_ref[j] = s
        m_ref[...] = jnp.maximum(m_ref[...], jnp.maximum(s[:, :LANES], s[:, LANES:]))

    def far(j, carry):
        scores(j, far_bias)
        return carry

    lax.fori_loop(0, i - 1, far, 0)

    @pl.when(i >= 1)
    def _():
        scores(i - 1, bias_ref[:, :, 0:TQ].reshape(4 * TQ, TQ))

    scores(i, bias_ref[:, :, TQ:2 * TQ].reshape(4 * TQ, TQ))

    m = jnp.max(m_ref[...], axis=-1, keepdims=True)
    l_ref[...] = jnp.zeros_like(l_ref)
    acc_ref[...] = jnp.zeros_like(acc_ref)

    def pv(j, carry):
        p = jnp.exp(s_ref[j] - m)
        l_ref[...] += p[:, :LANES] + p[:, LANES:]
        p = p.astype(BF16)
        vj = v_ref[pl.ds(pl.multiple_of(j * TQ, TQ), TQ), :]
        v_first, v_second = _head_select(vj, lo, True), _head_select(vj, lo, False)
        for pr in range(2):
            r0 = 2 * pr * TQ
            acc_ref[pr * TQ:(pr + 1) * TQ, :] += (_dot(p[r0:r0 + TQ], v_first)
                                                  + _dot(p[r0 + TQ:r0 + 2 * TQ], v_second))
        return carry

    lax.fori_loop(0, i + 1, pv, 0)

    inv = 1.0 / jnp.sum(l_ref[...], axis=-1, keepdims=True)
    outs = []
    for pr in range(2):
        r0 = 2 * pr * TQ
        scale = jnp.where(lo, inv[r0:r0 + TQ], inv[r0 + TQ:r0 + 2 * TQ])
        outs.append(acc_ref[pr * TQ:(pr + 1) * TQ, :] * scale)
    o_ref[...] = jnp.concatenate(outs, axis=1).astype(o_ref.dtype)


def dsa_attention(z, z_idx, table, bias_causal, batch):
    t = z.shape[0]
    s = t // batch
    nq = s // TQ
    n_sel = min(TOPK_MAX, s // 4)
    k_col0 = N_HEADS * HEAD_DIM // LANES
    v_col0 = k_col0 + N_KV_HEADS
    qi_col = (N_HEADS * HEAD_DIM + 4 * N_KV_HEADS * HEAD_DIM) // (IDX_HEADS * IDX_DIM)
    return pl.pallas_call(
        functools.partial(_dsa_kernel, n_sel=n_sel),
        grid=(batch, nq, N_KV_HEADS),
        in_specs=[
            pl.BlockSpec(memory_space=pltpu.SMEM),
            pl.BlockSpec((TQ, 4 * HEAD_DIM), lambda b, i, g: (b * nq + i, g)),
            pl.BlockSpec((s, LANES), lambda b, i, g: (b, k_col0 + g)),
            pl.BlockSpec((s, LANES), lambda b, i, g: (b, v_col0 + g)),
            pl.BlockSpec((TQ, IDX_HEADS * IDX_DIM), lambda b, i, g: (b * nq + i, qi_col)),
            pl.BlockSpec((s, LANES), lambda b, i, g: (b, 0)),
            pl.BlockSpec((TQ, LANES), lambda b, i, g: (b * nq + i, 1)),
            pl.BlockSpec((4, TQ, 2 * TQ), lambda b, i, g: (g, 0, 0)),
        ],
        out_specs=pl.BlockSpec((TQ, 4 * HEAD_DIM), lambda b, i, g: (b * nq + i, g)),
        out_shape=jax.ShapeDtypeStruct((t, N_HEADS * HEAD_DIM), BF16),
        scratch_shapes=[
            pltpu.VMEM((IDX_HEADS, TQ, LANES), BF16),
            pltpu.VMEM((nq, TQ, TQ), jnp.int32),
            pltpu.VMEM((nq, TQ, TQ), F32),
            pltpu.VMEM((nq, 4 * TQ, TQ), F32),
            pltpu.VMEM((4 * TQ, LANES), F32),
            pltpu.VMEM((4 * TQ, LANES), F32),
            pltpu.VMEM((2 * TQ, LANES), F32),
        ],
        compiler_params=_params("parallel", "parallel", "arbitrary"),
        name="dsa_attention",
    )(table, z, z, z, z, z_idx, z_idx, bias_causal)


def _lower_bound_kernel(x_ref, o_ref):
    x = x_ref[...]
    e = jnp.exp(x - jnp.max(x, axis=0, keepdims=True))
    soft = e / jnp.sum(e, axis=0, keepdims=True)
    run = soft[0:1]
    o_ref[0:1, :] = run - soft[0:1]
    for r in range(1, x.shape[0]):
        run = run + soft[r:r + 1]
        o_ref[r:r + 1, :] = run - soft[0:1]


def hgrn_lower_bounds(logits):
    return pl.pallas_call(
        _lower_bound_kernel,
        out_shape=jax.ShapeDtypeStruct(logits.shape, F32),
        name="hgrn_lower_bounds",
    )(logits)


def kernel(x, rel_bias_table, hgrn_lb_logits, norm_g, ffn_w_up, ffn_conv, ffn_w_down, swa_w_in, swa_w_out,
           swa_sinks, diff_w_in, diff_w_out, diff_lambda, diff_subln_g, hgrn_w_in, hgrn_w_out, hgrn_norm_g,
           dsa_w_in, dsa_w_out):
    batch, seq, d = x.shape
    t = batch * seq
    depth = norm_g.shape[0]
    n_mixers = 4
    bias_causal, bias_window = bias_tiles(rel_bias_table)
    lb_all = hgrn_lower_bounds(hgrn_lb_logits)
    kv = N_KV_HEADS * HEAD_DIM
    x2 = x.reshape(t, d)
    h = norm_cast(x2, norm_g[0, 0])
    for i in range(depth):
        kind, j = i % n_mixers, i // n_mixers
        if kind == 0:
            wq, wk, wv = jnp.split(swa_w_in[j], [d, d + kv], axis=1)
            w_in = jnp.concatenate([wq, _dup_heads(wk, N_KV_HEADS), _dup_heads(wv, N_KV_HEADS)], axis=1)
            z = matmul(h, w_in.astype(BF16), BF16)
            o = swa_attention(z, swa_sinks[j], bias_window, batch)
            w_out = swa_w_out[j]
        elif kind == 1:
            z = matmul(h, diff_w_in[j].astype(BF16), BF16)
            o = diff_attention(z, rel_bias_table, diff_lambda[j], diff_subln_g[j], bias_causal, batch, i)
            w_out = diff_w_out[j]
        elif kind == 2:
            z = matmul(h, hgrn_w_in[j].astype(BF16), F32)
            o = hgrn_recurrence(z, lb_all[i], hgrn_norm_g[j], batch)
            w_out = hgrn_w_out[j]
        else:
            n_qi = IDX_HEADS * IDX_DIM
            wq, wk, wv, wqi, wki, wwi = jnp.split(
                dsa_w_in[j], [d, d + kv, d + 2 * kv, d + 2 * kv + n_qi, d + 2 * kv + n_qi + IDX_DIM], axis=1)
            w_main = jnp.concatenate([wq, _dup_heads(wk, N_KV_HEADS), _dup_heads(wv, N_KV_HEADS), wqi], axis=1)
            w_idx = jnp.concatenate([wki, wki, wwi, jnp.zeros((d, LANES - IDX_HEADS), F32)], axis=1)
            z = matmul(h, w_main.astype(BF16), BF16)
            z_idx = matmul(h, w_idx.astype(BF16), F32)
            o = dsa_attention(z, z_idx, rel_bias_table, bias_causal, batch)
            w_out = dsa_w_out[j]
        x2, h = proj_residual(o, w_out.astype(BF16), x2, norm_g[i, 1], norm_g[i, 2])
        a = ffn_up(h, ffn_w_up[i].astype(BF16), ffn_conv[i], batch)
        x2, h = proj_residual(a, ffn_w_down[i].astype(BF16), x2, norm_g[i, 3], norm_g[(i + 1) % depth, 0])
    return x2.reshape(batch, seq, d)


def _dup_heads(w, n_heads):
    k = w.shape[0]
    w = w.reshape(k, n_heads, 1, HEAD_DIM)
    return jnp.broadcast_to(w, (k, n_heads, 2, HEAD_DIM)).reshape(k, n_heads * 2 * HEAD_DIM)
```

```python
import functools
import math

import jax
import jax.numpy as jnp
import numpy as np
from jax import lax
from jax.experimental import pallas as pl
from jax.experimental.pallas import tpu as pltpu

D_MODEL = 2048
HEAD_DIM = 64
N_HEADS = 32
N_KV_HEADS = 8
WINDOW = 128
DIFF_HEADS = 16
HGRN_HEADS = 16
HGRN_HEAD_DIM = 128
HGRN_CHUNK = 64
IDX_HEADS = 16
IDX_DIM = 64
TOPK_MAX = 256
D_FF = 5632
CONV_WIDTH = 3
NUM_BUCKETS = 32
MAX_DISTANCE = 128
RMS_EPS = 1e-6

LANES = 128
TQ = 256
VMEM_LIMIT = 56 * 1024 * 1024
NEG_INF = float("-inf")
BF16 = jnp.bfloat16
F32 = jnp.float32


def _params(*sem):
    return pltpu.CompilerParams(dimension_semantics=sem, vmem_limit_bytes=VMEM_LIMIT)


def _dot(a, b):
    return jnp.dot(a, b, preferred_element_type=F32)


def _dot_nt(a, b):
    return lax.dot_general(a, b, (((1,), (1,)), ((), ())), preferred_element_type=F32)


def _dot_tn(a, b):
    return lax.dot_general(a, b, (((0,), (0,)), ((), ())), preferred_element_type=F32)


def _rms(x, g):
    return x * lax.rsqrt(jnp.mean(x * x, axis=-1, keepdims=True) + RMS_EPS) * g


def _silu(x):
    return x * (1.0 / (1.0 + jnp.exp(-x)))


def _norm_cast_kernel(x_ref, g_ref, o_ref):
    o_ref[...] = _rms(x_ref[...], g_ref[...]).astype(o_ref.dtype)


def norm_cast(x, g):
    t, d = x.shape
    tm = 512
    return pl.pallas_call(
        _norm_cast_kernel,
        grid=(t // tm,),
        in_specs=[pl.BlockSpec((tm, d), lambda i: (i, 0)), pl.BlockSpec((1, d), lambda i: (0, 0))],
        out_specs=pl.BlockSpec((tm, d), lambda i: (i, 0)),
        out_shape=jax.ShapeDtypeStruct((t, d), BF16),
        compiler_params=_params("parallel"),
        name="norm_cast",
    )(x, g.reshape(1, d))


def _mm_kernel(x_ref, w_ref, o_ref):
    o_ref[...] = _dot(x_ref[...], w_ref[...]).astype(o_ref.dtype)


def matmul(x, w, out_dtype):
    t, k = x.shape
    n = w.shape[1]
    tm = 1024
    tn = next(c for c in (1024, 512, 256, 128) if n % c == 0)
    return pl.pallas_call(
        _mm_kernel,
        grid=(n // tn, t // tm),
        in_specs=[pl.BlockSpec((tm, k), lambda j, i: (i, 0)), pl.BlockSpec((k, tn), lambda j, i: (0, j))],
        out_specs=pl.BlockSpec((tm, tn), lambda j, i: (i, j)),
        out_shape=jax.ShapeDtypeStruct((t, n), out_dtype),
        compiler_params=_params("parallel", "parallel"),
        name="in_proj",
    )(x, w)


def _proj_res_kernel(a_ref, w_ref, x_ref, go_ref, gn_ref, xo_ref, h_ref, acc_ref):
    kk = pl.program_id(1)

    @pl.when(kk == 0)
    def _():
        acc_ref[...] = jnp.zeros_like(acc_ref)

    acc_ref[...] += _dot(a_ref[...], w_ref[...])

    @pl.when(kk == pl.num_programs(1) - 1)
    def _():
        xn = x_ref[...] + _rms(acc_ref[...], go_ref[...])
        xo_ref[...] = xn
        h_ref[...] = _rms(xn, gn_ref[...]).astype(h_ref.dtype)


def proj_residual(a, w, x, g_out, g_next):
    t, k = a.shape
    d = w.shape[1]
    tm, tk = 512, 512
    return pl.pallas_call(
        _proj_res_kernel,
        grid=(t // tm, k // tk),
        in_specs=[
            pl.BlockSpec((tm, tk), lambda i, kk: (i, kk)),
            pl.BlockSpec((tk, d), lambda i, kk: (kk, 0)),
            pl.BlockSpec((tm, d), lambda i, kk: (i, 0)),
            pl.BlockSpec((1, d), lambda i, kk: (0, 0)),
            pl.BlockSpec((1, d), lambda i, kk: (0, 0)),
        ],
        out_specs=[pl.BlockSpec((tm, d), lambda i, kk: (i, 0)), pl.BlockSpec((tm, d), lambda i, kk: (i, 0))],
        out_shape=[jax.ShapeDtypeStruct((t, d), F32), jax.ShapeDtypeStruct((t, d), BF16)],
        scratch_shapes=[pltpu.VMEM((tm, d), F32)],
        compiler_params=_params("parallel", "arbitrary"),
        name="proj_residual",
    )(a, w, x, g_out.reshape(1, d), g_next.reshape(1, d))


def _ffn_up_kernel(h_ref, wg_ref, wv_ref, cg_ref, cv_ref, o_ref):
    h = h_ref[...]
    s = h.shape[0]
    row = lax.broadcasted_iota(jnp.int32, (s, 1), 0)

    def conv(u, c_ref):
        u1 = jnp.where(row >= 1, pltpu.roll(u, 1, 0), 0.0)
        u2 = jnp.where(row >= 2, pltpu.roll(u, 2, 0), 0.0)
        c = c_ref[...]
        return (c[0:1] * u2 + c[1:2] * u1) + c[2:3] * u

    gate = conv(_dot(h, wg_ref[...]), cg_ref)
    val = conv(_dot(h, wv_ref[...]), cv_ref)
    o_ref[...] = (_silu(gate) * val).astype(o_ref.dtype)


def ffn_up(h, w_up, conv_w, batch):
    t, d = h.shape
    s = t // batch
    tn = 256
    nj = D_FF // tn
    return pl.pallas_call(
        _ffn_up_kernel,
        grid=(batch, nj),
        in_specs=[
            pl.BlockSpec((s, d), lambda b, j: (b, 0)),
            pl.BlockSpec((d, tn), lambda b, j: (0, j)),
            pl.BlockSpec((d, tn), lambda b, j: (0, nj + j)),
            pl.BlockSpec((CONV_WIDTH, tn), lambda b, j: (0, j)),
            pl.BlockSpec((CONV_WIDTH, tn), lambda b, j: (0, nj + j)),
        ],
        out_specs=pl.BlockSpec((s, tn), lambda b, j: (b, j)),
        out_shape=jax.ShapeDtypeStruct((t, D_FF), BF16),
        compiler_params=_params("parallel", "parallel"),
        name="ffn_up",
    )(h, w_up, w_up, conv_w, conv_w)


def _bucket_thresholds():
    max_exact = NUM_BUCKETS // 2
    n = np.arange(1, 2 * MAX_DISTANCE, dtype=np.float64)
    large = max_exact + np.floor(np.log(n / max_exact) / math.log(MAX_DISTANCE / max_exact) * (NUM_BUCKETS - max_exact))
    bucket = np.where(n < max_exact, n, np.minimum(large, NUM_BUCKETS - 1)).astype(np.int64)
    return tuple(int(n[bucket >= b][0]) for b in range(max_exact + 1, NUM_BUCKETS))


BUCKET_THRESHOLDS = _bucket_thresholds()


def _bias_tile_kernel(tab_ref, causal_ref, window_ref):
    m = pl.program_id(0)
    r = lax.broadcasted_iota(jnp.int32, (TQ, 2 * TQ), 0)
    c = lax.broadcasted_iota(jnp.int32, (TQ, 2 * TQ), 1)
    d = r - c + TQ
    n = jnp.maximum(d, 0)
    max_exact = NUM_BUCKETS // 2
    large = jnp.full_like(n, max_exact)
    for thr in BUCKET_THRESHOLDS:
        large = large + (n >= thr).astype(jnp.int32)
    bucket = jnp.where(n < max_exact, n, large)
    val = jnp.zeros((TQ, 2 * TQ), F32)
    for b in range(NUM_BUCKETS):
        val = jnp.where(bucket == b, tab_ref[b, m], val)
    causal = jnp.where(d >= 0, val, NEG_INF)
    causal_ref[0] = causal
    window_ref[0] = jnp.where(d < WINDOW, causal, NEG_INF)


def bias_tiles(table):
    shape = jax.ShapeDtypeStruct((N_HEADS, TQ, 2 * TQ), F32)
    return pl.pallas_call(
        _bias_tile_kernel,
        grid=(N_HEADS,),
        in_specs=[pl.BlockSpec(memory_space=pltpu.SMEM)],
        out_specs=[pl.BlockSpec((1, TQ, 2 * TQ), lambda m: (m, 0, 0))] * 2,
        out_shape=[shape, shape],
        compiler_params=_params("parallel"),
        name="bias_tiles",
    )(table)


def _head_select(x, lo, first):
    zero = jnp.zeros_like(x)
    return jnp.where(lo, x, zero) if first else jnp.where(lo, zero, x)


def _stack_heads(q):
    rows = q.shape[0]
    lo = lax.broadcasted_iota(jnp.int32, (rows, LANES), 1) < HEAD_DIM
    parts = []
    for p in range(q.shape[1] // LANES):
        qp = q[:, p * LANES:(p + 1) * LANES]
        parts += [_head_select(qp, lo, True), _head_select(qp, lo, False)]
    return jnp.concatenate(parts, axis=0)


SWA_BLOCK = WINDOW


def _swa_kernel(sink_ref, q_ref, kp_ref, ko_ref, vp_ref, vo_ref, bias_ref, o_ref):
    i = pl.program_id(1)
    blk = SWA_BLOCK
    col = lax.broadcasted_iota(jnp.int32, (1, 2 * blk), 1)
    no_prev = (col < blk) & (i == 0)
    lo2 = lax.broadcasted_iota(jnp.int32, (2 * blk, LANES), 1) < HEAD_DIM
    outs = []
    for g in range(N_KV_HEADS):
        q = q_ref[:, g * 4 * HEAD_DIM:(g + 1) * 4 * HEAD_DIM] * HEAD_DIM ** -0.5
        stack = _stack_heads(q)
        cols = slice(g * LANES, (g + 1) * LANES)
        kk = jnp.concatenate([kp_ref[:, cols], ko_ref[:, cols]], axis=0)
        s = _dot_nt(stack, kk) + bias_ref[4 * g:4 * g + 4].reshape(4 * blk, 2 * blk)
        s = jnp.where(no_prev, NEG_INF, s)
        sink = jnp.concatenate([jnp.full((blk, 1), sink_ref[4 * g + hh], F32) for hh in range(4)], axis=0)
        m = jnp.maximum(jnp.max(s, axis=-1, keepdims=True), sink)
        e = jnp.exp(s - m)
        inv = 1.0 / (jnp.sum(e, axis=-1, keepdims=True) + jnp.exp(sink - m))
        e = e.astype(BF16)
        vv = jnp.concatenate([vp_ref[:, cols], vo_ref[:, cols]], axis=0)
        v_first, v_second = _head_select(vv, lo2, True), _head_select(vv, lo2, False)
        for p in range(2):
            r0 = 2 * p * blk
            o_first = _dot(e[r0:r0 + blk], v_first) * inv[r0:r0 + blk]
            o_second = _dot(e[r0 + blk:r0 + 2 * blk], v_second) * inv[r0 + blk:r0 + 2 * blk]
            outs.append(o_first + o_second)
    o_ref[...] = jnp.concatenate(outs, axis=1).astype(o_ref.dtype)


def swa_attention(z, sinks, bias_window, batch):
    t = z.shape[0]
    blk = SWA_BLOCK
    nq = t // batch // blk
    dq = N_HEADS * HEAD_DIM
    dk = 2 * N_KV_HEADS * HEAD_DIM
    k_col, v_col = dq // dk, dq // dk + 1

    def own(col):
        return lambda b, i: (b * nq + i, col)

    def prev(col):
        return lambda b, i: (b * nq + jnp.maximum(i - 1, 0), col)

    bias_spec = pl.BlockSpec((N_HEADS, blk, 2 * blk), lambda b, i: (0, TQ // blk - 1, TQ // blk - 1))
    return pl.pallas_call(
        _swa_kernel,
        grid=(batch, nq),
        in_specs=[
            pl.BlockSpec(memory_space=pltpu.SMEM),
            pl.BlockSpec((blk, dq), own(0)),
            pl.BlockSpec((blk, dk), prev(k_col)),
            pl.BlockSpec((blk, dk), own(k_col)),
            pl.BlockSpec((blk, dk), prev(v_col)),
            pl.BlockSpec((blk, dk), own(v_col)),
            bias_spec,
        ],
        out_specs=pl.BlockSpec((blk, dq), own(0)),
        out_shape=jax.ShapeDtypeStruct((t, dq), BF16),
        compiler_params=_params("parallel", "parallel"),
        name="swa_attention",
    )(sinks, z, z, z, z, z, bias_window)


def _causal_scores(stack, k_ref, kv_len, bias_ref, far_bias):
    rows = stack.shape[0]
    s = _dot_nt(stack, k_ref[0:kv_len, :])
    if kv_len == TQ:
        return s + bias_ref[:, :, TQ:2 * TQ].reshape(rows, TQ)
    near = bias_ref[...].reshape(rows, 2 * TQ)
    if kv_len == 2 * TQ:
        return s + near
    n_far = kv_len - 2 * TQ
    return jnp.concatenate([s[:, :n_far] + far_bias, s[:, n_far:] + near], axis=1)


def _diff_kernel(tab_ref, lam_ref, g_ref, q_ref, k_ref, v_ref, bias_ref, o_ref, *, lam_init):
    h = pl.program_id(0)
    last = NUM_BUCKETS - 1
    far_bias = jnp.concatenate([jnp.full((TQ, 1), tab_ref[last, 2 * h], F32),
                                jnp.full((TQ, 1), tab_ref[last, 2 * h + 1], F32)], axis=0)
    lf = lam_ref[...]
    lam = (jnp.exp(jnp.sum(lf[0:1] * lf[1:2], axis=-1, keepdims=True))
           - jnp.exp(jnp.sum(lf[2:3] * lf[3:4], axis=-1, keepdims=True)) + lam_init)
    for c in range(q_ref.shape[0] // TQ):
        rows = slice(c * TQ, (c + 1) * TQ)
        kv_len = (c + 1) * TQ
        stack = _stack_heads(q_ref[rows, :] * HEAD_DIM ** -0.5)
        s = _causal_scores(stack, k_ref, kv_len, bias_ref, far_bias)
        p = jnp.exp(s - jnp.max(s, axis=-1, keepdims=True))
        inv = 1.0 / jnp.sum(p, axis=-1, keepdims=True)
        o12 = _dot(p.astype(BF16), v_ref[0:kv_len, :]) * inv
        o = o12[:TQ] - lam * o12[TQ:]
        o_ref[rows, :] = (_rms(o, g_ref[...]) * (1.0 - lam_init)).astype(o_ref.dtype)


def diff_attention(z, table, lambdas, subln_g, bias_causal, batch, layer_idx):
    t = z.shape[0]
    s = t // batch
    lam_init = 0.8 - 0.6 * math.exp(-0.3 * layer_idx)
    return pl.pallas_call(
        functools.partial(_diff_kernel, lam_init=lam_init),
        grid=(DIFF_HEADS, batch),
        in_specs=[
            pl.BlockSpec(memory_space=pltpu.SMEM),
            pl.BlockSpec((4, HEAD_DIM), lambda h, b: (0, 0)),
            pl.BlockSpec((1, LANES), lambda h, b: (0, 0)),
            pl.BlockSpec((s, LANES), lambda h, b: (b, h)),
            pl.BlockSpec((s, LANES), lambda h, b: (b, DIFF_HEADS + h)),
            pl.BlockSpec((s, LANES), lambda h, b: (b, 2 * DIFF_HEADS + h)),
            pl.BlockSpec((2, TQ, 2 * TQ), lambda h, b: (h, 0, 0)),
        ],
        out_specs=pl.BlockSpec((s, LANES), lambda h, b: (b, h)),
        out_shape=jax.ShapeDtypeStruct((t, DIFF_HEADS * 2 * HEAD_DIM), BF16),
        compiler_params=_params("parallel", "parallel"),
        name="diff_attention",
    )(table, lambdas, subln_g.reshape(1, LANES), z, z, z, bias_causal)


HGRN_LEVELS = tuple(2 ** e for e in range(int(math.log2(HGRN_CHUNK)) - 1, -1, -1))


def _hgrn_kernel(lb_ref, g_ref, q_ref, f_ref, i_ref, og_ref, o_ref,
                 qf_ref, kf_ref, bq_ref, kd_ref, dec_ref, v_ref, st_ref):
    s_len = q_ref.shape[0]
    c_len = HGRN_CHUNK
    n_lvl = len(HGRN_LEVELS)
    rc = lax.broadcasted_iota(jnp.int32, (s_len, 1), 0) % c_len

    z = f_ref[...]
    lb = lb_ref[...]
    log_sig = jnp.minimum(z, 0.0) - jnp.log1p(jnp.exp(-jnp.abs(z)))
    a = jnp.log(lb)
    c = jnp.log1p(-lb) + log_sig
    delta = a - c
    log_f = jnp.where(jnp.isnan(delta), a + c,
                      jnp.maximum(a, c) + jnp.log1p(jnp.exp(-jnp.abs(delta))))
    key = (1.0 - lb) * (1.0 / (1.0 + jnp.exp(z)))
    qs = _silu(q_ref[...])
    v_ref[...] = i_ref[...].astype(BF16)

    b = log_f
    for d in (1, 2, 4, 8, 16, 32):
        b = b + jnp.where(rc >= d, pltpu.roll(b, d, 0), 0.0)

    bq_ref[...] = (qs * jnp.exp(b)).astype(BF16)
    qf_ref[n_lvl] = qs.astype(BF16)
    kf_ref[n_lvl] = key.astype(BF16)

    r_m = jnp.where(rc >= 1, pltpu.roll(b, 1, 0), 0.0)
    for m in (1, 2, 4, 8, 16, 32):
        if m > 1:
            half = m // 2
            r_m = jnp.where((rc // half) % 2 == 1, pltpu.roll(r_m, half, 0), r_m)
        e_m = pltpu.roll(r_m, s_len - m, 0)
        lvl = HGRN_LEVELS.index(m)
        qf_ref[lvl] = (qs * jnp.exp(jnp.minimum(b - r_m, 0.0))).astype(BF16)
        kf_ref[lvl] = (key * jnp.exp(jnp.minimum(e_m - b, 0.0))).astype(BF16)
    b_end = jnp.where(rc == c_len - 1, b, 0.0)
    for d in (1, 2, 4, 8, 16, 32):
        b_end = b_end + jnp.where(rc < c_len - d, pltpu.roll(b_end, s_len - d, 0), 0.0)
    kd_ref[...] = (key * jnp.exp(jnp.minimum(b_end - b, 0.0))).astype(BF16)
    dec_ref[...] = jnp.exp(b_end)

    ti = lax.broadcasted_iota(jnp.int32, (c_len, c_len), 0)
    si = lax.broadcasted_iota(jnp.int32, (c_len, c_len), 1)
    masks = [((ti // m) % 2 == 1) & (si // m == ti // m - 1) for m in HGRN_LEVELS] + [ti == si]

    st_ref[...] = jnp.zeros_like(st_ref)
    g = g_ref[...]

    def chunk(ci, carry):
        r0 = pl.multiple_of(ci * c_len, c_len)
        rows = pl.ds(r0, c_len)
        att = jnp.zeros((c_len, c_len), F32)
        for lvl in range(n_lvl + 1):
            att = att + jnp.where(masks[lvl], _dot_nt(qf_ref[lvl, rows, :], kf_ref[lvl, rows, :]), 0.0)
        vc = v_ref[rows, :]
        st = st_ref[...]
        o = _dot(att.astype(BF16), vc) + _dot_nt(bq_ref[rows, :], st.astype(BF16))
        st_ref[...] = st * dec_ref[pl.ds(r0, 1), :] + _dot_tn(vc, kd_ref[rows, :])
        o_ref[rows, :] = (_rms(o, g) * _silu(og_ref[rows, :])).astype(o_ref.dtype)
        return carry

    lax.fori_loop(0, s_len // c_len, chunk, 0)


def hgrn_recurrence(z, lb, norm_g, batch):
    t = z.shape[0]
    s = t // batch
    hd = HGRN_HEAD_DIM
    n_lvl = len(HGRN_LEVELS)

    def part(p):
        return pl.BlockSpec((s, hd), lambda b, h: (b, p * HGRN_HEADS + h))

    return pl.pallas_call(
        _hgrn_kernel,
        grid=(batch, HGRN_HEADS),
        in_specs=[pl.BlockSpec((1, hd), lambda b, h: (0, h)), pl.BlockSpec((1, hd), lambda b, h: (0, 0)),
                  part(0), part(1), part(2), part(3)],
        out_specs=pl.BlockSpec((s, hd), lambda b, h: (b, h)),
        out_shape=jax.ShapeDtypeStruct((t, HGRN_HEADS * hd), BF16),
        scratch_shapes=[
            pltpu.VMEM((n_lvl + 1, s, hd), BF16),
            pltpu.VMEM((n_lvl + 1, s, hd), BF16),
            pltpu.VMEM((s, hd), BF16),
            pltpu.VMEM((s, hd), BF16),
            pltpu.VMEM((s, hd), F32),
            pltpu.VMEM((s, hd), BF16),
            pltpu.VMEM((hd, hd), F32),
        ],
        compiler_params=_params("parallel", "parallel"),
        name="hgrn_recurrence",
    )(lb.reshape(1, -1), norm_g.reshape(1, hd), z, z, z, z)


def _sortable(score):
    score = jnp.where(score == 0.0, 0.0, score)
    bits = lax.bitcast_convert_type(score, jnp.int32)
    return bits ^ ((bits >> 31) & jnp.int32(0x7FFFFFFF))


def _dsa_select(c, qi_ref, ki_ref, wi_ref, qsel_ref, kib_ref, wt_ref, sc_ref, keys_ref, mask_ref, n_sel):
    kv_len = (c + 1) * TQ
    lo = lax.broadcasted_iota(jnp.int32, (TQ, LANES), 1) < HEAD_DIM
    for p in range(IDX_HEADS // 2):
        qp = qi_ref[:, p * LANES:(p + 1) * LANES]
        qsel_ref[2 * p] = _head_select(qp, lo, True)
        qsel_ref[2 * p + 1] = _head_select(qp, lo, False)
    wt_ref[...] = wi_ref[...].T * (IDX_HEADS ** -0.5 * IDX_DIM ** -0.5)
    kib_ref[0:kv_len, :] = ki_ref[0:kv_len, :].astype(BF16)
    sc_ref[0:kv_len, :] = jnp.zeros((kv_len, TQ), F32)

    def head(hh, carry):
        logit = _dot_nt(kib_ref[0:kv_len, :], qsel_ref[hh])
        sc_ref[0:kv_len, :] += jnp.maximum(logit, 0.0) * wt_ref[pl.ds(hh, 1), :]
        return carry

    lax.fori_loop(0, IDX_HEADS, head, 0)

    key_i = lax.broadcasted_iota(jnp.int32, (TQ, TQ), 0)
    qry_i = lax.broadcasted_iota(jnp.int32, (TQ, TQ), 1)
    if c > 0:
        keys_ref[0:c * TQ, :] = _sortable(sc_ref[0:c * TQ, :])
    keys_ref[c * TQ:kv_len, :] = _sortable(jnp.where(key_i <= qry_i, sc_ref[c * TQ:kv_len, :], NEG_INF))

    def count(pred):
        return jnp.sum(pred(keys_ref[0:kv_len, :]).astype(jnp.int32), axis=0, keepdims=True)

    def search(it, thr):
        trial = thr ^ (jnp.int32(1) << (31 - it))
        return jnp.where(count(lambda kj: kj >= trial) >= n_sel, trial, thr)

    thr = lax.fori_loop(0, 32, search, jnp.full((1, TQ), jnp.iinfo(jnp.int32).min, jnp.int32))
    need = (n_sel - count(lambda kj: kj > thr)).astype(F32)
    earlier = (qry_i < key_i).astype(BF16)
    base = jnp.zeros((1, TQ), F32)
    for j in range(c + 1):
        kj = keys_ref[j * TQ:(j + 1) * TQ, :]
        eq = kj == thr
        eq_f = jnp.where(eq, 1.0, 0.0)
        rank = base + _dot(earlier, eq_f.astype(BF16))
        sel = (kj > thr) | (eq & (rank < need))
        mask_ref[:, j * TQ:(j + 1) * TQ] = jnp.where(sel, 0.0, NEG_INF).T.astype(mask_ref.dtype)
        base = base + jnp.sum(eq_f, axis=0, keepdims=True)
    s_len = mask_ref.shape[1]
    if kv_len < s_len:
        mask_ref[:, kv_len:] = jnp.full((TQ, s_len - kv_len), NEG_INF, mask_ref.dtype)


def _dsa_select_kernel(qi_ref, ki_ref, wi_ref, mask_ref, qsel_ref, kib_ref, wt_ref, sc_ref, keys_ref, *, n_sel, nq):
    i = pl.program_id(1)
    for c in range(nq):
        pl.when(i == c)(functools.partial(
            _dsa_select, c, qi_ref, ki_ref, wi_ref, qsel_ref, kib_ref, wt_ref, sc_ref, keys_ref, mask_ref, n_sel))


def _dsa_attend_kernel(tab_ref, q_ref, k_ref, v_ref, mask_ref, bias_ref, o_ref):
    g = pl.program_id(1)
    last = NUM_BUCKETS - 1
    far_bias = jnp.concatenate([jnp.full((TQ, 1), tab_ref[last, 4 * g + hh], F32) for hh in range(4)], axis=0)
    lo = lax.broadcasted_iota(jnp.int32, (TQ, LANES), 1) < HEAD_DIM
    for c in range(q_ref.shape[0] // TQ):
        rows = slice(c * TQ, (c + 1) * TQ)
        kv_len = (c + 1) * TQ
        stack = _stack_heads(q_ref[rows, :] * HEAD_DIM ** -0.5)
        s = _causal_scores(stack, k_ref, kv_len, bias_ref, far_bias)
        s = s + jnp.concatenate([mask_ref[rows, 0:kv_len].astype(F32)] * 4, axis=0)
        p = jnp.exp(s - jnp.max(s, axis=-1, keepdims=True))
        inv = 1.0 / jnp.sum(p, axis=-1, keepdims=True)
        p = p.astype(BF16)
        vv = v_ref[0:kv_len, :]
        lo_v = lax.broadcasted_iota(jnp.int32, (kv_len, LANES), 1) < HEAD_DIM
        v_both = jnp.concatenate([_head_select(vv, lo_v, True), _head_select(vv, lo_v, False)], axis=0)
        outs = []
        for pr in range(2):
            r0 = 2 * pr * TQ
            p_both = jnp.concatenate([p[r0:r0 + TQ], p[r0 + TQ:r0 + 2 * TQ]], axis=1)
            scale = jnp.where(lo, inv[r0:r0 + TQ], inv[r0 + TQ:r0 + 2 * TQ])
            outs.append(_dot(p_both, v_both) * scale)
        o_ref[rows, :] = jnp.concatenate(outs, axis=1).astype(o_ref.dtype)


def dsa_attention(z, z_idx, table, bias_causal, batch):
    t = z.shape[0]
    s = t // batch
    nq = s // TQ
    n_sel = min(TOPK_MAX, s // 4)
    k_col0 = N_HEADS * HEAD_DIM // LANES
    v_col0 = k_col0 + N_KV_HEADS
    qi_col = (N_HEADS * HEAD_DIM + 4 * N_KV_HEADS * HEAD_DIM) // (IDX_HEADS * IDX_DIM)
    mask = pl.pallas_call(
        functools.partial(_dsa_select_kernel, n_sel=n_sel, nq=nq),
        grid=(batch, nq),
        in_specs=[
            pl.BlockSpec((TQ, IDX_HEADS * IDX_DIM), lambda b, i: (b * nq + i, qi_col)),
            pl.BlockSpec((s, LANES), lambda b, i: (b, 0)),
            pl.BlockSpec((TQ, LANES), lambda b, i: (b * nq + i, 1)),
        ],
        out_specs=pl.BlockSpec((TQ, s), lambda b, i: (b * nq + i, 0)),
        out_shape=jax.ShapeDtypeStruct((t, s), BF16),
        scratch_shapes=[
            pltpu.VMEM((IDX_HEADS, TQ, LANES), BF16),
            pltpu.VMEM((s, LANES), BF16),
            pltpu.VMEM((LANES, TQ), F32),
            pltpu.VMEM((s, TQ), F32),
            pltpu.VMEM((s, TQ), jnp.int32),
        ],
        compiler_params=_params("parallel", "parallel"),
        name="dsa_select",
    )(z, z_idx, z_idx)
    return pl.pallas_call(
        _dsa_attend_kernel,
        grid=(batch, N_KV_HEADS),
        in_specs=[
            pl.BlockSpec(memory_space=pltpu.SMEM),
            pl.BlockSpec((s, 4 * HEAD_DIM), lambda b, g: (b, g)),
            pl.BlockSpec((s, LANES), lambda b, g: (b, k_col0 + g)),
            pl.BlockSpec((s, LANES), lambda b, g: (b, v_col0 + g)),
            pl.BlockSpec((s, s), lambda b, g: (b, 0)),
            pl.BlockSpec((4, TQ, 2 * TQ), lambda b, g: (g, 0, 0)),
        ],
        out_specs=pl.BlockSpec((s, 4 * HEAD_DIM), lambda b, g: (b, g)),
        out_shape=jax.ShapeDtypeStruct((t, N_HEADS * HEAD_DIM), BF16),
        compiler_params=_params("parallel", "parallel"),
        name="dsa_attend",
    )(table, z, z, z, mask, bias_causal)


def _lower_bound_kernel(x_ref, o_ref):
    x = x_ref[...]
    e = jnp.exp(x - jnp.max(x, axis=0, keepdims=True))
    soft = e / jnp.sum(e, axis=0, keepdims=True)
    run = soft[0:1]
    o_ref[0:1, :] = run - soft[0:1]
    for r in range(1, x.shape[0]):
        run = run + soft[r:r + 1]
        o_ref[r:r + 1, :] = run - soft[0:1]


def hgrn_lower_bounds(logits):
    return pl.pallas_call(
        _lower_bound_kernel,
        out_shape=jax.ShapeDtypeStruct(logits.shape, F32),
        name="hgrn_lower_bounds",
    )(logits)


def _dup_heads(w, n_heads):
    k = w.shape[0]
    w = w.reshape(k, n_heads, 1, HEAD_DIM)
    return jnp.broadcast_to(w, (k, n_heads, 2, HEAD_DIM)).reshape(k, n_heads * 2 * HEAD_DIM)


def kernel(x, rel_bias_table, hgrn_lb_logits, norm_g, ffn_w_up, ffn_conv, ffn_w_down, swa_w_in, swa_w_out,
           swa_sinks, diff_w_in, diff_w_out, diff_lambda, diff_subln_g, hgrn_w_in, hgrn_w_out, hgrn_norm_g,
           dsa_w_in, dsa_w_out):
    batch, seq, d = x.shape
    t = batch * seq
    depth = norm_g.shape[0]
    n_mixers = 4
    bias_causal, bias_window = bias_tiles(rel_bias_table)
    lb_all = hgrn_lower_bounds(hgrn_lb_logits)
    kv = N_KV_HEADS * HEAD_DIM
    x2 = x.reshape(t, d)
    h = norm_cast(x2, norm_g[0, 0])
    for i in range(depth):
        kind, j = i % n_mixers, i // n_mixers
        if kind == 0:
            wq, wk, wv = jnp.split(swa_w_in[j], [d, d + kv], axis=1)
            w_in = jnp.concatenate([wq, _dup_heads(wk, N_KV_HEADS), _dup_heads(wv, N_KV_HEADS)], axis=1)
            z = matmul(h, w_in.astype(BF16), BF16)
            o = swa_attention(z, swa_sinks[j], bias_window, batch)
            w_out = swa_w_out[j]
        elif kind == 1:
            z = matmul(h, diff_w_in[j].astype(BF16), BF16)
            o = diff_attention(z, rel_bias_table, diff_lambda[j], diff_subln_g[j], bias_causal, batch, i)
            w_out = diff_w_out[j]
        elif kind == 2:
            z = matmul(h, hgrn_w_in[j].astype(BF16), F32)
            o = hgrn_recurrence(z, lb_all[i], hgrn_norm_g[j], batch)
            w_out = hgrn_w_out[j]
        else:
            n_qi = IDX_HEADS * IDX_DIM
            wq, wk, wv, wqi, wki, wwi = jnp.split(
                dsa_w_in[j], [d, d + kv, d + 2 * kv, d + 2 * kv + n_qi, d + 2 * kv + n_qi + IDX_DIM], axis=1)
            w_main = jnp.concatenate([wq, _dup_heads(wk, N_KV_HEADS), _dup_heads(wv, N_KV_HEADS), wqi], axis=1)
            w_idx = jnp.concatenate([wki, wki, wwi, jnp.zeros((d, LANES - IDX_HEADS), F32)], axis=1)
            z = matmul(h, w_main.astype(BF16), BF16)
            z_idx = matmul(h, w_idx.astype(BF16), F32)
            o = dsa_attention(z, z_idx, rel_bias_table, bias_causal, batch)
            w_out = dsa_w_out[j]
        x2, h = proj_residual(o, w_out.astype(BF16), x2, norm_g[i, 1], norm_g[i, 2])
        a = ffn_up(h, ffn_w_up[i].astype(BF16), ffn_conv[i], batch)
        x2, h = proj_residual(a, ffn_w_down[i].astype(BF16), x2, norm_g[i, 3], norm_g[(i + 1) % depth, 0])
    return x2.reshape(batch, seq, d)
```

```python
import functools
import math

import jax
import jax.numpy as jnp
import numpy as np
from jax import lax
from jax.experimental import pallas as pl
from jax.experimental.pallas import tpu as pltpu

D_MODEL = 2048
HEAD_DIM = 64
N_HEADS = 32
N_KV_HEADS = 8
WINDOW = 128
DIFF_HEADS = 16
HGRN_HEADS = 16
HGRN_HEAD_DIM = 128
HGRN_CHUNK = 64
IDX_HEADS = 16
IDX_DIM = 64
TOPK_MAX = 256
D_FF = 5632
CONV_WIDTH = 3
NUM_BUCKETS = 32
MAX_DISTANCE = 128
RMS_EPS = 1e-6

LANES = 128
TQ = 256
VMEM_LIMIT = 56 * 1024 * 1024
NEG_INF = float("-inf")
LOG2_E = math.log2(math.e)
BF16 = jnp.bfloat16
F32 = jnp.float32


def _params(*sem):
    return pltpu.CompilerParams(dimension_semantics=sem, vmem_limit_bytes=VMEM_LIMIT)


def _dot(a, b):
    return jnp.dot(a, b, preferred_element_type=F32)


def _dot_nt(a, b):
    return lax.dot_general(a, b, (((1,), (1,)), ((), ())), preferred_element_type=F32)


def _dot_tn(a, b):
    return lax.dot_general(a, b, (((0,), (0,)), ((), ())), preferred_element_type=F32)


def _rms(x, g):
    return x * lax.rsqrt(jnp.mean(x * x, axis=-1, keepdims=True) + RMS_EPS) * g


def _silu(x):
    return x * (1.0 / (1.0 + jnp.exp(-x)))


def _norm_cast_kernel(x_ref, g_ref, o_ref):
    o_ref[...] = _rms(x_ref[...], g_ref[...]).astype(o_ref.dtype)


def norm_cast(x, g):
    t, d = x.shape
    tm = 512
    return pl.pallas_call(
        _norm_cast_kernel,
        grid=(t // tm,),
        in_specs=[pl.BlockSpec((tm, d), lambda i: (i, 0)), pl.BlockSpec((1, d), lambda i: (0, 0))],
        out_specs=pl.BlockSpec((tm, d), lambda i: (i, 0)),
        out_shape=jax.ShapeDtypeStruct((t, d), BF16),
        compiler_params=_params("parallel"),
        name="norm_cast",
    )(x, g.reshape(1, d))


def _mm_kernel(x_ref, w_ref, o_ref):
    o_ref[...] = _dot(x_ref[...], w_ref[...]).astype(o_ref.dtype)


def matmul(x, w, out_dtype):
    t, k = x.shape
    n = w.shape[1]
    tm = 1024
    tn = next(c for c in (1024, 512, 256, 128) if n % c == 0)
    return pl.pallas_call(
        _mm_kernel,
        grid=(n // tn, t // tm),
        in_specs=[pl.BlockSpec((tm, k), lambda j, i: (i, 0)), pl.BlockSpec((k, tn), lambda j, i: (0, j))],
        out_specs=pl.BlockSpec((tm, tn), lambda j, i: (i, j)),
        out_shape=jax.ShapeDtypeStruct((t, n), out_dtype),
        compiler_params=_params("parallel", "parallel"),
        name="in_proj",
    )(x, w)


def _proj_res_kernel(a_ref, w_ref, x_ref, go_ref, gn_ref, xo_ref, h_ref):
    xn = x_ref[...] + _rms(_dot(a_ref[...], w_ref[...]), go_ref[...])
    xo_ref[...] = xn
    h_ref[...] = _rms(xn, gn_ref[...]).astype(h_ref.dtype)


def _proj_rows(k, d):
    for tm in (512, 256, 128):
        streamed = 2 * (tm * k * 2 + 2 * tm * d * 4 + tm * d * 2)
        temporaries = 3 * tm * d * 4
        if k * d * 2 + streamed + temporaries <= VMEM_LIMIT - (4 << 20):
            return tm
    raise ValueError("projection weight does not fit in VMEM")


def proj_residual(a, w, x, g_out, g_next):
    t, k = a.shape
    d = w.shape[1]
    tm = _proj_rows(k, d)
    return pl.pallas_call(
        _proj_res_kernel,
        grid=(t // tm,),
        in_specs=[
            pl.BlockSpec((tm, k), lambda i: (i, 0)),
            pl.BlockSpec((k, d), lambda i: (0, 0), pipeline_mode=pl.Buffered(1)),
            pl.BlockSpec((tm, d), lambda i: (i, 0)),
            pl.BlockSpec((1, d), lambda i: (0, 0)),
            pl.BlockSpec((1, d), lambda i: (0, 0)),
        ],
        out_specs=[pl.BlockSpec((tm, d), lambda i: (i, 0)), pl.BlockSpec((tm, d), lambda i: (i, 0))],
        out_shape=[jax.ShapeDtypeStruct((t, d), F32), jax.ShapeDtypeStruct((t, d), BF16)],
        compiler_params=_params("parallel"),
        name="proj_residual",
    )(a, w, x, g_out.reshape(1, d), g_next.reshape(1, d))


FFN_ROWS = 512
FFN_COLS = 512


def _ffn_up_kernel(h_ref, wg_ref, wv_ref, cg_ref, cv_ref, o_ref):
    s = h_ref.shape[0]
    rows = min(FFN_ROWS, s)
    tn = o_ref.shape[1]
    row = lax.broadcasted_iota(jnp.int32, (rows, 1), 0)
    wg, wv, cg, cv = wg_ref[...], wv_ref[...], cg_ref[...], cv_ref[...]

    def conv(u, tail, c):
        u1 = jnp.where(row >= 1, pltpu.roll(u, 1, 0), tail[1:2])
        u2 = jnp.where(row >= 2, pltpu.roll(u, 2, 0), jnp.where(row == 1, tail[1:2], tail[0:1]))
        return (c[0:1] * u2 + c[1:2] * u1) + c[2:3] * u

    tail_g = tail_v = jnp.zeros((CONV_WIDTH - 1, tn), F32)
    for r in range(s // rows):
        hr = h_ref[r * rows:(r + 1) * rows, :]
        ug, uv = _dot(hr, wg), _dot(hr, wv)
        out = _silu(conv(ug, tail_g, cg)) * conv(uv, tail_v, cv)
        o_ref[r * rows:(r + 1) * rows, :] = out.astype(o_ref.dtype)
        tail_g, tail_v = ug[rows - 2:rows], uv[rows - 2:rows]


def ffn_up(h, w_up, conv_w, batch):
    t, d = h.shape
    s = t // batch
    tn = FFN_COLS
    nj = D_FF // tn
    return pl.pallas_call(
        _ffn_up_kernel,
        grid=(batch, nj),
        in_specs=[
            pl.BlockSpec((s, d), lambda b, j: (b, 0)),
            pl.BlockSpec((d, tn), lambda b, j: (0, j)),
            pl.BlockSpec((d, tn), lambda b, j: (0, nj + j)),
            pl.BlockSpec((CONV_WIDTH, tn), lambda b, j: (0, j)),
            pl.BlockSpec((CONV_WIDTH, tn), lambda b, j: (0, nj + j)),
        ],
        out_specs=pl.BlockSpec((s, tn), lambda b, j: (b, j)),
        out_shape=jax.ShapeDtypeStruct((t, D_FF), BF16),
        compiler_params=_params("parallel", "parallel"),
        name="ffn_up",
    )(h, w_up, w_up, conv_w, conv_w)


def _bucket_thresholds():
    max_exact = NUM_BUCKETS // 2
    n = np.arange(1, 2 * MAX_DISTANCE, dtype=np.float64)
    large = max_exact + np.floor(np.log(n / max_exact) / math.log(MAX_DISTANCE / max_exact) * (NUM_BUCKETS - max_exact))
    bucket = np.where(n < max_exact, n, np.minimum(large, NUM_BUCKETS - 1)).astype(np.int64)
    return tuple(int(n[bucket >= b][0]) for b in range(max_exact + 1, NUM_BUCKETS))


BUCKET_THRESHOLDS = _bucket_thresholds()


def _bias_tile_kernel(tab_ref, causal_ref, window_ref):
    m = pl.program_id(0)
    r = lax.broadcasted_iota(jnp.int32, (TQ, 2 * TQ), 0)
    c = lax.broadcasted_iota(jnp.int32, (TQ, 2 * TQ), 1)
    d = r - c + TQ
    n = jnp.maximum(d, 0)
    max_exact = NUM_BUCKETS // 2
    large = jnp.full_like(n, max_exact)
    for thr in BUCKET_THRESHOLDS:
        large = large + (n >= thr).astype(jnp.int32)
    bucket = jnp.where(n < max_exact, n, large)
    val = jnp.zeros((TQ, 2 * TQ), F32)
    for b in range(NUM_BUCKETS):
        val = jnp.where(bucket == b, tab_ref[b, m], val)
    causal = jnp.where(d >= 0, val, NEG_INF)
    causal_ref[0] = causal
    window_ref[0] = jnp.where(d < WINDOW, causal, NEG_INF)


def bias_tiles(table):
    shape = jax.ShapeDtypeStruct((N_HEADS, TQ, 2 * TQ), F32)
    return pl.pallas_call(
        _bias_tile_kernel,
        grid=(N_HEADS,),
        in_specs=[pl.BlockSpec(memory_space=pltpu.SMEM)],
        out_specs=[pl.BlockSpec((1, TQ, 2 * TQ), lambda m: (m, 0, 0))] * 2,
        out_shape=[shape, shape],
        compiler_params=_params("parallel"),
        name="bias_tiles",
    )(table)


def _head_select(x, lo, first):
    zero = jnp.zeros_like(x)
    return jnp.where(lo, x, zero) if first else jnp.where(lo, zero, x)


def _stack_heads(q):
    rows = q.shape[0]
    lo = lax.broadcasted_iota(jnp.int32, (rows, LANES), 1) < HEAD_DIM
    parts = []
    for p in range(q.shape[1] // LANES):
        qp = q[:, p * LANES:(p + 1) * LANES]
        parts += [_head_select(qp, lo, True), _head_select(qp, lo, False)]
    return jnp.concatenate(parts, axis=0)


SWA_BLOCK = WINDOW


def _swa_kernel(sink_ref, q_ref, kp_ref, ko_ref, vp_ref, vo_ref, bias_ref, o_ref):
    i = pl.program_id(1)
    blk = SWA_BLOCK
    col = lax.broadcasted_iota(jnp.int32, (1, 2 * blk), 1)
    no_prev = (col < blk) & (i == 0)
    lo2 = lax.broadcasted_iota(jnp.int32, (2 * blk, LANES), 1) < HEAD_DIM
    outs = []
    for g in range(N_KV_HEADS):
        q = q_ref[:, g * 4 * HEAD_DIM:(g + 1) * 4 * HEAD_DIM] * HEAD_DIM ** -0.5
        stack = _stack_heads(q)
        cols = slice(g * LANES, (g + 1) * LANES)
        kk = jnp.concatenate([kp_ref[:, cols], ko_ref[:, cols]], axis=0)
        s = _dot_nt(stack, kk) + bias_ref[4 * g:4 * g + 4].reshape(4 * blk, 2 * blk)
        s = jnp.where(no_prev, NEG_INF, s)
        sink = jnp.concatenate([jnp.full((blk, 1), sink_ref[4 * g + hh], F32) for hh in range(4)], axis=0)
        m = jnp.maximum(jnp.max(s, axis=-1, keepdims=True), sink)
        e = jnp.exp(s - m)
        inv = 1.0 / (jnp.sum(e, axis=-1, keepdims=True) + jnp.exp(sink - m))
        e = e.astype(BF16)
        vv = jnp.concatenate([vp_ref[:, cols], vo_ref[:, cols]], axis=0)
        v_first, v_second = _head_select(vv, lo2, True), _head_select(vv, lo2, False)
        for p in range(2):
            r0 = 2 * p * blk
            o_first = _dot(e[r0:r0 + blk], v_first) * inv[r0:r0 + blk]
            o_second = _dot(e[r0 + blk:r0 + 2 * blk], v_second) * inv[r0 + blk:r0 + 2 * blk]
            outs.append(o_first + o_second)
    o_ref[...] = jnp.concatenate(outs, axis=1).astype(o_ref.dtype)


def swa_attention(z, sinks, bias_window, batch):
    t = z.shape[0]
    blk = SWA_BLOCK
    nq = t // batch // blk
    dq = N_HEADS * HEAD_DIM
    dk = 2 * N_KV_HEADS * HEAD_DIM
    k_col, v_col = dq // dk, dq // dk + 1

    def own(col):
        return lambda b, i: (b * nq + i, col)

    def prev(col):
        return lambda b, i: (b * nq + jnp.maximum(i - 1, 0), col)

    bias_spec = pl.BlockSpec((N_HEADS, blk, 2 * blk), lambda b, i: (0, TQ // blk - 1, TQ // blk - 1))
    return pl.pallas_call(
        _swa_kernel,
        grid=(batch, nq),
        in_specs=[
            pl.BlockSpec(memory_space=pltpu.SMEM),
            pl.BlockSpec((blk, dq), own(0)),
            pl.BlockSpec((blk, dk), prev(k_col)),
            pl.BlockSpec((blk, dk), own(k_col)),
            pl.BlockSpec((blk, dk), prev(v_col)),
            pl.BlockSpec((blk, dk), own(v_col)),
            bias_spec,
        ],
        out_specs=pl.BlockSpec((blk, dq), own(0)),
        out_shape=jax.ShapeDtypeStruct((t, dq), BF16),
        compiler_params=_params("parallel", "parallel"),
        name="swa_attention",
    )(sinks, z, z, z, z, z, bias_window)


def _causal_scores(stack, k_ref, kv_len, bias_ref, far_bias):
    rows = stack.shape[0]
    s = _dot_nt(stack, k_ref[0:kv_len, :])
    if kv_len == TQ:
        return s + bias_ref[:, :, TQ:2 * TQ].reshape(rows, TQ)
    near = bias_ref[...].reshape(rows, 2 * TQ)
    if kv_len == 2 * TQ:
        return s + near
    n_far = kv_len - 2 * TQ
    return jnp.concatenate([s[:, :n_far] + far_bias, s[:, n_far:] + near], axis=1)


def _diff_kernel(tab_ref, lam_ref, g_ref, q_ref, k_ref, v_ref, bias_ref, o_ref, *, lam_init):
    h = pl.program_id(0)
    last = NUM_BUCKETS - 1
    far_bias = jnp.concatenate([jnp.full((TQ, 1), tab_ref[last, 2 * h], F32),
                                jnp.full((TQ, 1), tab_ref[last, 2 * h + 1], F32)], axis=0)
    lf = lam_ref[...]
    lam = (jnp.exp(jnp.sum(lf[0:1] * lf[1:2], axis=-1, keepdims=True))
           - jnp.exp(jnp.sum(lf[2:3] * lf[3:4], axis=-1, keepdims=True)) + lam_init)
    for c in range(q_ref.shape[0] // TQ):
        rows = slice(c * TQ, (c + 1) * TQ)
        kv_len = (c + 1) * TQ
        stack = _stack_heads(q_ref[rows, :] * HEAD_DIM ** -0.5)
        s = _causal_scores(stack, k_ref, kv_len, bias_ref, far_bias)
        p = jnp.exp(s - jnp.max(s, axis=-1, keepdims=True))
        inv = 1.0 / jnp.sum(p, axis=-1, keepdims=True)
        o12 = _dot(p.astype(BF16), v_ref[0:kv_len, :]) * inv
        o = o12[:TQ] - lam * o12[TQ:]
        o_ref[rows, :] = (_rms(o, g_ref[...]) * (1.0 - lam_init)).astype(o_ref.dtype)


def diff_attention(z, table, lambdas, subln_g, bias_causal, batch, layer_idx):
    t = z.shape[0]
    s = t // batch
    lam_init = 0.8 - 0.6 * math.exp(-0.3 * layer_idx)
    return pl.pallas_call(
        functools.partial(_diff_kernel, lam_init=lam_init),
        grid=(DIFF_HEADS, batch),
        in_specs=[
            pl.BlockSpec(memory_space=pltpu.SMEM),
            pl.BlockSpec((4, HEAD_DIM), lambda h, b: (0, 0)),
            pl.BlockSpec((1, LANES), lambda h, b: (0, 0)),
            pl.BlockSpec((s, LANES), lambda h, b: (b, h)),
            pl.BlockSpec((s, LANES), lambda h, b: (b, DIFF_HEADS + h)),
            pl.BlockSpec((s, LANES), lambda h, b: (b, 2 * DIFF_HEADS + h)),
            pl.BlockSpec((2, TQ, 2 * TQ), lambda h, b: (h, 0, 0)),
        ],
        out_specs=pl.BlockSpec((s, LANES), lambda h, b: (b, h)),
        out_shape=jax.ShapeDtypeStruct((t, DIFF_HEADS * 2 * HEAD_DIM), BF16),
        compiler_params=_params("parallel", "parallel"),
        name="diff_attention",
    )(table, lambdas, subln_g.reshape(1, LANES), z, z, z, bias_causal)


HGRN_LEVELS = tuple(2 ** e for e in range(int(math.log2(HGRN_CHUNK)) - 1, -1, -1))


def _hgrn_kernel(lb_ref, g_ref, q_ref, f_ref, i_ref, og_ref, o_ref,
                 qf_ref, kf_ref, bq_ref, kd_ref, dec_ref, v_ref, oi_ref, kv_ref, st_ref):
    s_len = q_ref.shape[0]
    c_len = HGRN_CHUNK
    n_lvl = len(HGRN_LEVELS)
    n_chunks = s_len // c_len
    hd = q_ref.shape[1]
    rc = lax.broadcasted_iota(jnp.int32, (s_len, hd), 0) & (c_len - 1)

    def log1p01(u):
        w = 1.0 + u
        return jnp.where(w == 1.0, u, jnp.log(w) * (u / (w - 1.0)))

    z = f_ref[...]
    lb = lb_ref[...]
    log_sig = jnp.minimum(z, 0.0) - log1p01(jnp.exp(-jnp.abs(z)))
    a = jnp.log(lb)
    c = jnp.log1p(-lb) + log_sig
    delta = a - c
    log_f = jnp.where(jnp.isnan(delta), a + c,
                      jnp.maximum(a, c) + log1p01(jnp.exp(-jnp.abs(delta))))
    key = (1.0 - lb) * (1.0 / (1.0 + jnp.exp(z)))
    qs = _silu(q_ref[...])
    v_ref[...] = i_ref[...].astype(BF16)

    b = log_f * LOG2_E
    for d in (1, 2, 4, 8, 16, 32):
        b = b + jnp.where(rc >= d, pltpu.roll(b, d, 0), 0.0)

    bq_ref[...] = (qs * jnp.exp2(b)).astype(BF16)
    qf_ref[n_lvl] = qs.astype(BF16)
    kf_ref[n_lvl] = key.astype(BF16)

    r_m = jnp.where(rc >= 1, pltpu.roll(b, 1, 0), 0.0)
    for m in (1, 2, 4, 8, 16, 32):
        if m > 1:
            half = m // 2
            r_m = jnp.where((rc & half) != 0, pltpu.roll(r_m, half, 0), r_m)
        e_m = pltpu.roll(r_m, s_len - m, 0)
        lvl = HGRN_LEVELS.index(m)
        qf_ref[lvl] = (qs * jnp.exp2(b - r_m)).astype(BF16)
        kf_ref[lvl] = (key * jnp.exp2(jnp.minimum(e_m - b, 0.0))).astype(BF16)
    b3 = b.reshape(n_chunks, c_len, hd)
    b_end = jnp.broadcast_to(b3[:, c_len - 1:c_len, :], b3.shape).reshape(s_len, hd)
    kd_ref[...] = (key * jnp.exp2(b_end - b)).astype(BF16)
    dec_ref[...] = jnp.exp2(b_end)

    ti = lax.broadcasted_iota(jnp.int32, (c_len, c_len), 0)
    si = lax.broadcasted_iota(jnp.int32, (c_len, c_len), 1)
    masks = [((ti // m) % 2 == 1) & (si // m == ti // m - 1) for m in HGRN_LEVELS] + [ti == si]

    n_chunks = s_len // c_len
    for ci in range(n_chunks):
        rows = slice(ci * c_len, (ci + 1) * c_len)
        att = jnp.zeros((c_len, c_len), F32)
        for lvl in range(n_lvl + 1):
            att = att + jnp.where(masks[lvl], _dot_nt(qf_ref[lvl, rows, :], kf_ref[lvl, rows, :]), 0.0)
        vc = v_ref[rows, :]
        oi_ref[rows, :] = _dot(att.astype(BF16), vc)
        kv_ref[ci] = _dot_tn(vc, kd_ref[rows, :])
    st = jnp.zeros((HGRN_HEAD_DIM, HGRN_HEAD_DIM), F32)
    for ci in range(n_chunks):
        st_ref[ci] = st.astype(BF16)
        st = st * dec_ref[ci * c_len:ci * c_len + 1, :] + kv_ref[ci]
    g = g_ref[...]
    for ci in range(n_chunks):
        rows = slice(ci * c_len, (ci + 1) * c_len)
        o = oi_ref[rows, :] + _dot_nt(bq_ref[rows, :], st_ref[ci])
        o_ref[rows, :] = (_rms(o, g) * _silu(og_ref[rows, :])).astype(o_ref.dtype)


def hgrn_recurrence(z, lb, norm_g, batch):
    t = z.shape[0]
    s = t // batch
    hd = HGRN_HEAD_DIM
    n_lvl = len(HGRN_LEVELS)

    def part(p):
        return pl.BlockSpec((s, hd), lambda b, h: (b, p * HGRN_HEADS + h))

    return pl.pallas_call(
        _hgrn_kernel,
        grid=(batch, HGRN_HEADS),
        in_specs=[pl.BlockSpec((1, hd), lambda b, h: (0, h)), pl.BlockSpec((1, hd), lambda b, h: (0, 0)),
                  part(0), part(1), part(2), part(3)],
        out_specs=pl.BlockSpec((s, hd), lambda b, h: (b, h)),
        out_shape=jax.ShapeDtypeStruct((t, HGRN_HEADS * hd), BF16),
        scratch_shapes=[
            pltpu.VMEM((n_lvl + 1, s, hd), BF16),
            pltpu.VMEM((n_lvl + 1, s, hd), BF16),
            pltpu.VMEM((s, hd), BF16),
            pltpu.VMEM((s, hd), BF16),
            pltpu.VMEM((s, hd), F32),
            pltpu.VMEM((s, hd), BF16),
            pltpu.VMEM((s, hd), F32),
            pltpu.VMEM((s // HGRN_CHUNK, hd, hd), F32),
            pltpu.VMEM((s // HGRN_CHUNK, hd, hd), BF16),
        ],
        compiler_params=_params("parallel", "parallel"),
        name="hgrn_recurrence",
    )(lb.reshape(1, -1), norm_g.reshape(1, hd), z, z, z, z)


def _sortable(score):
    score = jnp.where(score == 0.0, 0.0, score)
    bits = lax.bitcast_convert_type(score, jnp.int32)
    return bits ^ ((bits >> 31) & jnp.int32(0x7FFFFFFF))


def _dsa_select(c, qi_ref, ki_ref, wi_ref, qsel_ref, kib_ref, wt_ref, sc_ref, keys_ref, mask_ref, n_sel):
    kv_len = (c + 1) * TQ
    lo = lax.broadcasted_iota(jnp.int32, (TQ, LANES), 1) < HEAD_DIM
    for p in range(IDX_HEADS // 2):
        qp = qi_ref[:, p * LANES:(p + 1) * LANES]
        qsel_ref[2 * p] = _head_select(qp, lo, True)
        qsel_ref[2 * p + 1] = _head_select(qp, lo, False)
    wt_ref[...] = wi_ref[...].T * (IDX_HEADS ** -0.5 * IDX_DIM ** -0.5)
    kib_ref[0:kv_len, :] = ki_ref[0:kv_len, :].astype(BF16)
    sc_ref[0:kv_len, :] = jnp.zeros((kv_len, TQ), F32)

    def head(hh, carry):
        logit = _dot_nt(kib_ref[0:kv_len, :], qsel_ref[hh])
        sc_ref[0:kv_len, :] += jnp.maximum(logit, 0.0) * wt_ref[pl.ds(hh, 1), :]
        return carry

    lax.fori_loop(0, IDX_HEADS, head, 0)

    key_i = lax.broadcasted_iota(jnp.int32, (TQ, TQ), 0)
    qry_i = lax.broadcasted_iota(jnp.int32, (TQ, TQ), 1)
    if c > 0:
        keys_ref[0:c * TQ, :] = _sortable(sc_ref[0:c * TQ, :])
    keys_ref[c * TQ:kv_len, :] = _sortable(jnp.where(key_i <= qry_i, sc_ref[c * TQ:kv_len, :], NEG_INF))

    def count(pred):
        return jnp.sum(pred(keys_ref[0:kv_len, :]).astype(jnp.int32), axis=0, keepdims=True)

    def search(it, thr):
        trial = thr ^ (jnp.int32(1) << (31 - it))
        return jnp.where(count(lambda kj: kj >= trial) >= n_sel, trial, thr)

    thr = lax.fori_loop(0, 32, search, jnp.full((1, TQ), jnp.iinfo(jnp.int32).min, jnp.int32))
    need = (n_sel - count(lambda kj: kj > thr)).astype(F32)
    earlier = (qry_i < key_i).astype(BF16)
    base = jnp.zeros((1, TQ), F32)
    for j in range(c + 1):
        kj = keys_ref[j * TQ:(j + 1) * TQ, :]
        eq = kj == thr
        eq_f = jnp.where(eq, 1.0, 0.0)
        rank = base + _dot(earlier, eq_f.astype(BF16))
        sel = (kj > thr) | (eq & (rank < need))
        mask_ref[:, j * TQ:(j + 1) * TQ] = jnp.where(sel, 0.0, NEG_INF).T.astype(mask_ref.dtype)
        base = base + jnp.sum(eq_f, axis=0, keepdims=True)
    s_len = mask_ref.shape[1]
    if kv_len < s_len:
        mask_ref[:, kv_len:] = jnp.full((TQ, s_len - kv_len), NEG_INF, mask_ref.dtype)


def _dsa_select_kernel(qi_ref, ki_ref, wi_ref, mask_ref, qsel_ref, kib_ref, wt_ref, sc_ref, keys_ref, *, n_sel, nq):
    i = pl.program_id(1)
    for c in range(nq):
        pl.when(i == c)(functools.partial(
            _dsa_select, c, qi_ref, ki_ref, wi_ref, qsel_ref, kib_ref, wt_ref, sc_ref, keys_ref, mask_ref, n_sel))


def _dsa_attend_kernel(tab_ref, q_ref, k_ref, v_ref, mask_ref, bias_ref, o_ref):
    g = pl.program_id(1)
    last = NUM_BUCKETS - 1
    far_bias = jnp.concatenate([jnp.full((TQ, 1), tab_ref[last, 4 * g + hh], F32) for hh in range(4)], axis=0)
    lo = lax.broadcasted_iota(jnp.int32, (TQ, LANES), 1) < HEAD_DIM
    for c in range(q_ref.shape[0] // TQ):
        rows = slice(c * TQ, (c + 1) * TQ)
        kv_len = (c + 1) * TQ
        stack = _stack_heads(q_ref[rows, :] * HEAD_DIM ** -0.5)
        s = _causal_scores(stack, k_ref, kv_len, bias_ref, far_bias)
        s = s + jnp.concatenate([mask_ref[rows, 0:kv_len].astype(F32)] * 4, axis=0)
        p = jnp.exp(s - jnp.max(s, axis=-1, keepdims=True))
        inv = 1.0 / jnp.sum(p, axis=-1, keepdims=True)
        p = p.astype(BF16)
        vv = v_ref[0:kv_len, :]
        lo_v = lax.broadcasted_iota(jnp.int32, (kv_len, LANES), 1) < HEAD_DIM
        v_both = jnp.concatenate([_head_select(vv, lo_v, True), _head_select(vv, lo_v, False)], axis=0)
        outs = []
        for pr in range(2):
            r0 = 2 * pr * TQ
            p_both = jnp.concatenate([p[r0:r0 + TQ], p[r0 + TQ:r0 + 2 * TQ]], axis=1)
            scale = jnp.where(lo, inv[r0:r0 + TQ], inv[r0 + TQ:r0 + 2 * TQ])
            outs.append(_dot(p_both, v_both) * scale)
        o_ref[rows, :] = jnp.concatenate(outs, axis=1).astype(o_ref.dtype)


def dsa_attention(z, z_idx, table, bias_causal, batch):
    t = z.shape[0]
    s = t // batch
    nq = s // TQ
    n_sel = min(TOPK_MAX, s // 4)
    k_col0 = N_HEADS * HEAD_DIM // LANES
    v_col0 = k_col0 + N_KV_HEADS
    qi_col = (N_HEADS * HEAD_DIM + 4 * N_KV_HEADS * HEAD_DIM) // (IDX_HEADS * IDX_DIM)
    mask = pl.pallas_call(
        functools.partial(_dsa_select_kernel, n_sel=n_sel, nq=nq),
        grid=(batch, nq),
        in_specs=[
            pl.BlockSpec((TQ, IDX_HEADS * IDX_DIM), lambda b, i: (b * nq + i, qi_col)),
            pl.BlockSpec((s, LANES), lambda b, i: (b, 0)),
            pl.BlockSpec((TQ, LANES), lambda b, i: (b * nq + i, 1)),
        ],
        out_specs=pl.BlockSpec((TQ, s), lambda b, i: (b * nq + i, 0)),
        out_shape=jax.ShapeDtypeStruct((t, s), BF16),
        scratch_shapes=[
            pltpu.VMEM((IDX_HEADS, TQ, LANES), BF16),
            pltpu.VMEM((s, LANES), BF16),
            pltpu.VMEM((LANES, TQ), F32),
            pltpu.VMEM((s, TQ), F32),
            pltpu.VMEM((s, TQ), jnp.int32),
        ],
        compiler_params=_params("parallel", "parallel"),
        name="dsa_select",
    )(z, z_idx, z_idx)
    return pl.pallas_call(
        _dsa_attend_kernel,
        grid=(batch, N_KV_HEADS),
        in_specs=[
            pl.BlockSpec(memory_space=pltpu.SMEM),
            pl.BlockSpec((s, 4 * HEAD_DIM), lambda b, g: (b, g)),
            pl.BlockSpec((s, LANES), lambda b, g: (b, k_col0 + g)),
            pl.BlockSpec((s, LANES), lambda b, g: (b, v_col0 + g)),
            pl.BlockSpec((s, s), lambda b, g: (b, 0)),
            pl.BlockSpec((4, TQ, 2 * TQ), lambda b, g: (g, 0, 0)),
        ],
        out_specs=pl.BlockSpec((s, 4 * HEAD_DIM), lambda b, g: (b, g)),
        out_shape=jax.ShapeDtypeStruct((t, N_HEADS * HEAD_DIM), BF16),
        compiler_params=_params("parallel", "parallel"),
        name="dsa_attend",
    )(table, z, z, z, mask, bias_causal)


def _lower_bound_kernel(x_ref, o_ref):
    x = x_ref[...]
    e = jnp.exp(x - jnp.max(x, axis=0, keepdims=True))
    soft = e / jnp.sum(e, axis=0, keepdims=True)
    run = soft[0:1]
    o_ref[0:1, :] = run - soft[0:1]
    for r in range(1, x.shape[0]):
        run = run + soft[r:r + 1]
        o_ref[r:r + 1, :] = run - soft[0:1]


def hgrn_lower_bounds(logits):
    return pl.pallas_call(
        _lower_bound_kernel,
        out_shape=jax.ShapeDtypeStruct(logits.shape, F32),
        name="hgrn_lower_bounds",
    )(logits)


def _dup_heads(w, n_heads):
    k = w.shape[0]
    w = w.reshape(k, n_heads, 1, HEAD_DIM)
    return jnp.broadcast_to(w, (k, n_heads, 2, HEAD_DIM)).reshape(k, n_heads * 2 * HEAD_DIM)


def kernel(x, rel_bias_table, hgrn_lb_logits, norm_g, ffn_w_up, ffn_conv, ffn_w_down, swa_w_in, swa_w_out,
           swa_sinks, diff_w_in, diff_w_out, diff_lambda, diff_subln_g, hgrn_w_in, hgrn_w_out, hgrn_norm_g,
           dsa_w_in, dsa_w_out):
    batch, seq, d = x.shape
    t = batch * seq
    depth = norm_g.shape[0]
    n_mixers = 4
    bias_causal, bias_window = bias_tiles(rel_bias_table)
    lb_all = hgrn_lower_bounds(hgrn_lb_logits)
    kv = N_KV_HEADS * HEAD_DIM
    x2 = x.reshape(t, d)
    h = norm_cast(x2, norm_g[0, 0])
    for i in range(depth):
        kind, j = i % n_mixers, i // n_mixers
        if kind == 0:
            wq, wk, wv = jnp.split(swa_w_in[j], [d, d + kv], axis=1)
            w_in = jnp.concatenate([wq, _dup_heads(wk, N_KV_HEADS), _dup_heads(wv, N_KV_HEADS)], axis=1)
            z = matmul(h, w_in.astype(BF16), BF16)
            o = swa_attention(z, swa_sinks[j], bias_window, batch)
            w_out = swa_w_out[j]
        elif kind == 1:
            z = matmul(h, diff_w_in[j].astype(BF16), BF16)
            o = diff_attention(z, rel_bias_table, diff_lambda[j], diff_subln_g[j], bias_causal, batch, i)
            w_out = diff_w_out[j]
        elif kind == 2:
            z = matmul(h, hgrn_w_in[j].astype(BF16), F32)
            o = hgrn_recurrence(z, lb_all[i], hgrn_norm_g[j], batch)
            w_out = hgrn_w_out[j]
        else:
            n_qi = IDX_HEADS * IDX_DIM
            wq, wk, wv, wqi, wki, wwi = jnp.split(
                dsa_w_in[j], [d, d + kv, d + 2 * kv, d + 2 * kv + n_qi, d + 2 * kv + n_qi + IDX_DIM], axis=1)
            w_main = jnp.concatenate([wq, _dup_heads(wk, N_KV_HEADS), _dup_heads(wv, N_KV_HEADS), wqi], axis=1)
            w_idx = jnp.concatenate([wki, wki, wwi, jnp.zeros((d, LANES - IDX_HEADS), F32)], axis=1)
            z = matmul(h, w_main.astype(BF16), BF16)
            z_idx = matmul(h, w_idx.astype(BF16), F32)
            o = dsa_attention(z, z_idx, rel_bias_table, bias_causal, batch)
            w_out = dsa_w_out[j]
        x2, h = proj_residual(o, w_out.astype(BF16), x2, norm_g[i, 1], norm_g[i, 2])
        a = ffn_up(h, ffn_w_up[i].astype(BF16), ffn_conv[i], batch)
        x2, h = proj_residual(a, ffn_w_down[i].astype(BF16), x2, norm_g[i, 3], norm_g[(i + 1) % depth, 0])
    return x2.reshape(batch, seq, d)
```

```python
import functools
import math

import jax
import jax.numpy as jnp
import numpy as np
from jax import lax
from jax.experimental import pallas as pl
from jax.experimental.pallas import tpu as pltpu

D_MODEL = 2048
HEAD_DIM = 64
N_HEADS = 32
N_KV_HEADS = 8
WINDOW = 128
DIFF_HEADS = 16
HGRN_HEADS = 16
HGRN_HEAD_DIM = 128
HGRN_CHUNK = 64
IDX_HEADS = 16
IDX_DIM = 64
TOPK_MAX = 256
D_FF = 5632
CONV_WIDTH = 3
NUM_BUCKETS = 32
MAX_DISTANCE = 128
RMS_EPS = 1e-6

LANES = 128
TQ = 256
VMEM_LIMIT = 56 * 1024 * 1024
NEG_INF = float("-inf")
LOG2_E = math.log2(math.e)
BF16 = jnp.bfloat16
F32 = jnp.float32


def _params(*sem):
    return pltpu.CompilerParams(dimension_semantics=sem, vmem_limit_bytes=VMEM_LIMIT)


def _dot(a, b):
    return jnp.dot(a, b, preferred_element_type=F32)


def _dot_nt(a, b):
    return lax.dot_general(a, b, (((1,), (1,)), ((), ())), preferred_element_type=F32)


def _dot_tn(a, b):
    return lax.dot_general(a, b, (((0,), (0,)), ((), ())), preferred_element_type=F32)


def _rms(x, g):
    return x * lax.rsqrt(jnp.mean(x * x, axis=-1, keepdims=True) + RMS_EPS) * g


def _silu(x):
    return x * (1.0 / (1.0 + jnp.exp(-x)))


def _norm_cast_kernel(x_ref, g_ref, o_ref):
    o_ref[...] = _rms(x_ref[...], g_ref[...]).astype(o_ref.dtype)


def norm_cast(x, g):
    t, d = x.shape
    tm = 512
    return pl.pallas_call(
        _norm_cast_kernel,
        grid=(t // tm,),
        in_specs=[pl.BlockSpec((tm, d), lambda i: (i, 0)), pl.BlockSpec((1, d), lambda i: (0, 0))],
        out_specs=pl.BlockSpec((tm, d), lambda i: (i, 0)),
        out_shape=jax.ShapeDtypeStruct((t, d), BF16),
        compiler_params=_params("parallel"),
        name="norm_cast",
    )(x, g.reshape(1, d))


def _mm_kernel(x_ref, w_ref, s_ref, o_ref, wb_ref):
    @pl.when(pl.program_id(1) == 0)
    def _():
        wb_ref[...] = w_ref[...].astype(BF16)

    o_ref[...] = (_dot(x_ref[...], wb_ref[...]) * s_ref[...]).astype(o_ref.dtype)


def matmul(x, w, out_dtype, col_scale=None):
    t, k = x.shape
    n = w.shape[1]
    tm = 1024
    tn = next(c for c in (1024, 512, 256, 128) if n % c == 0)
    if col_scale is None:
        col_scale = jnp.ones((n,), F32)
    return pl.pallas_call(
        _mm_kernel,
        grid=(n // tn, t // tm),
        in_specs=[pl.BlockSpec((tm, k), lambda j, i: (i, 0)), pl.BlockSpec((k, tn), lambda j, i: (0, j)),
                  pl.BlockSpec((1, tn), lambda j, i: (0, j))],
        out_specs=pl.BlockSpec((tm, tn), lambda j, i: (i, j)),
        out_shape=jax.ShapeDtypeStruct((t, n), out_dtype),
        scratch_shapes=[pltpu.VMEM((k, tn), BF16)],
        compiler_params=_params("parallel", "arbitrary"),
        name="in_proj",
    )(x, w, col_scale.reshape(1, n))


def _proj_res_kernel(a_ref, w_ref, x_ref, go_ref, gn_ref, xo_ref, h_ref):
    xn = x_ref[...] + _rms(_dot(a_ref[...], w_ref[...]), go_ref[...])
    xo_ref[...] = xn
    h_ref[...] = _rms(xn, gn_ref[...]).astype(h_ref.dtype)


def _proj_rows(k, d):
    for tm in (512, 256, 128):
        streamed = 2 * (tm * k * 2 + 2 * tm * d * 4 + tm * d * 2)
        temporaries = 3 * tm * d * 4
        if k * d * 2 + streamed + temporaries <= VMEM_LIMIT - (4 << 20):
            return tm
    raise ValueError("projection weight does not fit in VMEM")


def proj_residual(a, w, layer, x, g_out, g_next):
    t, k = a.shape
    d = w.shape[2]
    tm = _proj_rows(k, d)
    return pl.pallas_call(
        _proj_res_kernel,
        grid=(t // tm,),
        in_specs=[
            pl.BlockSpec((tm, k), lambda i: (i, 0)),
            pl.BlockSpec((None, k, d), lambda i: (layer, 0, 0), pipeline_mode=pl.Buffered(1)),
            pl.BlockSpec((tm, d), lambda i: (i, 0)),
            pl.BlockSpec((1, d), lambda i: (0, 0)),
            pl.BlockSpec((1, d), lambda i: (0, 0)),
        ],
        out_specs=[pl.BlockSpec((tm, d), lambda i: (i, 0)), pl.BlockSpec((tm, d), lambda i: (i, 0))],
        out_shape=[jax.ShapeDtypeStruct((t, d), F32), jax.ShapeDtypeStruct((t, d), BF16)],
        compiler_params=_params("parallel"),
        name="proj_residual",
    )(a, w, x, g_out.reshape(1, d), g_next.reshape(1, d))


FFN_ROWS = 512
FFN_COLS = 512


def _ffn_up_kernel(h_ref, wg_ref, wv_ref, cg_ref, cv_ref, o_ref):
    s = h_ref.shape[0]
    rows = min(FFN_ROWS, s)
    tn = o_ref.shape[1]
    row = lax.broadcasted_iota(jnp.int32, (rows, 1), 0)
    wg, wv = wg_ref[...].astype(BF16), wv_ref[...].astype(BF16)
    cg, cv = cg_ref[...], cv_ref[...]

    def conv(u, tail, c):
        u1 = jnp.where(row >= 1, pltpu.roll(u, 1, 0), tail[1:2])
        u2 = jnp.where(row >= 2, pltpu.roll(u, 2, 0), jnp.where(row == 1, tail[1:2], tail[0:1]))
        return (c[0:1] * u2 + c[1:2] * u1) + c[2:3] * u

    tail_g = tail_v = jnp.zeros((CONV_WIDTH - 1, tn), F32)
    for r in range(s // rows):
        hr = h_ref[r * rows:(r + 1) * rows, :]
        ug, uv = _dot(hr, wg), _dot(hr, wv)
        out = _silu(conv(ug, tail_g, cg)) * conv(uv, tail_v, cv)
        o_ref[r * rows:(r + 1) * rows, :] = out.astype(o_ref.dtype)
        tail_g, tail_v = ug[rows - 2:rows], uv[rows - 2:rows]


def ffn_up(h, w_up, conv_w, layer, batch):
    t, d = h.shape
    s = t // batch
    tn = FFN_COLS
    nj = D_FF // tn
    return pl.pallas_call(
        _ffn_up_kernel,
        grid=(batch, nj),
        in_specs=[
            pl.BlockSpec((s, d), lambda b, j: (b, 0)),
            pl.BlockSpec((None, d, tn), lambda b, j: (layer, 0, j)),
            pl.BlockSpec((None, d, tn), lambda b, j: (layer, 0, nj + j)),
            pl.BlockSpec((None, CONV_WIDTH, tn), lambda b, j: (layer, 0, j)),
            pl.BlockSpec((None, CONV_WIDTH, tn), lambda b, j: (layer, 0, nj + j)),
        ],
        out_specs=pl.BlockSpec((s, tn), lambda b, j: (b, j)),
        out_shape=jax.ShapeDtypeStruct((t, D_FF), BF16),
        compiler_params=_params("parallel", "parallel"),
        name="ffn_up",
    )(h, w_up, w_up, conv_w, conv_w)


def _bucket_thresholds():
    max_exact = NUM_BUCKETS // 2
    n = np.arange(1, 2 * MAX_DISTANCE, dtype=np.float64)
    large = max_exact + np.floor(np.log(n / max_exact) / math.log(MAX_DISTANCE / max_exact) * (NUM_BUCKETS - max_exact))
    bucket = np.where(n < max_exact, n, np.minimum(large, NUM_BUCKETS - 1)).astype(np.int64)
    return tuple(int(n[bucket >= b][0]) for b in range(max_exact + 1, NUM_BUCKETS))


BUCKET_THRESHOLDS = _bucket_thresholds()


def _bias_tile_kernel(tab_ref, causal_ref, window_ref):
    m = pl.program_id(0)
    r = lax.broadcasted_iota(jnp.int32, (TQ, 2 * TQ), 0)
    c = lax.broadcasted_iota(jnp.int32, (TQ, 2 * TQ), 1)
    d = r - c + TQ
    n = jnp.maximum(d, 0)
    max_exact = NUM_BUCKETS // 2
    large = jnp.full_like(n, max_exact)
    for thr in BUCKET_THRESHOLDS:
        large = large + (n >= thr).astype(jnp.int32)
    bucket = jnp.where(n < max_exact, n, large)
    val = jnp.zeros((TQ, 2 * TQ), F32)
    for b in range(NUM_BUCKETS):
        val = jnp.where(bucket == b, tab_ref[b, m] * LOG2_E, val)
    causal = jnp.where(d >= 0, val, NEG_INF)
    causal_ref[0] = causal
    window_ref[0] = jnp.where(d < WINDOW, causal, NEG_INF)


def bias_tiles(table):
    shape = jax.ShapeDtypeStruct((N_HEADS, TQ, 2 * TQ), F32)
    return pl.pallas_call(
        _bias_tile_kernel,
        grid=(N_HEADS,),
        in_specs=[pl.BlockSpec(memory_space=pltpu.SMEM)],
        out_specs=[pl.BlockSpec((1, TQ, 2 * TQ), lambda m: (m, 0, 0))] * 2,
        out_shape=[shape, shape],
        compiler_params=_params("parallel"),
        name="bias_tiles",
    )(table)


def _head_select(x, lo, first):
    zero = jnp.zeros_like(x)
    return jnp.where(lo, x, zero) if first else jnp.where(lo, zero, x)


def _stack_heads(q):
    rows = q.shape[0]
    lo = lax.broadcasted_iota(jnp.int32, (rows, LANES), 1) < HEAD_DIM
    parts = []
    for p in range(q.shape[1] // LANES):
        qp = q[:, p * LANES:(p + 1) * LANES]
        parts += [_head_select(qp, lo, True), _head_select(qp, lo, False)]
    return jnp.concatenate(parts, axis=0)


def _pair_selector(n_keys):
    r = lax.broadcasted_iota(jnp.int32, (2 * n_keys, LANES), 0)
    c = lax.broadcasted_iota(jnp.int32, (2 * n_keys, LANES), 1)
    return jnp.where((r < n_keys) == (c < HEAD_DIM), 1.0, 0.0).astype(BF16)


def _paired_values(vv):
    lo = lax.broadcasted_iota(jnp.int32, vv.shape, 1) < HEAD_DIM
    v_both = jnp.concatenate([_head_select(vv, lo, True), _head_select(vv, lo, False)], axis=0)
    return jnp.concatenate([v_both, _pair_selector(vv.shape[0])], axis=1)


SWA_BLOCK = WINDOW


def _swa_kernel(sink_ref, q_ref, kp_ref, ko_ref, vp_ref, vo_ref, bias_ref, o_ref):
    i = pl.program_id(1)
    blk = SWA_BLOCK
    col = lax.broadcasted_iota(jnp.int32, (1, 2 * blk), 1)
    no_prev = (col < blk) & (i == 0)
    lo = lax.broadcasted_iota(jnp.int32, (blk, LANES), 1) < HEAD_DIM

    def scores(g):
        stack = _stack_heads(q_ref[:, g * 4 * HEAD_DIM:(g + 1) * 4 * HEAD_DIM])
        cols = slice(g * LANES, (g + 1) * LANES)
        kk = jnp.concatenate([kp_ref[:, cols], ko_ref[:, cols]], axis=0)
        s = _dot_nt(stack, kk) + bias_ref[4 * g:4 * g + 4].reshape(4 * blk, 2 * blk)
        return jnp.where(no_prev, NEG_INF, s)

    def softmax(g, s):
        sink = jnp.concatenate([jnp.full((blk, 1), sink_ref[4 * g + hh] * LOG2_E, F32) for hh in range(4)], axis=0)
        m = jnp.maximum(jnp.max(s, axis=-1, keepdims=True), sink)
        return jnp.exp2(s - m).astype(BF16), jnp.exp2(sink - m)

    def finish(g, e, e_sink):
        cols = slice(g * LANES, (g + 1) * LANES)
        v_ext = _paired_values(jnp.concatenate([vp_ref[:, cols], vo_ref[:, cols]], axis=0))
        outs = []
        for p in range(2):
            r0 = 2 * p * blk
            e_both = jnp.concatenate([e[r0:r0 + blk], e[r0 + blk:r0 + 2 * blk]], axis=1)
            res = _dot(e_both, v_ext)
            den = res[:, LANES:] + jnp.where(lo, e_sink[r0:r0 + blk], e_sink[r0 + blk:r0 + 2 * blk])
            outs.append(res[:, :LANES] / den)
        o_ref[:, g * 4 * HEAD_DIM:(g + 1) * 4 * HEAD_DIM] = jnp.concatenate(outs, axis=1).astype(o_ref.dtype)

    s_next = scores(0)
    prev = None
    for g in range(N_KV_HEADS):
        s = s_next
        if g + 1 < N_KV_HEADS:
            s_next = scores(g + 1)
        cur = softmax(g, s)
        if prev is not None:
            finish(g - 1, *prev)
        prev = cur
    finish(N_KV_HEADS - 1, *prev)


def swa_attention(z, sinks, bias_window, batch):
    t = z.shape[0]
    blk = SWA_BLOCK
    nq = t // batch // blk
    dq = N_HEADS * HEAD_DIM
    dk = 2 * N_KV_HEADS * HEAD_DIM
    k_col, v_col = dq // dk, dq // dk + 1

    def own(col):
        return lambda b, i: (b * nq + i, col)

    def prev(col):
        return lambda b, i: (b * nq + jnp.maximum(i - 1, 0), col)

    bias_spec = pl.BlockSpec((N_HEADS, blk, 2 * blk), lambda b, i: (0, TQ // blk - 1, TQ // blk - 1))
    return pl.pallas_call(
        _swa_kernel,
        grid=(batch, nq),
        in_specs=[
            pl.BlockSpec(memory_space=pltpu.SMEM),
            pl.BlockSpec((blk, dq), own(0)),
            pl.BlockSpec((blk, dk), prev(k_col)),
            pl.BlockSpec((blk, dk), own(k_col)),
            pl.BlockSpec((blk, dk), prev(v_col)),
            pl.BlockSpec((blk, dk), own(v_col)),
            bias_spec,
        ],
        out_specs=pl.BlockSpec((blk, dq), own(0)),
        out_shape=jax.ShapeDtypeStruct((t, dq), BF16),
        compiler_params=_params("parallel", "parallel"),
        name="swa_attention",
    )(sinks, z, z, z, z, z, bias_window)


def _causal_scores(stack, k_ref, kv_len, bias_ref, far_bias):
    rows = stack.shape[0]
    s = _dot_nt(stack, k_ref[0:kv_len, :])
    if kv_len == TQ:
        return s + bias_ref[:, :, TQ:2 * TQ].reshape(rows, TQ)
    near = bias_ref[...].reshape(rows, 2 * TQ)
    if kv_len == 2 * TQ:
        return s + near
    n_far = kv_len - 2 * TQ
    return jnp.concatenate([s[:, :n_far] + far_bias, s[:, n_far:] + near], axis=1)


def _softmax_pipeline(n_blocks, scores, finish):
    order = list(range(n_blocks - 1, -1, -1))
    s_next = scores(order[0])
    p_prev = None
    for n in range(n_blocks):
        s = s_next
        if n + 1 < n_blocks:
            s_next = scores(order[n + 1])
        p = jnp.exp2(s - jnp.max(s, axis=-1, keepdims=True)).astype(BF16)
        if p_prev is not None:
            finish(order[n - 1], p_prev)
        p_prev = p
    finish(order[-1], p_prev)


def _diff_kernel(tab_ref, lam_ref, g_ref, q_ref, k_ref, v_ref, bias_ref, o_ref, *, lam_init):
    h = pl.program_id(0)
    last = NUM_BUCKETS - 1
    far_bias = jnp.concatenate([jnp.full((TQ, 1), tab_ref[last, 2 * h] * LOG2_E, F32),
                                jnp.full((TQ, 1), tab_ref[last, 2 * h + 1] * LOG2_E, F32)], axis=0)
    lf = lam_ref[...]
    lam = (jnp.exp(jnp.sum(lf[0:1] * lf[1:2], axis=-1, keepdims=True))
           - jnp.exp(jnp.sum(lf[2:3] * lf[3:4], axis=-1, keepdims=True)) + lam_init)
    ones = jnp.ones((q_ref.shape[0], LANES), BF16)

    def scores(c):
        stack = _stack_heads(q_ref[c * TQ:(c + 1) * TQ, :])
        return _causal_scores(stack, k_ref, (c + 1) * TQ, bias_ref, far_bias)

    def finish(c, p):
        kv_len = (c + 1) * TQ
        v_ext = jnp.concatenate([v_ref[0:kv_len, :], ones[0:kv_len]], axis=1)
        res = _dot(p, v_ext)
        o12 = res[:, :LANES] / res[:, LANES:]
        o = o12[:TQ] - lam * o12[TQ:]
        o_ref[c * TQ:(c + 1) * TQ, :] = (_rms(o, g_ref[...]) * (1.0 - lam_init)).astype(o_ref.dtype)

    _softmax_pipeline(q_ref.shape[0] // TQ, scores, finish)


def diff_attention(z, table, lambdas, subln_g, bias_causal, batch, layer_idx):
    t = z.shape[0]
    s = t // batch
    lam_init = 0.8 - 0.6 * math.exp(-0.3 * layer_idx)
    return pl.pallas_call(
        functools.partial(_diff_kernel, lam_init=lam_init),
        grid=(DIFF_HEADS, batch),
        in_specs=[
            pl.BlockSpec(memory_space=pltpu.SMEM),
            pl.BlockSpec((4, HEAD_DIM), lambda h, b: (0, 0)),
            pl.BlockSpec((1, LANES), lambda h, b: (0, 0)),
            pl.BlockSpec((s, LANES), lambda h, b: (b, h)),
            pl.BlockSpec((s, LANES), lambda h, b: (b, DIFF_HEADS + h)),
            pl.BlockSpec((s, LANES), lambda h, b: (b, 2 * DIFF_HEADS + h)),
            pl.BlockSpec((2, TQ, 2 * TQ), lambda h, b: (h, 0, 0)),
        ],
        out_specs=pl.BlockSpec((s, LANES), lambda h, b: (b, h)),
        out_shape=jax.ShapeDtypeStruct((t, DIFF_HEADS * 2 * HEAD_DIM), BF16),
        compiler_params=_params("parallel", "parallel"),
        name="diff_attention",
    )(table, lambdas, subln_g.reshape(1, LANES), z, z, z, bias_causal)


HGRN_LEVELS = tuple(2 ** e for e in range(int(math.log2(HGRN_CHUNK)) - 1, -1, -1))
HGRN_GROUP = 256


def _hgrn_kernel(lb_ref, g_ref, q_ref, f_ref, i_ref, og_ref, o_ref,
                 qf_ref, kf_ref, bq_ref, kd_ref, dec_ref, v_ref):
    s_len, hd = q_ref.shape
    c_len = HGRN_CHUNK
    n_lvl = len(HGRN_LEVELS)
    grp = min(HGRN_GROUP, s_len)
    rc = lax.broadcasted_iota(jnp.int32, (grp, hd), 0) & (c_len - 1)
    lb = lb_ref[...]
    log_lb, log_1m_lb = jnp.log(lb), jnp.log1p(-lb)

    def log1p01(u):
        w = 1.0 + u
        return jnp.where(w == 1.0, u, jnp.log(w) * (u / (w - 1.0)))

    def prepare(gi):
        rows = slice(gi * grp, (gi + 1) * grp)
        z = f_ref[rows, :]
        log_sig = jnp.minimum(z, 0.0) - log1p01(jnp.exp(-jnp.abs(z)))
        c = log_1m_lb + log_sig
        delta = log_lb - c
        log_f = jnp.where(jnp.isnan(delta), log_lb + c,
                          jnp.maximum(log_lb, c) + log1p01(jnp.exp(-jnp.abs(delta))))
        key = (1.0 - lb) * (1.0 / (1.0 + jnp.exp(z)))
        qs = _silu(q_ref[rows, :])
        v_ref[rows, :] = i_ref[rows, :].astype(BF16)

        b = log_f * LOG2_E
        for d in (1, 2, 4, 8, 16, 32):
            b = b + jnp.where(rc >= d, pltpu.roll(b, d, 0), 0.0)

        bq_ref[rows, :] = (qs * jnp.exp2(b)).astype(BF16)
        qf_ref[n_lvl, rows, :] = qs.astype(BF16)
        kf_ref[n_lvl, rows, :] = key.astype(BF16)

        r_m = jnp.where(rc >= 1, pltpu.roll(b, 1, 0), 0.0)
        for m in (1, 2, 4, 8, 16, 32):
            if m > 1:
                half = m // 2
                r_m = jnp.where((rc & half) != 0, pltpu.roll(r_m, half, 0), r_m)
            e_m = pltpu.roll(r_m, grp - m, 0)
            lvl = HGRN_LEVELS.index(m)
            qf_ref[lvl, rows, :] = (qs * jnp.exp2(b - r_m)).astype(BF16)
            kf_ref[lvl, rows, :] = (key * jnp.exp2(jnp.minimum(e_m - b, 0.0))).astype(BF16)
        b3 = b.reshape(grp // c_len, c_len, hd)
        b_end = jnp.broadcast_to(b3[:, c_len - 1:c_len, :], b3.shape).reshape(grp, hd)
        kd_ref[rows, :] = (key * jnp.exp2(b_end - b)).astype(BF16)
        dec_ref[rows, :] = jnp.exp2(b_end)

    ti = lax.broadcasted_iota(jnp.int32, (c_len, c_len), 0)
    si = lax.broadcasted_iota(jnp.int32, (c_len, c_len), 1)
    masks = [((ti // m) % 2 == 1) & (si // m == ti // m - 1) for m in HGRN_LEVELS] + [ti == si]
    g = g_ref[...]

    def chunk(ci, st):
        rows = slice(ci * c_len, (ci + 1) * c_len)
        att = jnp.zeros((c_len, c_len), F32)
        for lvl in range(n_lvl + 1):
            att = att + jnp.where(masks[lvl], _dot_nt(qf_ref[lvl, rows, :], kf_ref[lvl, rows, :]), 0.0)
        vc = v_ref[rows, :]
        o = _dot(att.astype(BF16), vc) + _dot_nt(bq_ref[rows, :], st.astype(BF16))
        o_ref[rows, :] = (_rms(o, g) * _silu(og_ref[rows, :])).astype(o_ref.dtype)
        return st * dec_ref[ci * c_len:ci * c_len + 1, :] + _dot_tn(vc, kd_ref[rows, :])

    n_groups = s_len // grp
    per_group = grp // c_len
    st = jnp.zeros((hd, hd), F32)
    prepare(0)
    for gi in range(n_groups):
        if gi + 1 < n_groups:
            prepare(gi + 1)
        for ci in range(gi * per_group, (gi + 1) * per_group):
            st = chunk(ci, st)


def hgrn_recurrence(z, lb, norm_g, batch):
    t = z.shape[0]
    s = t // batch
    hd = HGRN_HEAD_DIM
    n_lvl = len(HGRN_LEVELS)

    def part(p):
        return pl.BlockSpec((s, hd), lambda b, h: (b, p * HGRN_HEADS + h))

    return pl.pallas_call(
        _hgrn_kernel,
        grid=(batch, HGRN_HEADS),
        in_specs=[pl.BlockSpec((1, hd), lambda b, h: (0, h)), pl.BlockSpec((1, hd), lambda b, h: (0, 0)),
                  part(0), part(1), part(2), part(3)],
        out_specs=pl.BlockSpec((s, hd), lambda b, h: (b, h)),
        out_shape=jax.ShapeDtypeStruct((t, HGRN_HEADS * hd), BF16),
        scratch_shapes=[
            pltpu.VMEM((n_lvl + 1, s, hd), BF16),
            pltpu.VMEM((n_lvl + 1, s, hd), BF16),
            pltpu.VMEM((s, hd), BF16),
            pltpu.VMEM((s, hd), BF16),
            pltpu.VMEM((s, hd), F32),
            pltpu.VMEM((s, hd), BF16),
        ],
        compiler_params=_params("parallel", "parallel"),
        name="hgrn_recurrence",
    )(lb.reshape(1, -1), norm_g.reshape(1, hd), z, z, z, z)


def _sortable(score):
    score = jnp.where(score == 0.0, 0.0, score)
    bits = lax.bitcast_convert_type(score, jnp.int32)
    return bits ^ ((bits >> 31) & jnp.int32(0x7FFFFFFF))


def _dsa_select(c, qi_ref, ki_ref, wi_ref, qsel_ref, kib_ref, wt_ref, sc_ref, keys_ref, mask_ref, n_sel):
    kv_len = (c + 1) * TQ
    lo = lax.broadcasted_iota(jnp.int32, (TQ, LANES), 1) < HEAD_DIM
    for p in range(IDX_HEADS // 2):
        qp = qi_ref[:, p * LANES:(p + 1) * LANES]
        qsel_ref[2 * p] = _head_select(qp, lo, True)
        qsel_ref[2 * p + 1] = _head_select(qp, lo, False)
    wt_ref[...] = wi_ref[...].T * (IDX_HEADS ** -0.5 * IDX_DIM ** -0.5)
    kib_ref[0:kv_len, :] = ki_ref[0:kv_len, :].astype(BF16)
    sc_ref[0:kv_len, :] = jnp.zeros((kv_len, TQ), F32)

    def head_pair(hp, carry):
        kib = kib_ref[0:kv_len, :]
        acc = sc_ref[0:kv_len, :]
        for hh in (2 * hp, 2 * hp + 1):
            logit = _dot_nt(kib, qsel_ref[hh])
            acc = acc + jnp.maximum(logit, 0.0) * wt_ref[pl.ds(hh, 1), :]
        sc_ref[0:kv_len, :] = acc
        return carry

    lax.fori_loop(0, IDX_HEADS // 2, head_pair, 0)

    key_i = lax.broadcasted_iota(jnp.int32, (TQ, TQ), 0)
    qry_i = lax.broadcasted_iota(jnp.int32, (TQ, TQ), 1)
    if c > 0:
        keys_ref[0:c * TQ, :] = _sortable(sc_ref[0:c * TQ, :])
    keys_ref[c * TQ:kv_len, :] = _sortable(jnp.where(key_i <= qry_i, sc_ref[c * TQ:kv_len, :], NEG_INF))

    def count(pred):
        return jnp.sum(pred(keys_ref[0:kv_len, :]).astype(jnp.int32), axis=0, keepdims=True)

    def search(it, thr):
        trial = thr ^ (jnp.int32(1) << (31 - it))
        return jnp.where(count(lambda kj: kj >= trial) >= n_sel, trial, thr)

    thr = lax.fori_loop(0, 32, search, jnp.full((1, TQ), jnp.iinfo(jnp.int32).min, jnp.int32))
    need = (n_sel - count(lambda kj: kj > thr)).astype(F32)
    earlier = (qry_i < key_i).astype(BF16)
    base = jnp.zeros((1, TQ), F32)
    for j in range(c + 1):
        kj = keys_ref[j * TQ:(j + 1) * TQ, :]
        eq = kj == thr
        eq_f = jnp.where(eq, 1.0, 0.0)
        rank = base + _dot(earlier, eq_f.astype(BF16))
        sel = (kj > thr) | (eq & (rank < need))
        mask_ref[:, j * TQ:(j + 1) * TQ] = jnp.where(sel, 0.0, NEG_INF).T.astype(mask_ref.dtype)
        base = base + jnp.sum(eq_f, axis=0, keepdims=True)
    s_len = mask_ref.shape[1]
    if kv_len < s_len:
        mask_ref[:, kv_len:] = jnp.full((TQ, s_len - kv_len), NEG_INF, mask_ref.dtype)


def _dsa_select_kernel(qi_ref, ki_ref, wi_ref, mask_ref, qsel_ref, kib_ref, wt_ref, sc_ref, keys_ref, *, n_sel, nq):
    i = pl.program_id(1)
    for c in range(nq):
        pl.when(i == c)(functools.partial(
            _dsa_select, c, qi_ref, ki_ref, wi_ref, qsel_ref, kib_ref, wt_ref, sc_ref, keys_ref, mask_ref, n_sel))


def _dsa_attend_kernel(tab_ref, q_ref, k_ref, v_ref, mask_ref, bias_ref, o_ref):
    g = pl.program_id(1)
    last = NUM_BUCKETS - 1
    far_bias = jnp.concatenate(
        [jnp.full((TQ, 1), tab_ref[last, 4 * g + hh] * LOG2_E, F32) for hh in range(4)], axis=0)

    def scores(c):
        rows = slice(c * TQ, (c + 1) * TQ)
        kv_len = (c + 1) * TQ
        s = _causal_scores(_stack_heads(q_ref[rows, :]), k_ref, kv_len, bias_ref, far_bias)
        return s + jnp.concatenate([mask_ref[rows, 0:kv_len].astype(F32)] * 4, axis=0)

    def finish(c, p):
        v_ext = _paired_values(v_ref[0:(c + 1) * TQ, :])
        outs = []
        for pr in range(2):
            r0 = 2 * pr * TQ
            p_both = jnp.concatenate([p[r0:r0 + TQ], p[r0 + TQ:r0 + 2 * TQ]], axis=1)
            res = _dot(p_both, v_ext)
            outs.append(res[:, :LANES] / res[:, LANES:])
        o_ref[c * TQ:(c + 1) * TQ, :] = jnp.concatenate(outs, axis=1).astype(o_ref.dtype)

    _softmax_pipeline(q_ref.shape[0] // TQ, scores, finish)


def dsa_attention(z, z_idx, table, bias_causal, batch):
    t = z.shape[0]
    s = t // batch
    nq = s // TQ
    n_sel = min(TOPK_MAX, s // 4)
    k_col0 = N_HEADS * HEAD_DIM // LANES
    v_col0 = k_col0 + N_KV_HEADS
    qi_col = (N_HEADS * HEAD_DIM + 4 * N_KV_HEADS * HEAD_DIM) // (IDX_HEADS * IDX_DIM)
    mask = pl.pallas_call(
        functools.partial(_dsa_select_kernel, n_sel=n_sel, nq=nq),
        grid=(batch, nq),
        in_specs=[
            pl.BlockSpec((TQ, IDX_HEADS * IDX_DIM), lambda b, i: (b * nq + i, qi_col)),
            pl.BlockSpec((s, LANES), lambda b, i: (b, 0)),
            pl.BlockSpec((TQ, LANES), lambda b, i: (b * nq + i, 1)),
        ],
        out_specs=pl.BlockSpec((TQ, s), lambda b, i: (b * nq + i, 0)),
        out_shape=jax.ShapeDtypeStruct((t, s), BF16),
        scratch_shapes=[
            pltpu.VMEM((IDX_HEADS, TQ, LANES), BF16),
            pltpu.VMEM((s, LANES), BF16),
            pltpu.VMEM((LANES, TQ), F32),
            pltpu.VMEM((s, TQ), F32),
            pltpu.VMEM((s, TQ), jnp.int32),
        ],
        compiler_params=_params("parallel", "parallel"),
        name="dsa_select",
    )(z, z_idx, z_idx)
    return pl.pallas_call(
        _dsa_attend_kernel,
        grid=(batch, N_KV_HEADS),
        in_specs=[
            pl.BlockSpec(memory_space=pltpu.SMEM),
            pl.BlockSpec((s, 4 * HEAD_DIM), lambda b, g: (b, g)),
            pl.BlockSpec((s, LANES), lambda b, g: (b, k_col0 + g)),
            pl.BlockSpec((s, LANES), lambda b, g: (b, v_col0 + g)),
            pl.BlockSpec((s, s), lambda b, g: (b, 0)),
            pl.BlockSpec((4, TQ, 2 * TQ), lambda b, g: (g, 0, 0)),
        ],
        out_specs=pl.BlockSpec((s, 4 * HEAD_DIM), lambda b, g: (b, g)),
        out_shape=jax.ShapeDtypeStruct((t, N_HEADS * HEAD_DIM), BF16),
        compiler_params=_params("parallel", "parallel"),
        name="dsa_attend",
    )(table, z, z, z, mask, bias_causal)


def _lower_bound_kernel(x_ref, o_ref):
    x = x_ref[...]
    e = jnp.exp(x - jnp.max(x, axis=0, keepdims=True))
    soft = e / jnp.sum(e, axis=0, keepdims=True)
    run = soft[0:1]
    o_ref[0:1, :] = run - soft[0:1]
    for r in range(1, x.shape[0]):
        run = run + soft[r:r + 1]
        o_ref[r:r + 1, :] = run - soft[0:1]


def hgrn_lower_bounds(logits):
    return pl.pallas_call(
        _lower_bound_kernel,
        out_shape=jax.ShapeDtypeStruct(logits.shape, F32),
        name="hgrn_lower_bounds",
    )(logits)


def _dup_heads(w, n_heads):
    k = w.shape[0]
    w = w.reshape(k, n_heads, 1, HEAD_DIM)
    return jnp.broadcast_to(w, (k, n_heads, 2, HEAD_DIM)).reshape(k, n_heads * 2 * HEAD_DIM)


def kernel(x, rel_bias_table, hgrn_lb_logits, norm_g, ffn_w_up, ffn_conv, ffn_w_down, swa_w_in, swa_w_out,
           swa_sinks, diff_w_in, diff_w_out, diff_lambda, diff_subln_g, hgrn_w_in, hgrn_w_out, hgrn_norm_g,
           dsa_w_in, dsa_w_out):
    batch, seq, d = x.shape
    t = batch * seq
    depth = norm_g.shape[0]
    n_mixers = 4
    bias_causal, bias_window = bias_tiles(rel_bias_table)
    lb_all = hgrn_lower_bounds(hgrn_lb_logits)
    kv = N_KV_HEADS * HEAD_DIM
    w_down = ffn_w_down.astype(BF16)

    def q_scale(n):
        return jnp.concatenate([jnp.full((d,), HEAD_DIM ** -0.5 * LOG2_E, F32), jnp.ones((n - d,), F32)])

    x2 = x.reshape(t, d)
    h = norm_cast(x2, norm_g[0, 0])
    for i in range(depth):
        kind, j = i % n_mixers, i // n_mixers
        if kind == 0:
            wq, wk, wv = jnp.split(swa_w_in[j], [d, d + kv], axis=1)
            w_in = jnp.concatenate([wq, _dup_heads(wk, N_KV_HEADS), _dup_heads(wv, N_KV_HEADS)], axis=1)
            z = matmul(h, w_in, BF16, q_scale(w_in.shape[1]))
            o = swa_attention(z, swa_sinks[j], bias_window, batch)
            w_out = swa_w_out
        elif kind == 1:
            z = matmul(h, diff_w_in[j], BF16, q_scale(diff_w_in.shape[2]))
            o = diff_attention(z, rel_bias_table, diff_lambda[j], diff_subln_g[j], bias_causal, batch, i)
            w_out = diff_w_out
        elif kind == 2:
            z = matmul(h, hgrn_w_in[j], F32)
            o = hgrn_recurrence(z, lb_all[i], hgrn_norm_g[j], batch)
            w_out = hgrn_w_out
        else:
            n_qi = IDX_HEADS * IDX_DIM
            wq, wk, wv, wqi, wki, wwi = jnp.split(
                dsa_w_in[j], [d, d + kv, d + 2 * kv, d + 2 * kv + n_qi, d + 2 * kv + n_qi + IDX_DIM], axis=1)
            w_main = jnp.concatenate([wq, _dup_heads(wk, N_KV_HEADS), _dup_heads(wv, N_KV_HEADS), wqi], axis=1)
            w_idx = jnp.concatenate([wki, wki, wwi, jnp.zeros((d, LANES - IDX_HEADS), F32)], axis=1)
            z = matmul(h, w_main, BF16, q_scale(w_main.shape[1]))
            z_idx = matmul(h, w_idx, F32)
            o = dsa_attention(z, z_idx, rel_bias_table, bias_causal, batch)
            w_out = dsa_w_out
        x2, h = proj_residual(o, w_out.astype(BF16), j, x2, norm_g[i, 1], norm_g[i, 2])
        a = ffn_up(h, ffn_w_up, ffn_conv, i, batch)
        x2, h = proj_residual(a, w_down, i, x2, norm_g[i, 3], norm_g[(i + 1) % depth, 0])
    return x2.reshape(batch, seq, d)
```

```python
import functools
import math

import jax
import jax.numpy as jnp
import numpy as np
from jax import lax
from jax.experimental import pallas as pl
from jax.experimental.pallas import tpu as pltpu

D_MODEL = 2048
HEAD_DIM = 64
N_HEADS = 32
N_KV_HEADS = 8
WINDOW = 128
DIFF_HEADS = 16
HGRN_HEADS = 16
HGRN_HEAD_DIM = 128
HGRN_CHUNK = 64
IDX_HEADS = 16
IDX_DIM = 64
TOPK_MAX = 256
D_FF = 5632
CONV_WIDTH = 3
NUM_BUCKETS = 32
MAX_DISTANCE = 128
RMS_EPS = 1e-6

LANES = 128
TQ = 256
VMEM_LIMIT = 56 * 1024 * 1024
NEG_INF = float("-inf")
LOG2_E = math.log2(math.e)
BF16 = jnp.bfloat16
F32 = jnp.float32


def _params(*sem):
    return pltpu.CompilerParams(dimension_semantics=sem, vmem_limit_bytes=VMEM_LIMIT)


def _dot(a, b):
    return jnp.dot(a, b, preferred_element_type=F32)


def _dot_nt(a, b):
    return lax.dot_general(a, b, (((1,), (1,)), ((), ())), preferred_element_type=F32)


def _dot_tn(a, b):
    return lax.dot_general(a, b, (((0,), (0,)), ((), ())), preferred_element_type=F32)


def _rms(x, g):
    return x * lax.rsqrt(jnp.mean(x * x, axis=-1, keepdims=True) + RMS_EPS) * g


def _silu(x):
    return x * (1.0 / (1.0 + jnp.exp(-x)))


def _norm_cast_kernel(x_ref, g_ref, o_ref):
    o_ref[...] = _rms(x_ref[...], g_ref[...]).astype(o_ref.dtype)


def norm_cast(x, g):
    t, d = x.shape
    tm = 512
    return pl.pallas_call(
        _norm_cast_kernel,
        grid=(t // tm,),
        in_specs=[pl.BlockSpec((tm, d), lambda i: (i, 0)), pl.BlockSpec((1, d), lambda i: (0, 0))],
        out_specs=pl.BlockSpec((tm, d), lambda i: (i, 0)),
        out_shape=jax.ShapeDtypeStruct((t, d), BF16),
        compiler_params=_params("parallel"),
        name="norm_cast",
    )(x, g.reshape(1, d))


def _dup_heads(w):
    lo = lax.broadcasted_iota(jnp.int32, (w.shape[0], LANES), 1) < HEAD_DIM
    parts = []
    for p in range(w.shape[1] // LANES):
        blk = w[:, p * LANES:(p + 1) * LANES]
        swapped = pltpu.roll(blk, HEAD_DIM, 1)
        parts += [jnp.where(lo, blk, swapped), jnp.where(lo, swapped, blk)]
    return jnp.concatenate(parts, axis=1)


def _mm_kernel(x_ref, w_ref, o_ref, wb_ref, *, scale, dup_heads):
    @pl.when(pl.program_id(1) == 0)
    def _():
        w = w_ref[...]
        wb_ref[...] = (_dup_heads(w) if dup_heads else w).astype(BF16)

    y = _dot(x_ref[...], wb_ref[...])
    o_ref[...] = (y if scale == 1.0 else y * scale).astype(o_ref.dtype)


def matmul(x, w, out_dtype, *, col0=0, n_cols=None, scale=1.0, dup_heads=False):
    t, k = x.shape
    n_cols = w.shape[1] - col0 if n_cols is None else n_cols
    tm = 1024
    tn_out = next(c for c in (1024, 512, 256, 128) if (n_cols * (2 if dup_heads else 1)) % c == 0)
    tn_in = tn_out // 2 if dup_heads else tn_out
    assert n_cols % tn_in == 0 and col0 % tn_in == 0, (n_cols, col0, tn_in)
    n_out = n_cols * (2 if dup_heads else 1)
    return pl.pallas_call(
        functools.partial(_mm_kernel, scale=scale, dup_heads=dup_heads),
        grid=(n_cols // tn_in, t // tm),
        in_specs=[pl.BlockSpec((tm, k), lambda j, i: (i, 0)),
                  pl.BlockSpec((k, tn_in), lambda j, i: (0, col0 // tn_in + j))],
        out_specs=pl.BlockSpec((tm, tn_out), lambda j, i: (i, j)),
        out_shape=jax.ShapeDtypeStruct((t, n_out), out_dtype),
        scratch_shapes=[pltpu.VMEM((k, tn_out), BF16)],
        compiler_params=_params("parallel", "arbitrary"),
        name="in_proj",
    )(x, w)


def _proj_res_kernel(a_ref, w_ref, x_ref, go_ref, gn_ref, xo_ref, h_ref):
    xn = x_ref[...] + _rms(_dot(a_ref[...], w_ref[...]), go_ref[...])
    xo_ref[...] = xn
    h_ref[...] = _rms(xn, gn_ref[...]).astype(h_ref.dtype)


def _proj_rows(k, d):
    for tm in (512, 256, 128):
        streamed = 2 * (tm * k * 2 + 2 * tm * d * 4 + tm * d * 2)
        temporaries = 3 * tm * d * 4
        if k * d * 2 + streamed + temporaries <= VMEM_LIMIT - (4 << 20):
            return tm
    raise ValueError("projection weight does not fit in VMEM")


def proj_residual(a, w, layer, x, g_out, g_next):
    t, k = a.shape
    d = w.shape[2]
    tm = _proj_rows(k, d)
    return pl.pallas_call(
        _proj_res_kernel,
        grid=(t // tm,),
        in_specs=[
            pl.BlockSpec((tm, k), lambda i: (i, 0)),
            pl.BlockSpec((None, k, d), lambda i: (layer, 0, 0), pipeline_mode=pl.Buffered(1)),
            pl.BlockSpec((tm, d), lambda i: (i, 0)),
            pl.BlockSpec((1, d), lambda i: (0, 0)),
            pl.BlockSpec((1, d), lambda i: (0, 0)),
        ],
        out_specs=[pl.BlockSpec((tm, d), lambda i: (i, 0)), pl.BlockSpec((tm, d), lambda i: (i, 0))],
        out_shape=[jax.ShapeDtypeStruct((t, d), F32), jax.ShapeDtypeStruct((t, d), BF16)],
        compiler_params=_params("parallel"),
        name="proj_residual",
    )(a, w, x, g_out.reshape(1, d), g_next.reshape(1, d))


FFN_ROWS = 512
FFN_COLS = 512


def _ffn_up_kernel(h_ref, wg_ref, wv_ref, cg_ref, cv_ref, wd_ref, o_ref, wdb_ref):
    wdb_ref[...] = wd_ref[...].astype(BF16)
    s = h_ref.shape[0]
    rows = min(FFN_ROWS, s)
    tn = o_ref.shape[1]
    row = lax.broadcasted_iota(jnp.int32, (rows, 1), 0)
    wg, wv = wg_ref[...].astype(BF16), wv_ref[...].astype(BF16)
    cg, cv = cg_ref[...], cv_ref[...]

    def conv(u, tail, c):
        u1 = jnp.where(row >= 1, pltpu.roll(u, 1, 0), tail[1:2])
        u2 = jnp.where(row >= 2, pltpu.roll(u, 2, 0), jnp.where(row == 1, tail[1:2], tail[0:1]))
        return (c[0:1] * u2 + c[1:2] * u1) + c[2:3] * u

    tail_g = tail_v = jnp.zeros((CONV_WIDTH - 1, tn), F32)
    for r in range(s // rows):
        hr = h_ref[r * rows:(r + 1) * rows, :]
        ug, uv = _dot(hr, wg), _dot(hr, wv)
        out = _silu(conv(ug, tail_g, cg)) * conv(uv, tail_v, cv)
        o_ref[r * rows:(r + 1) * rows, :] = out.astype(o_ref.dtype)
        tail_g, tail_v = ug[rows - 2:rows], uv[rows - 2:rows]


def ffn_up(h, w_up, conv_w, w_down, layer, batch):
    t, d = h.shape
    s = t // batch
    tn = FFN_COLS
    nj = D_FF // tn
    return pl.pallas_call(
        _ffn_up_kernel,
        grid=(nj, batch),
        in_specs=[
            pl.BlockSpec((s, d), lambda j, b: (b, 0)),
            pl.BlockSpec((None, d, tn), lambda j, b: (layer, 0, j)),
            pl.BlockSpec((None, d, tn), lambda j, b: (layer, 0, nj + j)),
            pl.BlockSpec((None, CONV_WIDTH, tn), lambda j, b: (layer, 0, j)),
            pl.BlockSpec((None, CONV_WIDTH, tn), lambda j, b: (layer, 0, nj + j)),
            pl.BlockSpec((None, tn, d), lambda j, b: (layer, j, 0)),
        ],
        out_specs=[pl.BlockSpec((s, tn), lambda j, b: (b, j)), pl.BlockSpec((None, tn, d), lambda j, b: (0, j, 0))],
        out_shape=[jax.ShapeDtypeStruct((t, D_FF), BF16), jax.ShapeDtypeStruct((1, D_FF, d), BF16)],
        compiler_params=_params("arbitrary", "arbitrary"),
        name="ffn_up",
    )(h, w_up, w_up, conv_w, conv_w, w_down)


def _bucket_thresholds():
    max_exact = NUM_BUCKETS // 2
    n = np.arange(1, 2 * MAX_DISTANCE, dtype=np.float64)
    large = max_exact + np.floor(np.log(n / max_exact) / math.log(MAX_DISTANCE / max_exact) * (NUM_BUCKETS - max_exact))
    bucket = np.where(n < max_exact, n, np.minimum(large, NUM_BUCKETS - 1)).astype(np.int64)
    return tuple(int(n[bucket >= b][0]) for b in range(max_exact + 1, NUM_BUCKETS))


BUCKET_THRESHOLDS = _bucket_thresholds()


def _bias_tile_kernel(tab_ref, causal_ref, window_ref):
    m = pl.program_id(0)
    r = lax.broadcasted_iota(jnp.int32, (TQ, 2 * TQ), 0)
    c = lax.broadcasted_iota(jnp.int32, (TQ, 2 * TQ), 1)
    d = r - c + TQ
    n = jnp.maximum(d, 0)
    max_exact = NUM_BUCKETS // 2
    large = jnp.full_like(n, max_exact)
    for thr in BUCKET_THRESHOLDS:
        large = large + (n >= thr).astype(jnp.int32)
    bucket = jnp.where(n < max_exact, n, large)
    val = jnp.zeros((TQ, 2 * TQ), F32)
    for b in range(NUM_BUCKETS):
        val = jnp.where(bucket == b, tab_ref[b, m] * LOG2_E, val)
    causal = jnp.where(d >= 0, val, NEG_INF)
    causal_ref[0] = causal
    window_ref[0] = jnp.where(d < WINDOW, causal, NEG_INF)


def bias_tiles(table):
    shape = jax.ShapeDtypeStruct((N_HEADS, TQ, 2 * TQ), F32)
    return pl.pallas_call(
        _bias_tile_kernel,
        grid=(N_HEADS,),
        in_specs=[pl.BlockSpec(memory_space=pltpu.SMEM)],
        out_specs=[pl.BlockSpec((1, TQ, 2 * TQ), lambda m: (m, 0, 0))] * 2,
        out_shape=[shape, shape],
        compiler_params=_params("parallel"),
        name="bias_tiles",
    )(table)


def _head_select(x, lo, first):
    zero = jnp.zeros_like(x)
    return jnp.where(lo, x, zero) if first else jnp.where(lo, zero, x)


def _stack_heads(q):
    rows = q.shape[0]
    lo = lax.broadcasted_iota(jnp.int32, (rows, LANES), 1) < HEAD_DIM
    parts = []
    for p in range(q.shape[1] // LANES):
        qp = q[:, p * LANES:(p + 1) * LANES]
        parts += [_head_select(qp, lo, True), _head_select(qp, lo, False)]
    return jnp.concatenate(parts, axis=0)


def _pair_selector(n_keys):
    r = lax.broadcasted_iota(jnp.int32, (2 * n_keys, LANES), 0)
    c = lax.broadcasted_iota(jnp.int32, (2 * n_keys, LANES), 1)
    return jnp.where((r < n_keys) == (c < HEAD_DIM), 1.0, 0.0).astype(BF16)


def _paired_values(vv):
    lo = lax.broadcasted_iota(jnp.int32, vv.shape, 1) < HEAD_DIM
    v_both = jnp.concatenate([_head_select(vv, lo, True), _head_select(vv, lo, False)], axis=0)
    return jnp.concatenate([v_both, _pair_selector(vv.shape[0])], axis=1)


SWA_BLOCK = WINDOW


def _swa_kernel(sink_ref, q_ref, kp_ref, ko_ref, vp_ref, vo_ref, bias_ref, o_ref):
    i = pl.program_id(1)
    blk = SWA_BLOCK
    col = lax.broadcasted_iota(jnp.int32, (1, 2 * blk), 1)
    no_prev = (col < blk) & (i == 0)
    lo = lax.broadcasted_iota(jnp.int32, (blk, LANES), 1) < HEAD_DIM

    def scores(g):
        stack = _stack_heads(q_ref[:, g * 4 * HEAD_DIM:(g + 1) * 4 * HEAD_DIM])
        cols = slice(g * LANES, (g + 1) * LANES)
        kk = jnp.concatenate([kp_ref[:, cols], ko_ref[:, cols]], axis=0)
        s = _dot_nt(stack, kk) + bias_ref[4 * g:4 * g + 4].reshape(4 * blk, 2 * blk)
        return jnp.where(no_prev, NEG_INF, s)

    def softmax(g, s):
        sink = jnp.concatenate([jnp.full((blk, 1), sink_ref[4 * g + hh] * LOG2_E, F32) for hh in range(4)], axis=0)
        m = jnp.maximum(jnp.max(s, axis=-1, keepdims=True), sink)
        return jnp.exp2(s - m).astype(BF16), jnp.exp2(sink - m)

    def finish(g, e, e_sink):
        cols = slice(g * LANES, (g + 1) * LANES)
        v_ext = _paired_values(jnp.concatenate([vp_ref[:, cols], vo_ref[:, cols]], axis=0))
        outs = []
        for p in range(2):
            r0 = 2 * p * blk
            e_both = jnp.concatenate([e[r0:r0 + blk], e[r0 + blk:r0 + 2 * blk]], axis=1)
            res = _dot(e_both, v_ext)
            den = res[:, LANES:] + jnp.where(lo, e_sink[r0:r0 + blk], e_sink[r0 + blk:r0 + 2 * blk])
            outs.append(res[:, :LANES] / den)
        o_ref[:, g * 4 * HEAD_DIM:(g + 1) * 4 * HEAD_DIM] = jnp.concatenate(outs, axis=1).astype(o_ref.dtype)

    s_next = scores(0)
    prev = None
    for g in range(N_KV_HEADS):
        s = s_next
        if g + 1 < N_KV_HEADS:
            s_next = scores(g + 1)
        cur = softmax(g, s)
        if prev is not None:
            finish(g - 1, *prev)
        prev = cur
    finish(N_KV_HEADS - 1, *prev)


def swa_attention(q, kv, sinks, bias_window, batch):
    t, dq = q.shape
    blk = SWA_BLOCK
    nq = t // batch // blk
    dk = 2 * N_KV_HEADS * HEAD_DIM

    def own(col):
        return lambda b, i: (b * nq + i, col)

    def prev(col):
        return lambda b, i: (b * nq + jnp.maximum(i - 1, 0), col)

    bias_spec = pl.BlockSpec((N_HEADS, blk, 2 * blk), lambda b, i: (0, TQ // blk - 1, TQ // blk - 1))
    return pl.pallas_call(
        _swa_kernel,
        grid=(batch, nq),
        in_specs=[
            pl.BlockSpec(memory_space=pltpu.SMEM),
            pl.BlockSpec((blk, dq), own(0)),
            pl.BlockSpec((blk, dk), prev(0)),
            pl.BlockSpec((blk, dk), own(0)),
            pl.BlockSpec((blk, dk), prev(1)),
            pl.BlockSpec((blk, dk), own(1)),
            bias_spec,
        ],
        out_specs=pl.BlockSpec((blk, dq), own(0)),
        out_shape=jax.ShapeDtypeStruct((t, dq), BF16),
        compiler_params=_params("parallel", "parallel"),
        name="swa_attention",
    )(sinks, q, kv, kv, kv, kv, bias_window)


def _causal_scores(stack, k_ref, kv_len, bias_ref, far_bias):
    rows = stack.shape[0]
    s = _dot_nt(stack, k_ref[0:kv_len, :])
    if kv_len == TQ:
        return s + bias_ref[:, :, TQ:2 * TQ].reshape(rows, TQ)
    near = bias_ref[...].reshape(rows, 2 * TQ)
    if kv_len == 2 * TQ:
        return s + near
    n_far = kv_len - 2 * TQ
    return jnp.concatenate([s[:, :n_far] + far_bias, s[:, n_far:] + near], axis=1)


def _softmax_pipeline(n_blocks, scores, finish):
    order = list(range(n_blocks - 1, -1, -1))
    s_next = scores(order[0])
    p_prev = None
    for n in range(n_blocks):
        s = s_next
        if n + 1 < n_blocks:
            s_next = scores(order[n + 1])
        p = jnp.exp2(s - jnp.max(s, axis=-1, keepdims=True)).astype(BF16)
        if p_prev is not None:
            finish(order[n - 1], p_prev)
        p_prev = p
    finish(order[-1], p_prev)


def _diff_kernel(tab_ref, lam_ref, g_ref, q_ref, k_ref, v_ref, bias_ref, o_ref, *, lam_init):
    h = pl.program_id(0)
    last = NUM_BUCKETS - 1
    far_bias = jnp.concatenate([jnp.full((TQ, 1), tab_ref[last, 2 * h] * LOG2_E, F32),
                                jnp.full((TQ, 1), tab_ref[last, 2 * h + 1] * LOG2_E, F32)], axis=0)
    lf = lam_ref[...]
    lam = (jnp.exp(jnp.sum(lf[0:1] * lf[1:2], axis=-1, keepdims=True))
           - jnp.exp(jnp.sum(lf[2:3] * lf[3:4], axis=-1, keepdims=True)) + lam_init)
    ones = jnp.ones((q_ref.shape[0], LANES), BF16)

    def scores(c):
        stack = _stack_heads(q_ref[c * TQ:(c + 1) * TQ, :])
        return _causal_scores(stack, k_ref, (c + 1) * TQ, bias_ref, far_bias)

    def finish(c, p):
        kv_len = (c + 1) * TQ
        v_ext = jnp.concatenate([v_ref[0:kv_len, :], ones[0:kv_len]], axis=1)
        res = _dot(p, v_ext)
        o12 = res[:, :LANES] / res[:, LANES:]
        o = o12[:TQ] - lam * o12[TQ:]
        o_ref[c * TQ:(c + 1) * TQ, :] = (_rms(o, g_ref[...]) * (1.0 - lam_init)).astype(o_ref.dtype)

    _softmax_pipeline(q_ref.shape[0] // TQ, scores, finish)


def diff_attention(q, kv, table, lambdas, subln_g, bias_causal, batch, layer_idx):
    t = q.shape[0]
    s = t // batch
    lam_init = 0.8 - 0.6 * math.exp(-0.3 * layer_idx)
    return pl.pallas_call(
        functools.partial(_diff_kernel, lam_init=lam_init),
        grid=(DIFF_HEADS, batch),
        in_specs=[
            pl.BlockSpec(memory_space=pltpu.SMEM),
            pl.BlockSpec((4, HEAD_DIM), lambda h, b: (0, 0)),
            pl.BlockSpec((1, LANES), lambda h, b: (0, 0)),
            pl.BlockSpec((s, LANES), lambda h, b: (b, h)),
            pl.BlockSpec((s, LANES), lambda h, b: (b, h)),
            pl.BlockSpec((s, LANES), lambda h, b: (b, DIFF_HEADS + h)),
            pl.BlockSpec((2, TQ, 2 * TQ), lambda h, b: (h, 0, 0)),
        ],
        out_specs=pl.BlockSpec((s, LANES), lambda h, b: (b, h)),
        out_shape=jax.ShapeDtypeStruct((t, DIFF_HEADS * 2 * HEAD_DIM), BF16),
        compiler_params=_params("parallel", "parallel"),
        name="diff_attention",
    )(table, lambdas, subln_g.reshape(1, LANES), q, kv, kv, bias_causal)


HGRN_LEVELS = tuple(2 ** e for e in range(int(math.log2(HGRN_CHUNK)) - 1, -1, -1))
HGRN_GROUP = 256


def _hgrn_kernel(lb_ref, g_ref, q_ref, f_ref, i_ref, og_ref, o_ref,
                 qf_ref, kf_ref, bq_ref, kd_ref, dec_ref, v_ref):
    s_len, hd = q_ref.shape
    c_len = HGRN_CHUNK
    n_lvl = len(HGRN_LEVELS)
    grp = min(HGRN_GROUP, s_len)
    rc = lax.broadcasted_iota(jnp.int32, (grp, hd), 0) & (c_len - 1)
    lb = lb_ref[...]
    log_lb, log_1m_lb = jnp.log(lb), jnp.log1p(-lb)

    def log1p01(u):
        w = 1.0 + u
        return jnp.where(w == 1.0, u, jnp.log(w) * (u / (w - 1.0)))

    def prepare(gi):
        rows = slice(gi * grp, (gi + 1) * grp)
        z = f_ref[rows, :]
        log_sig = jnp.minimum(z, 0.0) - log1p01(jnp.exp(-jnp.abs(z)))
        c = log_1m_lb + log_sig
        delta = log_lb - c
        log_f = jnp.where(jnp.isnan(delta), log_lb + c,
                          jnp.maximum(log_lb, c) + log1p01(jnp.exp(-jnp.abs(delta))))
        key = (1.0 - lb) * (1.0 / (1.0 + jnp.exp(z)))
        qs = _silu(q_ref[rows, :])
        v_ref[rows, :] = i_ref[rows, :].astype(BF16)

        b = log_f * LOG2_E
        for d in (1, 2, 4, 8, 16, 32):
            b = b + jnp.where(rc >= d, pltpu.roll(b, d, 0), 0.0)

        bq_ref[rows, :] = (qs * jnp.exp2(b)).astype(BF16)
        qf_ref[n_lvl, rows, :] = qs.astype(BF16)
        kf_ref[n_lvl, rows, :] = key.astype(BF16)

        r_m = jnp.where(rc >= 1, pltpu.roll(b, 1, 0), 0.0)
        for m in (1, 2, 4, 8, 16, 32):
            if m > 1:
                half = m // 2
                r_m = jnp.where((rc & half) != 0, pltpu.roll(r_m, half, 0), r_m)
            e_m = pltpu.roll(r_m, grp - m, 0)
            lvl = HGRN_LEVELS.index(m)
            qf_ref[lvl, rows, :] = (qs * jnp.exp2(b - r_m)).astype(BF16)
            kf_ref[lvl, rows, :] = (key * jnp.exp2(jnp.minimum(e_m - b, 0.0))).astype(BF16)
        b3 = b.reshape(grp // c_len, c_len, hd)
        b_end = jnp.broadcast_to(b3[:, c_len - 1:c_len, :], b3.shape).reshape(grp, hd)
        kd_ref[rows, :] = (key * jnp.exp2(b_end - b)).astype(BF16)
        dec_ref[rows, :] = jnp.exp2(b_end)

    ti = lax.broadcasted_iota(jnp.int32, (c_len, c_len), 0)
    si = lax.broadcasted_iota(jnp.int32, (c_len, c_len), 1)
    masks = [((ti // m) % 2 == 1) & (si // m == ti // m - 1) for m in HGRN_LEVELS] + [ti == si]
    g = g_ref[...]

    def chunk(ci, st):
        rows = slice(ci * c_len, (ci + 1) * c_len)
        att = jnp.zeros((c_len, c_len), F32)
        for lvl in range(n_lvl + 1):
            att = jnp.where(masks[lvl], _dot_nt(qf_ref[lvl, rows, :], kf_ref[lvl, rows, :]), att)
        vc = v_ref[rows, :]
        o = _dot(att.astype(BF16), vc) + _dot_nt(bq_ref[rows, :], st.astype(BF16))
        o_ref[rows, :] = (_rms(o, g) * _silu(og_ref[rows, :])).astype(o_ref.dtype)
        return st * dec_ref[ci * c_len:ci * c_len + 1, :] + _dot_tn(vc, kd_ref[rows, :])

    n_groups = s_len // grp
    per_group = grp // c_len
    st = jnp.zeros((hd, hd), F32)
    prepare(0)
    for gi in range(n_groups):
        if gi + 1 < n_groups:
            prepare(gi + 1)
        for ci in range(gi * per_group, (gi + 1) * per_group):
            st = chunk(ci, st)


def hgrn_recurrence(z, lb, norm_g, batch):
    t = z.shape[0]
    s = t // batch
    hd = HGRN_HEAD_DIM
    n_lvl = len(HGRN_LEVELS)

    def part(p):
        return pl.BlockSpec((s, hd), lambda b, h: (b, p * HGRN_HEADS + h))

    return pl.pallas_call(
        _hgrn_kernel,
        grid=(batch, HGRN_HEADS),
        in_specs=[pl.BlockSpec((1, hd), lambda b, h: (0, h)), pl.BlockSpec((1, hd), lambda b, h: (0, 0)),
                  part(0), part(1), part(2), part(3)],
        out_specs=pl.BlockSpec((s, hd), lambda b, h: (b, h)),
        out_shape=jax.ShapeDtypeStruct((t, HGRN_HEADS * hd), BF16),
        scratch_shapes=[
            pltpu.VMEM((n_lvl + 1, s, hd), BF16),
            pltpu.VMEM((n_lvl + 1, s, hd), BF16),
            pltpu.VMEM((s, hd), BF16),
            pltpu.VMEM((s, hd), BF16),
            pltpu.VMEM((s, hd), F32),
            pltpu.VMEM((s, hd), BF16),
        ],
        compiler_params=_params("parallel", "parallel"),
        name="hgrn_recurrence",
    )(lb.reshape(1, -1), norm_g.reshape(1, hd), z, z, z, z)


def _sortable(score):
    score = jnp.where(score == 0.0, 0.0, score)
    bits = lax.bitcast_convert_type(score, jnp.int32)
    return bits ^ ((bits >> 31) & jnp.int32(0x7FFFFFFF))


def _dsa_select(c, qi_ref, ki_ref, wi_ref, qsel_ref, kib_ref, wt_ref, sc_ref, keys_ref, mask_ref, n_sel):
    kv_len = (c + 1) * TQ
    lo = lax.broadcasted_iota(jnp.int32, (TQ, LANES), 1) < HEAD_DIM
    for p in range(IDX_HEADS // 2):
        qp = qi_ref[:, p * LANES:(p + 1) * LANES]
        qsel_ref[2 * p] = _head_select(qp, lo, True)
        qsel_ref[2 * p + 1] = _head_select(qp, lo, False)
    wt_ref[...] = wi_ref[...].T * (IDX_HEADS ** -0.5 * IDX_DIM ** -0.5)
    ki = ki_ref[0:kv_len, :]
    lo_k = lax.broadcasted_iota(jnp.int32, (kv_len, LANES), 1) < IDX_DIM
    kib_ref[0:kv_len, :] = jnp.where(lo_k, ki, pltpu.roll(ki, IDX_DIM, 1)).astype(BF16)
    sc_ref[0:kv_len, :] = jnp.zeros((kv_len, TQ), F32)

    def head_pair(hp, carry):
        kib = kib_ref[0:kv_len, :]
        acc = sc_ref[0:kv_len, :]
        for hh in (2 * hp, 2 * hp + 1):
            logit = _dot_nt(kib, qsel_ref[hh])
            acc = acc + jnp.maximum(logit, 0.0) * wt_ref[pl.ds(IDX_DIM + hh, 1), :]
        sc_ref[0:kv_len, :] = acc
        return carry

    lax.fori_loop(0, IDX_HEADS // 2, head_pair, 0)

    key_i = lax.broadcasted_iota(jnp.int32, (TQ, TQ), 0)
    qry_i = lax.broadcasted_iota(jnp.int32, (TQ, TQ), 1)
    if c > 0:
        keys_ref[0:c * TQ, :] = _sortable(sc_ref[0:c * TQ, :])
    keys_ref[c * TQ:kv_len, :] = _sortable(jnp.where(key_i <= qry_i, sc_ref[c * TQ:kv_len, :], NEG_INF))

    def count(pred):
        return jnp.sum(pred(keys_ref[0:kv_len, :]).astype(jnp.int32), axis=0, keepdims=True)

    def search(it, thr):
        trial = thr ^ (jnp.int32(1) << (31 - it))
        return jnp.where(count(lambda kj: kj >= trial) >= n_sel, trial, thr)

    thr = lax.fori_loop(0, 32, search, jnp.full((1, TQ), jnp.iinfo(jnp.int32).min, jnp.int32))
    need = (n_sel - count(lambda kj: kj > thr)).astype(F32)
    earlier = (qry_i < key_i).astype(BF16)
    base = jnp.zeros((1, TQ), F32)
    for j in range(c + 1):
        kj = keys_ref[j * TQ:(j + 1) * TQ, :]
        eq = kj == thr
        eq_f = jnp.where(eq, 1.0, 0.0)
        rank = base + _dot(earlier, eq_f.astype(BF16))
        sel = (kj > thr) | (eq & (rank < need))
        mask_ref[:, j * TQ:(j + 1) * TQ] = jnp.where(sel, 0.0, NEG_INF).T.astype(mask_ref.dtype)
        base = base + jnp.sum(eq_f, axis=0, keepdims=True)
    s_len = mask_ref.shape[1]
    if kv_len < s_len:
        mask_ref[:, kv_len:] = jnp.full((TQ, s_len - kv_len), NEG_INF, mask_ref.dtype)


def _dsa_select_kernel(qi_ref, ki_ref, wi_ref, mask_ref, qsel_ref, kib_ref, wt_ref, sc_ref, keys_ref, *, n_sel, nq):
    i = pl.program_id(1)
    for c in range(nq):
        pl.when(i == c)(functools.partial(
            _dsa_select, c, qi_ref, ki_ref, wi_ref, qsel_ref, kib_ref, wt_ref, sc_ref, keys_ref, mask_ref, n_sel))


def _dsa_attend_kernel(tab_ref, q_ref, k_ref, v_ref, mask_ref, bias_ref, o_ref):
    g = pl.program_id(1)
    last = NUM_BUCKETS - 1
    far_bias = jnp.concatenate(
        [jnp.full((TQ, 1), tab_ref[last, 4 * g + hh] * LOG2_E, F32) for hh in range(4)], axis=0)

    def scores(c):
        rows = slice(c * TQ, (c + 1) * TQ)
        kv_len = (c + 1) * TQ
        s = _causal_scores(_stack_heads(q_ref[rows, :]), k_ref, kv_len, bias_ref, far_bias)
        return s + jnp.concatenate([mask_ref[rows, 0:kv_len].astype(F32)] * 4, axis=0)

    def finish(c, p):
        v_ext = _paired_values(v_ref[0:(c + 1) * TQ, :])
        outs = []
        for pr in range(2):
            r0 = 2 * pr * TQ
            p_both = jnp.concatenate([p[r0:r0 + TQ], p[r0 + TQ:r0 + 2 * TQ]], axis=1)
            res = _dot(p_both, v_ext)
            outs.append(res[:, :LANES] / res[:, LANES:])
        o_ref[c * TQ:(c + 1) * TQ, :] = jnp.concatenate(outs, axis=1).astype(o_ref.dtype)

    _softmax_pipeline(q_ref.shape[0] // TQ, scores, finish)


def dsa_attention(q, kv, qi, z_idx, table, bias_causal, batch):
    t = q.shape[0]
    s = t // batch
    nq = s // TQ
    n_sel = min(TOPK_MAX, s // 4)
    mask = pl.pallas_call(
        functools.partial(_dsa_select_kernel, n_sel=n_sel, nq=nq),
        grid=(batch, nq),
        in_specs=[
            pl.BlockSpec((TQ, IDX_HEADS * IDX_DIM), lambda b, i: (b * nq + i, 0)),
            pl.BlockSpec((s, LANES), lambda b, i: (b, 0)),
            pl.BlockSpec((TQ, LANES), lambda b, i: (b * nq + i, 0)),
        ],
        out_specs=pl.BlockSpec((TQ, s), lambda b, i: (b * nq + i, 0)),
        out_shape=jax.ShapeDtypeStruct((t, s), BF16),
        scratch_shapes=[
            pltpu.VMEM((IDX_HEADS, TQ, LANES), BF16),
            pltpu.VMEM((s, LANES), BF16),
            pltpu.VMEM((LANES, TQ), F32),
            pltpu.VMEM((s, TQ), F32),
            pltpu.VMEM((s, TQ), jnp.int32),
        ],
        compiler_params=_params("parallel", "parallel"),
        name="dsa_select",
    )(qi, z_idx, z_idx)
    return pl.pallas_call(
        _dsa_attend_kernel,
        grid=(batch, N_KV_HEADS),
        in_specs=[
            pl.BlockSpec(memory_space=pltpu.SMEM),
            pl.BlockSpec((s, 4 * HEAD_DIM), lambda b, g: (b, g)),
            pl.BlockSpec((s, LANES), lambda b, g: (b, g)),
            pl.BlockSpec((s, LANES), lambda b, g: (b, N_KV_HEADS + g)),
            pl.BlockSpec((s, s), lambda b, g: (b, 0)),
            pl.BlockSpec((4, TQ, 2 * TQ), lambda b, g: (g, 0, 0)),
        ],
        out_specs=pl.BlockSpec((s, 4 * HEAD_DIM), lambda b, g: (b, g)),
        out_shape=jax.ShapeDtypeStruct((t, N_HEADS * HEAD_DIM), BF16),
        compiler_params=_params("parallel", "parallel"),
        name="dsa_attend",
    )(table, q, kv, kv, mask, bias_causal)


def _lower_bound_kernel(x_ref, o_ref):
    x = x_ref[...]
    e = jnp.exp(x - jnp.max(x, axis=0, keepdims=True))
    soft = e / jnp.sum(e, axis=0, keepdims=True)
    run = soft[0:1]
    o_ref[0:1, :] = run - soft[0:1]
    for r in range(1, x.shape[0]):
        run = run + soft[r:r + 1]
        o_ref[r:r + 1, :] = run - soft[0:1]


def hgrn_lower_bounds(logits):
    return pl.pallas_call(
        _lower_bound_kernel,
        out_shape=jax.ShapeDtypeStruct(logits.shape, F32),
        name="hgrn_lower_bounds",
    )(logits)


def kernel(x, rel_bias_table, hgrn_lb_logits, norm_g, ffn_w_up, ffn_conv, ffn_w_down, swa_w_in, swa_w_out,
           swa_sinks, diff_w_in, diff_w_out, diff_lambda, diff_subln_g, hgrn_w_in, hgrn_w_out, hgrn_norm_g,
           dsa_w_in, dsa_w_out):
    batch, seq, d = x.shape
    t = batch * seq
    depth = norm_g.shape[0]
    n_mixers = 4
    bias_causal, bias_window = bias_tiles(rel_bias_table)
    lb_all = hgrn_lower_bounds(hgrn_lb_logits)
    kv = N_KV_HEADS * HEAD_DIM
    q_scale = HEAD_DIM ** -0.5 * LOG2_E

    x2 = x.reshape(t, d)
    h = norm_cast(x2, norm_g[0, 0])
    for i in range(depth):
        kind, j = i % n_mixers, i // n_mixers
        if kind == 0:
            w = swa_w_in[j]
            q = matmul(h, w, BF16, n_cols=d, scale=q_scale)
            kvd = matmul(h, w, BF16, col0=d, n_cols=2 * kv, dup_heads=True)
            o = swa_attention(q, kvd, swa_sinks[j], bias_window, batch)
            w_out = swa_w_out
        elif kind == 1:
            w = diff_w_in[j]
            q = matmul(h, w, BF16, n_cols=d, scale=q_scale)
            kvp = matmul(h, w, BF16, col0=d)
            o = diff_attention(q, kvp, rel_bias_table, diff_lambda[j], diff_subln_g[j], bias_causal, batch, i)
            w_out = diff_w_out
        elif kind == 2:
            z = matmul(h, hgrn_w_in[j], F32)
            o = hgrn_recurrence(z, lb_all[i], hgrn_norm_g[j], batch)
            w_out = hgrn_w_out
        else:
            w = dsa_w_in[j]
            n_qi = IDX_HEADS * IDX_DIM
            q = matmul(h, w, BF16, n_cols=d, scale=q_scale)
            kvd = matmul(h, w, BF16, col0=d, n_cols=2 * kv, dup_heads=True)
            qi = matmul(h, w, BF16, col0=d + 2 * kv, n_cols=n_qi)
            w_idx = jnp.pad(w[:, d + 2 * kv + n_qi:], ((0, 0), (0, LANES - IDX_DIM - IDX_HEADS)))
            z_idx = matmul(h, w_idx, F32)
            o = dsa_attention(q, kvd, qi, z_idx, rel_bias_table, bias_causal, batch)
            w_out = dsa_w_out
        x2, h = proj_residual(o, w_out.astype(BF16), j, x2, norm_g[i, 1], norm_g[i, 2])
        a, w_down = ffn_up(h, ffn_w_up, ffn_conv, ffn_w_down, i, batch)
        x2, h = proj_residual(a, w_down, 0, x2, norm_g[i, 3], norm_g[(i + 1) % depth, 0])
    return x2.reshape(batch, seq, d)
```

```python
import functools
import math

import jax
import jax.numpy as jnp
import numpy as np
from jax import lax
from jax.experimental import pallas as pl
from jax.experimental.pallas import tpu as pltpu

D_MODEL = 2048
HEAD_DIM = 64
N_HEADS = 32
N_KV_HEADS = 8
WINDOW = 128
DIFF_HEADS = 16
HGRN_HEADS = 16
HGRN_HEAD_DIM = 128
HGRN_CHUNK = 64
IDX_HEADS = 16
IDX_DIM = 64
TOPK_MAX = 256
D_FF = 5632
CONV_WIDTH = 3
NUM_BUCKETS = 32
MAX_DISTANCE = 128
RMS_EPS = 1e-6

LANES = 128
TQ = 256
VMEM_LIMIT = 56 * 1024 * 1024
NEG_INF = float("-inf")
LOG2_E = math.log2(math.e)
BF16 = jnp.bfloat16
F32 = jnp.float32


def _params(*sem):
    return pltpu.CompilerParams(dimension_semantics=sem, vmem_limit_bytes=VMEM_LIMIT)


def _dot(a, b):
    return jnp.dot(a, b, preferred_element_type=F32)


def _dot_nt(a, b):
    return lax.dot_general(a, b, (((1,), (1,)), ((), ())), preferred_element_type=F32)


def _dot_tn(a, b):
    return lax.dot_general(a, b, (((0,), (0,)), ((), ())), preferred_element_type=F32)


def _rms(x, g):
    return x * lax.rsqrt(jnp.mean(x * x, axis=-1, keepdims=True) + RMS_EPS) * g


def _silu(x):
    return x * (1.0 / (1.0 + jnp.exp(-x)))


def _dup_heads(w):
    lo = lax.broadcasted_iota(jnp.int32, (w.shape[0], LANES), 1) < HEAD_DIM
    parts = []
    for p in range(w.shape[1] // LANES):
        blk = w[:, p * LANES:(p + 1) * LANES]
        swapped = pltpu.roll(blk, HEAD_DIM, 1)
        parts += [jnp.where(lo, blk, swapped), jnp.where(lo, swapped, blk)]
    return jnp.concatenate(parts, axis=1)


def _mm_kernel(x_ref, g_ref, w_ref, o_ref, wb_ref, *, scale, dup_heads, norm):
    @pl.when(pl.program_id(1) == 0)
    def _():
        w = w_ref[...]
        wb_ref[...] = (_dup_heads(w) if dup_heads else w).astype(BF16)

    x = _rms(x_ref[...], g_ref[...]).astype(BF16) if norm else x_ref[...]
    y = _dot(x, wb_ref[...])
    o_ref[...] = (y if scale == 1.0 else y * scale).astype(o_ref.dtype)


def matmul(x, w, out_dtype, *, col0=0, n_cols=None, scale=1.0, dup_heads=False, norm_g=None):
    t, k = x.shape
    n_cols = w.shape[1] - col0 if n_cols is None else n_cols
    tm = 1024
    tn_out = next(c for c in (1024, 512, 256, 128) if (n_cols * (2 if dup_heads else 1)) % c == 0)
    tn_in = tn_out // 2 if dup_heads else tn_out
    assert n_cols % tn_in == 0 and col0 % tn_in == 0, (n_cols, col0, tn_in)
    n_out = n_cols * (2 if dup_heads else 1)
    norm = norm_g is not None
    g = norm_g.reshape(1, k) if norm else jnp.ones((1, k), F32)
    return pl.pallas_call(
        functools.partial(_mm_kernel, scale=scale, dup_heads=dup_heads, norm=norm),
        grid=(n_cols // tn_in, t // tm),
        in_specs=[pl.BlockSpec((tm, k), lambda j, i: (i, 0)),
                  pl.BlockSpec((1, k), lambda j, i: (0, 0)),
                  pl.BlockSpec((k, tn_in), lambda j, i: (0, col0 // tn_in + j))],
        out_specs=pl.BlockSpec((tm, tn_out), lambda j, i: (i, j)),
        out_shape=jax.ShapeDtypeStruct((t, n_out), out_dtype),
        scratch_shapes=[pltpu.VMEM((k, tn_out), BF16)],
        compiler_params=_params("parallel", "arbitrary"),
        name="in_proj",
    )(x, g, w)


PROJ_SUB_ROWS = 128


def _proj_res_kernel(a_ref, w_ref, x_ref, go_ref, gn_ref, xo_ref, h_ref):
    tm = a_ref.shape[0]
    sub = PROJ_SUB_ROWS if tm >= 4 * PROJ_SUB_ROWS else tm
    ys = [_dot(a_ref[r:r + sub, :], w_ref[...]) for r in range(0, tm, sub)]
    for n, r in enumerate(range(0, tm, sub)):
        xn = x_ref[r:r + sub, :] + _rms(ys[n], go_ref[...])
        xo_ref[r:r + sub, :] = xn
        h_ref[r:r + sub, :] = _rms(xn, gn_ref[...]).astype(h_ref.dtype)


def _proj_rows(k, d):
    for tm in (512, 256, 128):
        streamed = 2 * (tm * k * 2 + 2 * tm * d * 4 + tm * d * 2)
        temporaries = 3 * tm * d * 4
        if k * d * 2 + streamed + temporaries <= VMEM_LIMIT - (4 << 20):
            return tm
    raise ValueError("projection weight does not fit in VMEM")


def proj_residual(a, w, layer, x, g_out, g_next):
    t, k = a.shape
    d = w.shape[2]
    tm = _proj_rows(k, d)
    return pl.pallas_call(
        _proj_res_kernel,
        grid=(t // tm,),
        in_specs=[
            pl.BlockSpec((tm, k), lambda i: (i, 0)),
            pl.BlockSpec((None, k, d), lambda i: (layer, 0, 0), pipeline_mode=pl.Buffered(1)),
            pl.BlockSpec((tm, d), lambda i: (i, 0)),
            pl.BlockSpec((1, d), lambda i: (0, 0)),
            pl.BlockSpec((1, d), lambda i: (0, 0)),
        ],
        out_specs=[pl.BlockSpec((tm, d), lambda i: (i, 0)), pl.BlockSpec((tm, d), lambda i: (i, 0))],
        out_shape=[jax.ShapeDtypeStruct((t, d), F32), jax.ShapeDtypeStruct((t, d), BF16)],
        compiler_params=_params("parallel"),
        name="proj_residual",
    )(a, w, x, g_out.reshape(1, d), g_next.reshape(1, d))


FFN_ROWS = 512
FFN_COLS = 512


def _ffn_up_kernel(h_ref, wg_ref, wv_ref, cg_ref, cv_ref, wd_ref, o_ref, wdb_ref):
    wdb_ref[...] = wd_ref[...].astype(BF16)
    s = h_ref.shape[0]
    rows = min(FFN_ROWS, s)
    tn = o_ref.shape[1]
    row = lax.broadcasted_iota(jnp.int32, (rows, 1), 0)
    wg, wv = wg_ref[...].astype(BF16), wv_ref[...].astype(BF16)
    cg, cv = cg_ref[...], cv_ref[...]

    def conv(u, tail, c):
        u1 = jnp.where(row >= 1, pltpu.roll(u, 1, 0), tail[1:2])
        u2 = jnp.where(row >= 2, pltpu.roll(u, 2, 0), jnp.where(row == 1, tail[1:2], tail[0:1]))
        return (c[0:1] * u2 + c[1:2] * u1) + c[2:3] * u

    tail_g = tail_v = jnp.zeros((CONV_WIDTH - 1, tn), F32)
    for r in range(s // rows):
        hr = h_ref[r * rows:(r + 1) * rows, :]
        ug, uv = _dot(hr, wg), _dot(hr, wv)
        out = _silu(conv(ug, tail_g, cg)) * conv(uv, tail_v, cv)
        o_ref[r * rows:(r + 1) * rows, :] = out.astype(o_ref.dtype)
        tail_g, tail_v = ug[rows - 2:rows], uv[rows - 2:rows]


def ffn_up(h, w_up, conv_w, w_down, layer, batch):
    t, d = h.shape
    s = t // batch
    tn = FFN_COLS
    nj = D_FF // tn
    return pl.pallas_call(
        _ffn_up_kernel,
        grid=(nj, batch),
        in_specs=[
            pl.BlockSpec((s, d), lambda j, b: (b, 0)),
            pl.BlockSpec((None, d, tn), lambda j, b: (layer, 0, j)),
            pl.BlockSpec((None, d, tn), lambda j, b: (layer, 0, nj + j)),
            pl.BlockSpec((None, CONV_WIDTH, tn), lambda j, b: (layer, 0, j)),
            pl.BlockSpec((None, CONV_WIDTH, tn), lambda j, b: (layer, 0, nj + j)),
            pl.BlockSpec((None, tn, d), lambda j, b: (layer, j, 0)),
        ],
        out_specs=[pl.BlockSpec((s, tn), lambda j, b: (b, j)), pl.BlockSpec((None, tn, d), lambda j, b: (0, j, 0))],
        out_shape=[jax.ShapeDtypeStruct((t, D_FF), BF16), jax.ShapeDtypeStruct((1, D_FF, d), BF16)],
        compiler_params=_params("arbitrary", "arbitrary"),
        name="ffn_up",
    )(h, w_up, w_up, conv_w, conv_w, w_down)


def _bucket_thresholds():
    max_exact = NUM_BUCKETS // 2
    n = np.arange(1, 2 * MAX_DISTANCE, dtype=np.float64)
    large = max_exact + np.floor(np.log(n / max_exact) / math.log(MAX_DISTANCE / max_exact) * (NUM_BUCKETS - max_exact))
    bucket = np.where(n < max_exact, n, np.minimum(large, NUM_BUCKETS - 1)).astype(np.int64)
    return tuple(int(n[bucket >= b][0]) for b in range(max_exact + 1, NUM_BUCKETS))


BUCKET_THRESHOLDS = _bucket_thresholds()


def _bias_tile_kernel(tab_ref, causal_ref, window_ref):
    m = pl.program_id(0)
    r = lax.broadcasted_iota(jnp.int32, (TQ, 2 * TQ), 0)
    c = lax.broadcasted_iota(jnp.int32, (TQ, 2 * TQ), 1)
    d = r - c + TQ
    n = jnp.maximum(d, 0)
    max_exact = NUM_BUCKETS // 2
    large = jnp.full_like(n, max_exact)
    for thr in BUCKET_THRESHOLDS:
        large = large + (n >= thr).astype(jnp.int32)
    bucket = jnp.where(n < max_exact, n, large)
    val = jnp.zeros((TQ, 2 * TQ), F32)
    for b in range(NUM_BUCKETS):
        val = jnp.where(bucket == b, tab_ref[b, m] * LOG2_E, val)
    causal = jnp.where(d >= 0, val, NEG_INF)
    causal_ref[0] = causal
    window_ref[0] = jnp.where(d < WINDOW, causal, NEG_INF)


def bias_tiles(table):
    shape = jax.ShapeDtypeStruct((N_HEADS, TQ, 2 * TQ), F32)
    return pl.pallas_call(
        _bias_tile_kernel,
        grid=(N_HEADS,),
        in_specs=[pl.BlockSpec(memory_space=pltpu.SMEM)],
        out_specs=[pl.BlockSpec((1, TQ, 2 * TQ), lambda m: (m, 0, 0))] * 2,
        out_shape=[shape, shape],
        compiler_params=_params("parallel"),
        name="bias_tiles",
    )(table)


def _head_select(x, lo, first):
    zero = jnp.zeros_like(x)
    return jnp.where(lo, x, zero) if first else jnp.where(lo, zero, x)


def _stack_heads(q):
    rows = q.shape[0]
    lo = lax.broadcasted_iota(jnp.int32, (rows, LANES), 1) < HEAD_DIM
    parts = []
    for p in range(q.shape[1] // LANES):
        qp = q[:, p * LANES:(p + 1) * LANES]
        parts += [_head_select(qp, lo, True), _head_select(qp, lo, False)]
    return jnp.concatenate(parts, axis=0)


def _pair_selector(n_keys):
    r = lax.broadcasted_iota(jnp.int32, (2 * n_keys, LANES), 0)
    c = lax.broadcasted_iota(jnp.int32, (2 * n_keys, LANES), 1)
    return jnp.where((r < n_keys) == (c < HEAD_DIM), 1.0, 0.0).astype(BF16)


def _paired_values(vv):
    lo = lax.broadcasted_iota(jnp.int32, vv.shape, 1) < HEAD_DIM
    v_both = jnp.concatenate([_head_select(vv, lo, True), _head_select(vv, lo, False)], axis=0)
    return jnp.concatenate([v_both, _pair_selector(vv.shape[0])], axis=1)


SWA_BLOCK = WINDOW


def _swa_kernel(sink_ref, q_ref, kp_ref, ko_ref, vp_ref, vo_ref, bias_ref, o_ref):
    i = pl.program_id(1)
    blk = SWA_BLOCK
    col = lax.broadcasted_iota(jnp.int32, (1, 2 * blk), 1)
    no_prev = (col < blk) & (i == 0)
    lo = lax.broadcasted_iota(jnp.int32, (blk, LANES), 1) < HEAD_DIM

    def scores(g):
        stack = _stack_heads(q_ref[:, g * 4 * HEAD_DIM:(g + 1) * 4 * HEAD_DIM])
        cols = slice(g * LANES, (g + 1) * LANES)
        kk = jnp.concatenate([kp_ref[:, cols], ko_ref[:, cols]], axis=0)
        s = _dot_nt(stack, kk) + bias_ref[4 * g:4 * g + 4].reshape(4 * blk, 2 * blk)
        return jnp.where(no_prev, NEG_INF, s)

    def softmax(g, s):
        sink = jnp.concatenate([jnp.full((blk, 1), sink_ref[4 * g + hh] * LOG2_E, F32) for hh in range(4)], axis=0)
        m = jnp.maximum(jnp.max(s, axis=-1, keepdims=True), sink)
        return jnp.exp2(s - m).astype(BF16), jnp.exp2(sink - m)

    def finish(g, e, e_sink):
        cols = slice(g * LANES, (g + 1) * LANES)
        v_ext = _paired_values(jnp.concatenate([vp_ref[:, cols], vo_ref[:, cols]], axis=0))
        outs = []
        for p in range(2):
            r0 = 2 * p * blk
            e_both = jnp.concatenate([e[r0:r0 + blk], e[r0 + blk:r0 + 2 * blk]], axis=1)
            res = _dot(e_both, v_ext)
            den = res[:, LANES:] + jnp.where(lo, e_sink[r0:r0 + blk], e_sink[r0 + blk:r0 + 2 * blk])
            outs.append(res[:, :LANES] / den)
        o_ref[:, g * 4 * HEAD_DIM:(g + 1) * 4 * HEAD_DIM] = jnp.concatenate(outs, axis=1).astype(o_ref.dtype)

    s_next = scores(0)
    prev = None
    for g in range(N_KV_HEADS):
        s = s_next
        if g + 1 < N_KV_HEADS:
            s_next = scores(g + 1)
        cur = softmax(g, s)
        if prev is not None:
            finish(g - 1, *prev)
        prev = cur
    finish(N_KV_HEADS - 1, *prev)


def swa_attention(q, kv, sinks, bias_window, batch):
    t, dq = q.shape
    blk = SWA_BLOCK
    nq = t // batch // blk
    dk = 2 * N_KV_HEADS * HEAD_DIM

    def own(col):
        return lambda b, i: (b * nq + i, col)

    def prev(col):
        return lambda b, i: (b * nq + jnp.maximum(i - 1, 0), col)

    bias_spec = pl.BlockSpec((N_HEADS, blk, 2 * blk), lambda b, i: (0, TQ // blk - 1, TQ // blk - 1))
    return pl.pallas_call(
        _swa_kernel,
        grid=(batch, nq),
        in_specs=[
            pl.BlockSpec(memory_space=pltpu.SMEM),
            pl.BlockSpec((blk, dq), own(0)),
            pl.BlockSpec((blk, dk), prev(0)),
            pl.BlockSpec((blk, dk), own(0)),
            pl.BlockSpec((blk, dk), prev(1)),
            pl.BlockSpec((blk, dk), own(1)),
            bias_spec,
        ],
        out_specs=pl.BlockSpec((blk, dq), own(0)),
        out_shape=jax.ShapeDtypeStruct((t, dq), BF16),
        compiler_params=_params("parallel", "parallel"),
        name="swa_attention",
    )(sinks, q, kv, kv, kv, kv, bias_window)


def _causal_scores(stack, k_ref, kv_len, bias_ref, far_bias):
    rows = stack.shape[0]
    s = _dot_nt(stack, k_ref[0:kv_len, :])
    if kv_len == TQ:
        return s + bias_ref[:, :, TQ:2 * TQ].reshape(rows, TQ)
    near = bias_ref[...].reshape(rows, 2 * TQ)
    if kv_len == 2 * TQ:
        return s + near
    n_far = kv_len - 2 * TQ
    return jnp.concatenate([s[:, :n_far] + far_bias, s[:, n_far:] + near], axis=1)


def _softmax_pipeline(n_blocks, scores, finish):
    order = list(range(n_blocks - 1, -1, -1))
    s_next = scores(order[0])
    p_prev = None
    for n in range(n_blocks):
        s = s_next
        if n + 1 < n_blocks:
            s_next = scores(order[n + 1])
        p = jnp.exp2(s - jnp.max(s, axis=-1, keepdims=True)).astype(BF16)
        if p_prev is not None:
            finish(order[n - 1], p_prev)
        p_prev = p
    finish(order[-1], p_prev)


def _diff_kernel(tab_ref, lam_ref, g_ref, q_ref, k_ref, v_ref, bias_ref, o_ref, *, lam_init):
    h = pl.program_id(0)
    last = NUM_BUCKETS - 1
    far_bias = jnp.concatenate([jnp.full((TQ, 1), tab_ref[last, 2 * h] * LOG2_E, F32),
                                jnp.full((TQ, 1), tab_ref[last, 2 * h + 1] * LOG2_E, F32)], axis=0)
    lf = lam_ref[...]
    lam = (jnp.exp(jnp.sum(lf[0:1] * lf[1:2], axis=-1, keepdims=True))
           - jnp.exp(jnp.sum(lf[2:3] * lf[3:4], axis=-1, keepdims=True)) + lam_init)
    ones = jnp.ones((q_ref.shape[0], LANES), BF16)

    def scores(c):
        stack = _stack_heads(q_ref[c * TQ:(c + 1) * TQ, :])
        return _causal_scores(stack, k_ref, (c + 1) * TQ, bias_ref, far_bias)

    def finish(c, p):
        kv_len = (c + 1) * TQ
        v_ext = jnp.concatenate([v_ref[0:kv_len, :], ones[0:kv_len]], axis=1)
        res = _dot(p, v_ext)
        o12 = res[:, :LANES] / res[:, LANES:]
        o = o12[:TQ] - lam * o12[TQ:]
        o_ref[c * TQ:(c + 1) * TQ, :] = (_rms(o, g_ref[...]) * (1.0 - lam_init)).astype(o_ref.dtype)

    _softmax_pipeline(q_ref.shape[0] // TQ, scores, finish)


def diff_attention(q, kv, table, lambdas, subln_g, bias_causal, batch, layer_idx):
    t = q.shape[0]
    s = t // batch
    lam_init = 0.8 - 0.6 * math.exp(-0.3 * layer_idx)
    return pl.pallas_call(
        functools.partial(_diff_kernel, lam_init=lam_init),
        grid=(DIFF_HEADS, batch),
        in_specs=[
            pl.BlockSpec(memory_space=pltpu.SMEM),
            pl.BlockSpec((4, HEAD_DIM), lambda h, b: (0, 0)),
            pl.BlockSpec((1, LANES), lambda h, b: (0, 0)),
            pl.BlockSpec((s, LANES), lambda h, b: (b, h)),
            pl.BlockSpec((s, LANES), lambda h, b: (b, h)),
            pl.BlockSpec((s, LANES), lambda h, b: (b, DIFF_HEADS + h)),
            pl.BlockSpec((2, TQ, 2 * TQ), lambda h, b: (h, 0, 0)),
        ],
        out_specs=pl.BlockSpec((s, LANES), lambda h, b: (b, h)),
        out_shape=jax.ShapeDtypeStruct((t, DIFF_HEADS * 2 * HEAD_DIM), BF16),
        compiler_params=_params("parallel", "parallel"),
        name="diff_attention",
    )(table, lambdas, subln_g.reshape(1, LANES), q, kv, kv, bias_causal)


HGRN_LEVELS = tuple(2 ** e for e in range(int(math.log2(HGRN_CHUNK)) - 1, -1, -1))
HGRN_GROUP = 256


def _hgrn_kernel(lb_ref, g_ref, q_ref, f_ref, i_ref, og_ref, o_ref,
                 qf_ref, kf_ref, bq_ref, kd_ref, dec_ref, v_ref):
    s_len, hd = q_ref.shape
    c_len = HGRN_CHUNK
    n_lvl = len(HGRN_LEVELS)
    grp = min(HGRN_GROUP, s_len)
    rc = lax.broadcasted_iota(jnp.int32, (grp, hd), 0) & (c_len - 1)
    lb = lb_ref[...]
    log_lb, log_1m_lb = jnp.log(lb), jnp.log1p(-lb)

    def log1p01(u):
        w = 1.0 + u
        return jnp.where(w == 1.0, u, jnp.log(w) * (u / (w - 1.0)))

    def prepare(gi):
        rows = slice(gi * grp, (gi + 1) * grp)
        z = f_ref[rows, :]
        log_sig = jnp.minimum(z, 0.0) - log1p01(jnp.exp(-jnp.abs(z)))
        c = log_1m_lb + log_sig
        delta = log_lb - c
        log_f = jnp.where(jnp.isnan(delta), log_lb + c,
                          jnp.maximum(log_lb, c) + log1p01(jnp.exp(-jnp.abs(delta))))
        key = (1.0 - lb) * (1.0 / (1.0 + jnp.exp(z)))
        qs = _silu(q_ref[rows, :])
        v_ref[rows, :] = i_ref[rows, :].astype(BF16)

        b = log_f * LOG2_E
        for d in (1, 2, 4, 8, 16, 32):
            b = b + jnp.where(rc >= d, pltpu.roll(b, d, 0), 0.0)

        bq_ref[rows, :] = (qs * jnp.exp2(b)).astype(BF16)
        qf_ref[n_lvl, rows, :] = qs.astype(BF16)
        kf_ref[n_lvl, rows, :] = key.astype(BF16)

        r_m = jnp.where(rc >= 1, pltpu.roll(b, 1, 0), 0.0)
        for m in (1, 2, 4, 8, 16, 32):
            if m > 1:
                half = m // 2
                r_m = jnp.where((rc & half) != 0, pltpu.roll(r_m, half, 0), r_m)
            e_m = pltpu.roll(r_m, grp - m, 0)
            lvl = HGRN_LEVELS.index(m)
            qf_ref[lvl, rows, :] = (qs * jnp.exp2(b - r_m)).astype(BF16)
            kf_ref[lvl, rows, :] = (key * jnp.exp2(jnp.minimum(e_m - b, 0.0))).astype(BF16)
        b3 = b.reshape(grp // c_len, c_len, hd)
        b_end = jnp.broadcast_to(b3[:, c_len - 1:c_len, :], b3.shape).reshape(grp, hd)
        kd_ref[rows, :] = (key * jnp.exp2(b_end - b)).astype(BF16)
        dec_ref[rows, :] = jnp.exp2(b_end)

    ti = lax.broadcasted_iota(jnp.int32, (c_len, c_len), 0)
    si = lax.broadcasted_iota(jnp.int32, (c_len, c_len), 1)
    masks = [((ti // m) % 2 == 1) & (si // m == ti // m - 1) for m in HGRN_LEVELS] + [ti == si]
    g = g_ref[...]

    def chunk(ci, st):
        rows = slice(ci * c_len, (ci + 1) * c_len)
        att = jnp.zeros((c_len, c_len), F32)
        for lvl in range(n_lvl + 1):
            att = jnp.where(masks[lvl], _dot_nt(qf_ref[lvl, rows, :], kf_ref[lvl, rows, :]), att)
        vc = v_ref[rows, :]
        o = _dot(att.astype(BF16), vc) + _dot_nt(bq_ref[rows, :], st.astype(BF16))
        o_ref[rows, :] = (_rms(o, g) * _silu(og_ref[rows, :])).astype(o_ref.dtype)
        return st * dec_ref[ci * c_len:ci * c_len + 1, :] + _dot_tn(vc, kd_ref[rows, :])

    n_groups = s_len // grp
    per_group = grp // c_len
    st = jnp.zeros((hd, hd), F32)
    prepare(0)
    for gi in range(n_groups):
        if gi + 1 < n_groups:
            prepare(gi + 1)
        for ci in range(gi * per_group, (gi + 1) * per_group):
            st = chunk(ci, st)


def hgrn_recurrence(z, lb, norm_g, batch):
    t = z.shape[0]
    s = t // batch
    hd = HGRN_HEAD_DIM
    n_lvl = len(HGRN_LEVELS)

    def part(p):
        return pl.BlockSpec((s, hd), lambda b, h: (b, p * HGRN_HEADS + h))

    return pl.pallas_call(
        _hgrn_kernel,
        grid=(batch, HGRN_HEADS),
        in_specs=[pl.BlockSpec((1, hd), lambda b, h: (0, h)), pl.BlockSpec((1, hd), lambda b, h: (0, 0)),
                  part(0), part(1), part(2), part(3)],
        out_specs=pl.BlockSpec((s, hd), lambda b, h: (b, h)),
        out_shape=jax.ShapeDtypeStruct((t, HGRN_HEADS * hd), BF16),
        scratch_shapes=[
            pltpu.VMEM((n_lvl + 1, s, hd), BF16),
            pltpu.VMEM((n_lvl + 1, s, hd), BF16),
            pltpu.VMEM((s, hd), BF16),
            pltpu.VMEM((s, hd), BF16),
            pltpu.VMEM((s, hd), F32),
            pltpu.VMEM((s, hd), BF16),
        ],
        compiler_params=_params("parallel", "parallel"),
        name="hgrn_recurrence",
    )(lb.reshape(1, -1), norm_g.reshape(1, hd), z, z, z, z)


def _sortable(score):
    score = jnp.where(score == 0.0, 0.0, score)
    bits = lax.bitcast_convert_type(score, jnp.int32)
    return bits ^ ((bits >> 31) & jnp.int32(0x7FFFFFFF))


def _dsa_select(c, qi_ref, ki_ref, wi_ref, qsel_ref, kib_ref, wt_ref, sc_ref, keys_ref, mask_ref, n_sel):
    kv_len = (c + 1) * TQ
    lo = lax.broadcasted_iota(jnp.int32, (TQ, LANES), 1) < HEAD_DIM
    for p in range(IDX_HEADS // 2):
        qp = qi_ref[:, p * LANES:(p + 1) * LANES]
        qsel_ref[2 * p] = _head_select(qp, lo, True)
        qsel_ref[2 * p + 1] = _head_select(qp, lo, False)
    wt_ref[...] = wi_ref[...].T * (IDX_HEADS ** -0.5 * IDX_DIM ** -0.5)
    ki = ki_ref[0:kv_len, :]
    lo_k = lax.broadcasted_iota(jnp.int32, (kv_len, LANES), 1) < IDX_DIM
    kib_ref[0:kv_len, :] = jnp.where(lo_k, ki, pltpu.roll(ki, IDX_DIM, 1)).astype(BF16)
    sc_ref[0:kv_len, :] = jnp.zeros((kv_len, TQ), F32)

    def head_pair(hp, carry):
        kib = kib_ref[0:kv_len, :]
        acc = sc_ref[0:kv_len, :]
        for hh in (2 * hp, 2 * hp + 1):
            logit = _dot_nt(kib, qsel_ref[hh])
            acc = acc + jnp.maximum(logit, 0.0) * wt_ref[pl.ds(IDX_DIM + hh, 1), :]
        sc_ref[0:kv_len, :] = acc
        return carry

    lax.fori_loop(0, IDX_HEADS // 2, head_pair, 0)

    key_i = lax.broadcasted_iota(jnp.int32, (TQ, TQ), 0)
    qry_i = lax.broadcasted_iota(jnp.int32, (TQ, TQ), 1)
    if c > 0:
        keys_ref[0:c * TQ, :] = _sortable(sc_ref[0:c * TQ, :])
    keys_ref[c * TQ:kv_len, :] = _sortable(jnp.where(key_i <= qry_i, sc_ref[c * TQ:kv_len, :], NEG_INF))

    def count(pred):
        return jnp.sum(pred(keys_ref[0:kv_len, :]).astype(jnp.int32), axis=0, keepdims=True)

    def search(it, thr):
        trial = thr ^ (jnp.int32(1) << (31 - it))
        return jnp.where(count(lambda kj: kj >= trial) >= n_sel, trial, thr)

    thr = lax.fori_loop(0, 32, search, jnp.full((1, TQ), jnp.iinfo(jnp.int32).min, jnp.int32))
    need = (n_sel - count(lambda kj: kj > thr)).astype(F32)
    earlier = (qry_i < key_i).astype(BF16)
    base = jnp.zeros((1, TQ), F32)
    for j in range(c + 1):
        kj = keys_ref[j * TQ:(j + 1) * TQ, :]
        eq = kj == thr
        eq_f = jnp.where(eq, 1.0, 0.0)
        rank = base + _dot(earlier, eq_f.astype(BF16))
        sel = (kj > thr) | (eq & (rank < need))
        mask_ref[:, j * TQ:(j + 1) * TQ] = jnp.where(sel, 0.0, NEG_INF).T.astype(mask_ref.dtype)
        base = base + jnp.sum(eq_f, axis=0, keepdims=True)
    s_len = mask_ref.shape[1]
    if kv_len < s_len:
        mask_ref[:, kv_len:] = jnp.full((TQ, s_len - kv_len), NEG_INF, mask_ref.dtype)


def _dsa_select_kernel(qi_ref, ki_ref, wi_ref, mask_ref, qsel_ref, kib_ref, wt_ref, sc_ref, keys_ref, *, n_sel, nq):
    i = pl.program_id(1)
    for c in range(nq):
        pl.when(i == c)(functools.partial(
            _dsa_select, c, qi_ref, ki_ref, wi_ref, qsel_ref, kib_ref, wt_ref, sc_ref, keys_ref, mask_ref, n_sel))


def _dsa_attend_kernel(tab_ref, q_ref, k_ref, v_ref, mask_ref, bias_ref, o_ref):
    g = pl.program_id(1)
    last = NUM_BUCKETS - 1
    far_bias = jnp.concatenate(
        [jnp.full((TQ, 1), tab_ref[last, 4 * g + hh] * LOG2_E, F32) for hh in range(4)], axis=0)

    def scores(c):
        rows = slice(c * TQ, (c + 1) * TQ)
        kv_len = (c + 1) * TQ
        s = _causal_scores(_stack_heads(q_ref[rows, :]), k_ref, kv_len, bias_ref, far_bias)
        return s + jnp.concatenate([mask_ref[rows, 0:kv_len].astype(F32)] * 4, axis=0)

    def finish(c, p):
        v_ext = _paired_values(v_ref[0:(c + 1) * TQ, :])
        outs = []
        for pr in range(2):
            r0 = 2 * pr * TQ
            p_both = jnp.concatenate([p[r0:r0 + TQ], p[r0 + TQ:r0 + 2 * TQ]], axis=1)
            res = _dot(p_both, v_ext)
            outs.append(res[:, :LANES] / res[:, LANES:])
        o_ref[c * TQ:(c + 1) * TQ, :] = jnp.concatenate(outs, axis=1).astype(o_ref.dtype)

    _softmax_pipeline(q_ref.shape[0] // TQ, scores, finish)


def dsa_attention(q, kv, qi, z_idx, table, bias_causal, batch):
    t = q.shape[0]
    s = t // batch
    nq = s // TQ
    n_sel = min(TOPK_MAX, s // 4)
    mask = pl.pallas_call(
        functools.partial(_dsa_select_kernel, n_sel=n_sel, nq=nq),
        grid=(batch, nq),
        in_specs=[
            pl.BlockSpec((TQ, IDX_HEADS * IDX_DIM), lambda b, i: (b * nq + i, 0)),
            pl.BlockSpec((s, LANES), lambda b, i: (b, 0)),
            pl.BlockSpec((TQ, LANES), lambda b, i: (b * nq + i, 0)),
        ],
        out_specs=pl.BlockSpec((TQ, s), lambda b, i: (b * nq + i, 0)),
        out_shape=jax.ShapeDtypeStruct((t, s), BF16),
        scratch_shapes=[
            pltpu.VMEM((IDX_HEADS, TQ, LANES), BF16),
            pltpu.VMEM((s, LANES), BF16),
            pltpu.VMEM((LANES, TQ), F32),
            pltpu.VMEM((s, TQ), F32),
            pltpu.VMEM((s, TQ), jnp.int32),
        ],
        compiler_params=_params("parallel", "parallel"),
        name="dsa_select",
    )(qi, z_idx, z_idx)
    return pl.pallas_call(
        _dsa_attend_kernel,
        grid=(batch, N_KV_HEADS),
        in_specs=[
            pl.BlockSpec(memory_space=pltpu.SMEM),
            pl.BlockSpec((s, 4 * HEAD_DIM), lambda b, g: (b, g)),
            pl.BlockSpec((s, LANES), lambda b, g: (b, g)),
            pl.BlockSpec((s, LANES), lambda b, g: (b, N_KV_HEADS + g)),
            pl.BlockSpec((s, s), lambda b, g: (b, 0)),
            pl.BlockSpec((4, TQ, 2 * TQ), lambda b, g: (g, 0, 0)),
        ],
        out_specs=pl.BlockSpec((s, 4 * HEAD_DIM), lambda b, g: (b, g)),
        out_shape=jax.ShapeDtypeStruct((t, N_HEADS * HEAD_DIM), BF16),
        compiler_params=_params("parallel", "parallel"),
        name="dsa_attend",
    )(table, q, kv, kv, mask, bias_causal)


def _lower_bound_kernel(x_ref, o_ref):
    x = x_ref[...]
    e = jnp.exp(x - jnp.max(x, axis=0, keepdims=True))
    soft = e / jnp.sum(e, axis=0, keepdims=True)
    run = soft[0:1]
    o_ref[0:1, :] = run - soft[0:1]
    for r in range(1, x.shape[0]):
        run = run + soft[r:r + 1]
        o_ref[r:r + 1, :] = run - soft[0:1]


def hgrn_lower_bounds(logits):
    return pl.pallas_call(
        _lower_bound_kernel,
        out_shape=jax.ShapeDtypeStruct(logits.shape, F32),
        name="hgrn_lower_bounds",
    )(logits)


def kernel(x, rel_bias_table, hgrn_lb_logits, norm_g, ffn_w_up, ffn_conv, ffn_w_down, swa_w_in, swa_w_out,
           swa_sinks, diff_w_in, diff_w_out, diff_lambda, diff_subln_g, hgrn_w_in, hgrn_w_out, hgrn_norm_g,
           dsa_w_in, dsa_w_out):
    batch, seq, d = x.shape
    t = batch * seq
    depth = norm_g.shape[0]
    n_mixers = 4
    bias_causal, bias_window = bias_tiles(rel_bias_table)
    lb_all = hgrn_lower_bounds(hgrn_lb_logits)
    kv = N_KV_HEADS * HEAD_DIM
    q_scale = HEAD_DIM ** -0.5 * LOG2_E

    x2 = x.reshape(t, d)
    h = None
    for i in range(depth):
        kind, j = i % n_mixers, i // n_mixers
        if kind == 0:
            w = swa_w_in[j]
            src, g0 = (x2, norm_g[i, 0]) if h is None else (h, None)
            q = matmul(src, w, BF16, n_cols=d, scale=q_scale, norm_g=g0)
            kvd = matmul(src, w, BF16, col0=d, n_cols=2 * kv, dup_heads=True, norm_g=g0)
            o = swa_attention(q, kvd, swa_sinks[j], bias_window, batch)
            w_out = swa_w_out
        elif kind == 1:
            w = diff_w_in[j]
            q = matmul(h, w, BF16, n_cols=d, scale=q_scale)
            kvp = matmul(h, w, BF16, col0=d)
            o = diff_attention(q, kvp, rel_bias_table, diff_lambda[j], diff_subln_g[j], bias_causal, batch, i)
            w_out = diff_w_out
        elif kind == 2:
            z = matmul(h, hgrn_w_in[j], F32)
            o = hgrn_recurrence(z, lb_all[i], hgrn_norm_g[j], batch)
            w_out = hgrn_w_out
        else:
            w = dsa_w_in[j]
            n_qi = IDX_HEADS * IDX_DIM
            q = matmul(h, w, BF16, n_cols=d, scale=q_scale)
            kvd = matmul(h, w, BF16, col0=d, n_cols=2 * kv, dup_heads=True)
            qi = matmul(h, w, BF16, col0=d + 2 * kv, n_cols=n_qi)
            w_idx = jnp.pad(w[:, d + 2 * kv + n_qi:], ((0, 0), (0, LANES - IDX_DIM - IDX_HEADS)))
            z_idx = matmul(h, w_idx, F32)
            o = dsa_attention(q, kvd, qi, z_idx, rel_bias_table, bias_causal, batch)
            w_out = dsa_w_out
        x2, h = proj_residual(o, w_out.astype(BF16), j, x2, norm_g[i, 1], norm_g[i, 2])
        a, w_down = ffn_up(h, ffn_w_up, ffn_conv, ffn_w_down, i, batch)
        x2, h = proj_residual(a, w_down, 0, x2, norm_g[i, 3], norm_g[(i + 1) % depth, 0])
    return x2.reshape(batch, seq, d)
```

```python
import functools
import math

import jax
import jax.numpy as jnp
import numpy as np
from jax import lax
from jax.experimental import pallas as pl
from jax.experimental.pallas import tpu as pltpu

D_MODEL = 2048
HEAD_DIM = 64
N_HEADS = 32
N_KV_HEADS = 8
WINDOW = 128
DIFF_HEADS = 16
HGRN_HEADS = 16
HGRN_HEAD_DIM = 128
HGRN_CHUNK = 64
IDX_HEADS = 16
IDX_DIM = 64
TOPK_MAX = 256
D_FF = 5632
CONV_WIDTH = 3
NUM_BUCKETS = 32
MAX_DISTANCE = 128
RMS_EPS = 1e-6

LANES = 128
TQ = 256
VMEM_LIMIT = 56 * 1024 * 1024
NEG_INF = float("-inf")
LOG2_E = math.log2(math.e)
BF16 = jnp.bfloat16
F32 = jnp.float32


def _params(*sem):
    return pltpu.CompilerParams(dimension_semantics=sem, vmem_limit_bytes=VMEM_LIMIT)


def _dot(a, b):
    return jnp.dot(a, b, preferred_element_type=F32)


def _dot_nt(a, b):
    return lax.dot_general(a, b, (((1,), (1,)), ((), ())), preferred_element_type=F32)


def _dot_tn(a, b):
    return lax.dot_general(a, b, (((0,), (0,)), ((), ())), preferred_element_type=F32)


def _rms(x, g):
    return x * lax.rsqrt(jnp.mean(x * x, axis=-1, keepdims=True) + RMS_EPS) * g


def _silu(x):
    return x * (1.0 / (1.0 + jnp.exp(-x)))


def _dup_heads(w):
    lo = lax.broadcasted_iota(jnp.int32, (w.shape[0], LANES), 1) < HEAD_DIM
    parts = []
    for p in range(w.shape[1] // LANES):
        blk = w[:, p * LANES:(p + 1) * LANES]
        swapped = pltpu.roll(blk, HEAD_DIM, 1)
        parts += [jnp.where(lo, blk, swapped), jnp.where(lo, swapped, blk)]
    return jnp.concatenate(parts, axis=1)


def _mm_kernel(x_ref, g_ref, w_ref, o_ref, wb_ref, *, scale, dup_heads, norm):
    @pl.when(pl.program_id(1) == 0)
    def _():
        w = w_ref[...]
        wb_ref[...] = (_dup_heads(w) if dup_heads else w).astype(BF16)

    x = _rms(x_ref[...], g_ref[...]).astype(BF16) if norm else x_ref[...]
    y = _dot(x, wb_ref[...])
    o_ref[...] = (y if scale == 1.0 else y * scale).astype(o_ref.dtype)


def matmul(x, w, out_dtype, *, col0=0, n_cols=None, scale=1.0, dup_heads=False, norm_g=None):
    t, k = x.shape
    n_cols = w.shape[1] - col0 if n_cols is None else n_cols
    tm = 1024
    tn_out = next(c for c in (1024, 512, 256, 128) if (n_cols * (2 if dup_heads else 1)) % c == 0)
    tn_in = tn_out // 2 if dup_heads else tn_out
    assert n_cols % tn_in == 0 and col0 % tn_in == 0, (n_cols, col0, tn_in)
    n_out = n_cols * (2 if dup_heads else 1)
    norm = norm_g is not None
    g = norm_g.reshape(1, k) if norm else jnp.ones((1, k), F32)
    return pl.pallas_call(
        functools.partial(_mm_kernel, scale=scale, dup_heads=dup_heads, norm=norm),
        grid=(n_cols // tn_in, t // tm),
        in_specs=[pl.BlockSpec((tm, k), lambda j, i: (i, 0)),
                  pl.BlockSpec((1, k), lambda j, i: (0, 0)),
                  pl.BlockSpec((k, tn_in), lambda j, i: (0, col0 // tn_in + j))],
        out_specs=pl.BlockSpec((tm, tn_out), lambda j, i: (i, j)),
        out_shape=jax.ShapeDtypeStruct((t, n_out), out_dtype),
        scratch_shapes=[pltpu.VMEM((k, tn_out), BF16)],
        compiler_params=_params("parallel", "arbitrary"),
        name="in_proj",
    )(x, g, w)


PROJ_SUB_ROWS = 128


def _proj_res_kernel(a_ref, w_ref, x_ref, go_ref, gn_ref, xo_ref, h_ref):
    tm = a_ref.shape[0]
    sub = PROJ_SUB_ROWS if tm >= 4 * PROJ_SUB_ROWS else tm
    ys = [_dot(a_ref[r:r + sub, :], w_ref[...]) for r in range(0, tm, sub)]
    for n, r in enumerate(range(0, tm, sub)):
        xn = x_ref[r:r + sub, :] + _rms(ys[n], go_ref[...])
        xo_ref[r:r + sub, :] = xn
        h_ref[r:r + sub, :] = _rms(xn, gn_ref[...]).astype(h_ref.dtype)


def _proj_rows(k, d):
    for tm in (512, 256, 128):
        streamed = 2 * (tm * k * 2 + 2 * tm * d * 4 + tm * d * 2)
        temporaries = 3 * tm * d * 4
        if k * d * 2 + streamed + temporaries <= VMEM_LIMIT - (4 << 20):
            return tm
    raise ValueError("projection weight does not fit in VMEM")


def proj_residual(a, w, layer, x, g_out, g_next):
    t, k = a.shape
    d = w.shape[2]
    tm = _proj_rows(k, d)
    return pl.pallas_call(
        _proj_res_kernel,
        grid=(t // tm,),
        in_specs=[
            pl.BlockSpec((tm, k), lambda i: (i, 0)),
            pl.BlockSpec((None, k, d), lambda i: (layer, 0, 0), pipeline_mode=pl.Buffered(1)),
            pl.BlockSpec((tm, d), lambda i: (i, 0)),
            pl.BlockSpec((1, d), lambda i: (0, 0)),
            pl.BlockSpec((1, d), lambda i: (0, 0)),
        ],
        out_specs=[pl.BlockSpec((tm, d), lambda i: (i, 0)), pl.BlockSpec((tm, d), lambda i: (i, 0))],
        out_shape=[jax.ShapeDtypeStruct((t, d), F32), jax.ShapeDtypeStruct((t, d), BF16)],
        compiler_params=_params("parallel"),
        name="proj_residual",
    )(a, w, x, g_out.reshape(1, d), g_next.reshape(1, d))


FFN_ROWS = 512
FFN_COLS = 512


def _ffn_up_kernel(h_ref, wg_ref, wv_ref, cg_ref, cv_ref, wd_ref, o_ref, wdb_ref):
    wdb_ref[...] = wd_ref[...].astype(BF16)
    s = h_ref.shape[0]
    rows = min(FFN_ROWS, s)
    tn = o_ref.shape[1]
    row = lax.broadcasted_iota(jnp.int32, (rows, 1), 0)
    wg, wv = wg_ref[...].astype(BF16), wv_ref[...].astype(BF16)
    cg, cv = cg_ref[...], cv_ref[...]

    def conv(u, tail, c):
        u1 = jnp.where(row >= 1, pltpu.roll(u, 1, 0), tail[1:2])
        u2 = jnp.where(row >= 2, pltpu.roll(u, 2, 0), jnp.where(row == 1, tail[1:2], tail[0:1]))
        return (c[0:1] * u2 + c[1:2] * u1) + c[2:3] * u

    tail_g = tail_v = jnp.zeros((CONV_WIDTH - 1, tn), F32)
    for r in range(s // rows):
        hr = h_ref[r * rows:(r + 1) * rows, :]
        ug, uv = _dot(hr, wg), _dot(hr, wv)
        out = _silu(conv(ug, tail_g, cg)) * conv(uv, tail_v, cv)
        o_ref[r * rows:(r + 1) * rows, :] = out.astype(o_ref.dtype)
        tail_g, tail_v = ug[rows - 2:rows], uv[rows - 2:rows]


def ffn_up(h, w_up, conv_w, w_down, layer, batch):
    t, d = h.shape
    s = t // batch
    tn = FFN_COLS
    nj = D_FF // tn
    return pl.pallas_call(
        _ffn_up_kernel,
        grid=(nj, batch),
        in_specs=[
            pl.BlockSpec((s, d), lambda j, b: (b, 0)),
            pl.BlockSpec((None, d, tn), lambda j, b: (layer, 0, j)),
            pl.BlockSpec((None, d, tn), lambda j, b: (layer, 0, nj + j)),
            pl.BlockSpec((None, CONV_WIDTH, tn), lambda j, b: (layer, 0, j)),
            pl.BlockSpec((None, CONV_WIDTH, tn), lambda j, b: (layer, 0, nj + j)),
            pl.BlockSpec((None, tn, d), lambda j, b: (layer, j, 0)),
        ],
        out_specs=[pl.BlockSpec((s, tn), lambda j, b: (b, j)), pl.BlockSpec((None, tn, d), lambda j, b: (0, j, 0))],
        out_shape=[jax.ShapeDtypeStruct((t, D_FF), BF16), jax.ShapeDtypeStruct((1, D_FF, d), BF16)],
        compiler_params=_params("arbitrary", "arbitrary"),
        name="ffn_up",
    )(h, w_up, w_up, conv_w, conv_w, w_down)


def _bucket_thresholds():
    max_exact = NUM_BUCKETS // 2
    n = np.arange(1, 2 * MAX_DISTANCE, dtype=np.float64)
    large = max_exact + np.floor(np.log(n / max_exact) / math.log(MAX_DISTANCE / max_exact) * (NUM_BUCKETS - max_exact))
    bucket = np.where(n < max_exact, n, np.minimum(large, NUM_BUCKETS - 1)).astype(np.int64)
    return tuple(int(n[bucket >= b][0]) for b in range(max_exact + 1, NUM_BUCKETS))


BUCKET_THRESHOLDS = _bucket_thresholds()


BIAS_MAPS = 8


def _bias_tile_kernel(tab_ref, causal_ref, window_ref):
    m0 = pl.program_id(0) * BIAS_MAPS
    r = lax.broadcasted_iota(jnp.int32, (TQ, 2 * TQ), 0)
    c = lax.broadcasted_iota(jnp.int32, (TQ, 2 * TQ), 1)
    d = r - c + TQ
    n = jnp.maximum(d, 0)
    max_exact = NUM_BUCKETS // 2
    large = jnp.full_like(n, max_exact)
    for thr in BUCKET_THRESHOLDS:
        large = large + (n >= thr).astype(jnp.int32)
    bucket = jnp.where(n < max_exact, n, large)
    for mm in range(BIAS_MAPS):
        val = jnp.zeros((TQ, 2 * TQ), F32)
        for b in range(NUM_BUCKETS):
            val = jnp.where(bucket == b, tab_ref[b, m0 + mm] * LOG2_E, val)
        causal = jnp.where(d >= 0, val, NEG_INF)
        causal_ref[mm] = causal
        window_ref[mm] = jnp.where(d < WINDOW, causal, NEG_INF)


def bias_tiles(table):
    shape = jax.ShapeDtypeStruct((N_HEADS, TQ, 2 * TQ), F32)
    return pl.pallas_call(
        _bias_tile_kernel,
        grid=(N_HEADS // BIAS_MAPS,),
        in_specs=[pl.BlockSpec(memory_space=pltpu.SMEM)],
        out_specs=[pl.BlockSpec((BIAS_MAPS, TQ, 2 * TQ), lambda m: (m, 0, 0))] * 2,
        out_shape=[shape, shape],
        compiler_params=_params("parallel"),
        name="bias_tiles",
    )(table)


def _head_select(x, lo, first):
    zero = jnp.zeros_like(x)
    return jnp.where(lo, x, zero) if first else jnp.where(lo, zero, x)


def _stack_heads(q):
    rows = q.shape[0]
    lo = lax.broadcasted_iota(jnp.int32, (rows, LANES), 1) < HEAD_DIM
    parts = []
    for p in range(q.shape[1] // LANES):
        qp = q[:, p * LANES:(p + 1) * LANES]
        parts += [_head_select(qp, lo, True), _head_select(qp, lo, False)]
    return jnp.concatenate(parts, axis=0)


def _pair_selector(n_keys):
    r = lax.broadcasted_iota(jnp.int32, (2 * n_keys, LANES), 0)
    c = lax.broadcasted_iota(jnp.int32, (2 * n_keys, LANES), 1)
    return jnp.where((r < n_keys) == (c < HEAD_DIM), 1.0, 0.0).astype(BF16)


def _paired_values(vv):
    lo = lax.broadcasted_iota(jnp.int32, vv.shape, 1) < HEAD_DIM
    v_both = jnp.concatenate([_head_select(vv, lo, True), _head_select(vv, lo, False)], axis=0)
    return jnp.concatenate([v_both, _pair_selector(vv.shape[0])], axis=1)


SWA_BLOCK = WINDOW


def _swa_kernel(sink_ref, q_ref, kp_ref, ko_ref, vp_ref, vo_ref, bias_ref, o_ref):
    i = pl.program_id(1)
    blk = SWA_BLOCK
    col = lax.broadcasted_iota(jnp.int32, (1, 2 * blk), 1)
    no_prev = (col < blk) & (i == 0)
    lo = lax.broadcasted_iota(jnp.int32, (blk, LANES), 1) < HEAD_DIM

    def scores(g):
        stack = _stack_heads(q_ref[:, g * 4 * HEAD_DIM:(g + 1) * 4 * HEAD_DIM])
        cols = slice(g * LANES, (g + 1) * LANES)
        kk = jnp.concatenate([kp_ref[:, cols], ko_ref[:, cols]], axis=0)
        s = _dot_nt(stack, kk) + bias_ref[4 * g:4 * g + 4].reshape(4 * blk, 2 * blk)
        return jnp.where(no_prev, NEG_INF, s)

    def softmax(g, s):
        sink = jnp.concatenate([jnp.full((blk, 1), sink_ref[4 * g + hh] * LOG2_E, F32) for hh in range(4)], axis=0)
        m = jnp.maximum(jnp.max(s, axis=-1, keepdims=True), sink)
        return jnp.exp2(s - m).astype(BF16), jnp.exp2(sink - m)

    def finish(g, e, e_sink):
        cols = slice(g * LANES, (g + 1) * LANES)
        v_ext = _paired_values(jnp.concatenate([vp_ref[:, cols], vo_ref[:, cols]], axis=0))
        outs = []
        for p in range(2):
            r0 = 2 * p * blk
            e_both = jnp.concatenate([e[r0:r0 + blk], e[r0 + blk:r0 + 2 * blk]], axis=1)
            res = _dot(e_both, v_ext)
            den = res[:, LANES:] + jnp.where(lo, e_sink[r0:r0 + blk], e_sink[r0 + blk:r0 + 2 * blk])
            outs.append(res[:, :LANES] / den)
        o_ref[:, g * 4 * HEAD_DIM:(g + 1) * 4 * HEAD_DIM] = jnp.concatenate(outs, axis=1).astype(o_ref.dtype)

    s_next = scores(0)
    prev = None
    for g in range(N_KV_HEADS):
        s = s_next
        if g + 1 < N_KV_HEADS:
            s_next = scores(g + 1)
        cur = softmax(g, s)
        if prev is not None:
            finish(g - 1, *prev)
        prev = cur
    finish(N_KV_HEADS - 1, *prev)


def swa_attention(q, kv, sinks, bias_window, batch):
    t, dq = q.shape
    blk = SWA_BLOCK
    nq = t // batch // blk
    dk = 2 * N_KV_HEADS * HEAD_DIM

    def own(col):
        return lambda b, i: (b * nq + i, col)

    def prev(col):
        return lambda b, i: (b * nq + jnp.maximum(i - 1, 0), col)

    bias_spec = pl.BlockSpec((N_HEADS, blk, 2 * blk), lambda b, i: (0, TQ // blk - 1, TQ // blk - 1))
    return pl.pallas_call(
        _swa_kernel,
        grid=(batch, nq),
        in_specs=[
            pl.BlockSpec(memory_space=pltpu.SMEM),
            pl.BlockSpec((blk, dq), own(0)),
            pl.BlockSpec((blk, dk), prev(0)),
            pl.BlockSpec((blk, dk), own(0)),
            pl.BlockSpec((blk, dk), prev(1)),
            pl.BlockSpec((blk, dk), own(1)),
            bias_spec,
        ],
        out_specs=pl.BlockSpec((blk, dq), own(0)),
        out_shape=jax.ShapeDtypeStruct((t, dq), BF16),
        compiler_params=_params("parallel", "parallel"),
        name="swa_attention",
    )(sinks, q, kv, kv, kv, kv, bias_window)


def _causal_scores(stack, k_ref, kv_len, bias_ref, far_bias):
    rows = stack.shape[0]
    s = _dot_nt(stack, k_ref[0:kv_len, :])
    if kv_len == TQ:
        return s + bias_ref[:, :, TQ:2 * TQ].reshape(rows, TQ)
    near = bias_ref[...].reshape(rows, 2 * TQ)
    if kv_len == 2 * TQ:
        return s + near
    n_far = kv_len - 2 * TQ
    return jnp.concatenate([s[:, :n_far] + far_bias, s[:, n_far:] + near], axis=1)


def _softmax_pipeline(n_blocks, scores, finish):
    order = list(range(n_blocks - 1, -1, -1))
    s_next = scores(order[0])
    p_prev = None
    for n in range(n_blocks):
        s = s_next
        if n + 1 < n_blocks:
            s_next = scores(order[n + 1])
        p = jnp.exp2(s - jnp.max(s, axis=-1, keepdims=True)).astype(BF16)
        if p_prev is not None:
            finish(order[n - 1], p_prev)
        p_prev = p
    finish(order[-1], p_prev)


def _diff_kernel(tab_ref, lam_ref, g_ref, q_ref, k_ref, v_ref, bias_ref, o_ref, *, lam_init):
    h = pl.program_id(0)
    last = NUM_BUCKETS - 1
    far_bias = jnp.concatenate([jnp.full((TQ, 1), tab_ref[last, 2 * h] * LOG2_E, F32),
                                jnp.full((TQ, 1), tab_ref[last, 2 * h + 1] * LOG2_E, F32)], axis=0)
    lf = lam_ref[...]
    lam = (jnp.exp(jnp.sum(lf[0:1] * lf[1:2], axis=-1, keepdims=True))
           - jnp.exp(jnp.sum(lf[2:3] * lf[3:4], axis=-1, keepdims=True)) + lam_init)
    ones = jnp.ones((q_ref.shape[0], LANES), BF16)

    def scores(c):
        stack = _stack_heads(q_ref[c * TQ:(c + 1) * TQ, :])
        return _causal_scores(stack, k_ref, (c + 1) * TQ, bias_ref, far_bias)

    def finish(c, p):
        kv_len = (c + 1) * TQ
        v_ext = jnp.concatenate([v_ref[0:kv_len, :], ones[0:kv_len]], axis=1)
        res = _dot(p, v_ext)
        o12 = res[:, :LANES] / res[:, LANES:]
        o = o12[:TQ] - lam * o12[TQ:]
        o_ref[c * TQ:(c + 1) * TQ, :] = (_rms(o, g_ref[...]) * (1.0 - lam_init)).astype(o_ref.dtype)

    _softmax_pipeline(q_ref.shape[0] // TQ, scores, finish)


def diff_attention(q, kv, table, lambdas, subln_g, bias_causal, batch, layer_idx):
    t = q.shape[0]
    s = t // batch
    lam_init = 0.8 - 0.6 * math.exp(-0.3 * layer_idx)
    return pl.pallas_call(
        functools.partial(_diff_kernel, lam_init=lam_init),
        grid=(DIFF_HEADS, batch),
        in_specs=[
            pl.BlockSpec(memory_space=pltpu.SMEM),
            pl.BlockSpec((4, HEAD_DIM), lambda h, b: (0, 0)),
            pl.BlockSpec((1, LANES), lambda h, b: (0, 0)),
            pl.BlockSpec((s, LANES), lambda h, b: (b, h)),
            pl.BlockSpec((s, LANES), lambda h, b: (b, h)),
            pl.BlockSpec((s, LANES), lambda h, b: (b, DIFF_HEADS + h)),
            pl.BlockSpec((2, TQ, 2 * TQ), lambda h, b: (h, 0, 0)),
        ],
        out_specs=pl.BlockSpec((s, LANES), lambda h, b: (b, h)),
        out_shape=jax.ShapeDtypeStruct((t, DIFF_HEADS * 2 * HEAD_DIM), BF16),
        compiler_params=_params("parallel", "parallel"),
        name="diff_attention",
    )(table, lambdas, subln_g.reshape(1, LANES), q, kv, kv, bias_causal)


HGRN_LEVELS = tuple(2 ** e for e in range(int(math.log2(HGRN_CHUNK)) - 1, -1, -1))
HGRN_GROUP = 256


def _hgrn_kernel(lb_ref, g_ref, q_ref, f_ref, i_ref, og_ref, o_ref,
                 qf_ref, kf_ref, bq_ref, kd_ref, dec_ref, v_ref):
    s_len, hd = q_ref.shape
    c_len = HGRN_CHUNK
    n_lvl = len(HGRN_LEVELS)
    grp = min(HGRN_GROUP, s_len)
    rc = lax.broadcasted_iota(jnp.int32, (grp, hd), 0) & (c_len - 1)
    lb = lb_ref[...]
    log_lb, log_1m_lb = jnp.log(lb), jnp.log1p(-lb)

    def prepare(gi):
        rows = slice(gi * grp, (gi + 1) * grp)
        z = f_ref[rows, :]
        u = jnp.exp(-jnp.abs(z))
        w = 1.0 + u
        log_sig = jnp.minimum(z, 0.0) - jnp.log(w)
        c = log_1m_lb + log_sig
        delta = log_lb - c
        log_f = jnp.where(jnp.isnan(delta), log_lb + c,
                          jnp.maximum(log_lb, c) + jnp.log(1.0 + jnp.exp(-jnp.abs(delta))))
        inv_w = 1.0 / w
        key = (1.0 - lb) * jnp.where(z >= 0.0, u * inv_w, inv_w)
        qs = _silu(q_ref[rows, :])
        v_ref[rows, :] = i_ref[rows, :].astype(BF16)

        b = log_f * LOG2_E
        for d in (1, 2, 4, 8, 16, 32):
            b = b + jnp.where(rc >= d, pltpu.roll(b, d, 0), 0.0)

        bq_ref[rows, :] = (qs * jnp.exp2(b)).astype(BF16)
        qf_ref[n_lvl, rows, :] = qs.astype(BF16)
        kf_ref[n_lvl, rows, :] = key.astype(BF16)

        r_m = jnp.where(rc >= 1, pltpu.roll(b, 1, 0), 0.0)
        for m in (1, 2, 4, 8, 16, 32):
            if m > 1:
                half = m // 2
                r_m = jnp.where((rc & half) != 0, pltpu.roll(r_m, half, 0), r_m)
            e_m = pltpu.roll(r_m, grp - m, 0)
            lvl = HGRN_LEVELS.index(m)
            qf_ref[lvl, rows, :] = (qs * jnp.exp2(b - r_m)).astype(BF16)
            kf_ref[lvl, rows, :] = (key * jnp.exp2(jnp.minimum(e_m - b, 0.0))).astype(BF16)
        b3 = b.reshape(grp // c_len, c_len, hd)
        b_end = jnp.broadcast_to(b3[:, c_len - 1:c_len, :], b3.shape).reshape(grp, hd)
        kd_ref[rows, :] = (key * jnp.exp2(b_end - b)).astype(BF16)
        dec_ref[rows, :] = jnp.exp2(b_end)

    ti = lax.broadcasted_iota(jnp.int32, (c_len, c_len), 0)
    si = lax.broadcasted_iota(jnp.int32, (c_len, c_len), 1)
    masks = [((ti // m) % 2 == 1) & (si // m == ti // m - 1) for m in HGRN_LEVELS] + [ti == si]
    g = g_ref[...]

    def chunk(ci, st):
        rows = slice(ci * c_len, (ci + 1) * c_len)
        att = jnp.zeros((c_len, c_len), F32)
        for lvl in range(n_lvl + 1):
            att = jnp.where(masks[lvl], _dot_nt(qf_ref[lvl, rows, :], kf_ref[lvl, rows, :]), att)
        vc = v_ref[rows, :]
        o = _dot(att.astype(BF16), vc) + _dot_nt(bq_ref[rows, :], st.astype(BF16))
        o_ref[rows, :] = (_rms(o, g) * _silu(og_ref[rows, :])).astype(o_ref.dtype)
        return st * dec_ref[ci * c_len:ci * c_len + 1, :] + _dot_tn(vc, kd_ref[rows, :])

    n_groups = s_len // grp
    per_group = grp // c_len
    st = jnp.zeros((hd, hd), F32)
    prepare(0)
    for gi in range(n_groups):
        if gi + 1 < n_groups:
            prepare(gi + 1)
        for ci in range(gi * per_group, (gi + 1) * per_group):
            st = chunk(ci, st)


def hgrn_recurrence(z, lb, norm_g, batch):
    t = z.shape[0]
    s = t // batch
    hd = HGRN_HEAD_DIM
    n_lvl = len(HGRN_LEVELS)

    def part(p):
        return pl.BlockSpec((s, hd), lambda b, h: (b, p * HGRN_HEADS + h))

    return pl.pallas_call(
        _hgrn_kernel,
        grid=(batch, HGRN_HEADS),
        in_specs=[pl.BlockSpec((1, hd), lambda b, h: (0, h)), pl.BlockSpec((1, hd), lambda b, h: (0, 0)),
                  part(0), part(1), part(2), part(3)],
        out_specs=pl.BlockSpec((s, hd), lambda b, h: (b, h)),
        out_shape=jax.ShapeDtypeStruct((t, HGRN_HEADS * hd), BF16),
        scratch_shapes=[
            pltpu.VMEM((n_lvl + 1, s, hd), BF16),
            pltpu.VMEM((n_lvl + 1, s, hd), BF16),
            pltpu.VMEM((s, hd), BF16),
            pltpu.VMEM((s, hd), BF16),
            pltpu.VMEM((s, hd), F32),
            pltpu.VMEM((s, hd), BF16),
        ],
        compiler_params=_params("parallel", "parallel"),
        name="hgrn_recurrence",
    )(lb.reshape(1, -1), norm_g.reshape(1, hd), z, z, z, z)


def _sortable(score):
    score = jnp.where(score == 0.0, 0.0, score)
    bits = lax.bitcast_convert_type(score, jnp.int32)
    return bits ^ ((bits >> 31) & jnp.int32(0x7FFFFFFF))


def _dsa_select(c, qi_ref, ki_ref, wi_ref, qsel_ref, kib_ref, wt_ref, sc_ref, keys_ref, mask_ref, n_sel):
    kv_len = (c + 1) * TQ
    lo = lax.broadcasted_iota(jnp.int32, (TQ, LANES), 1) < HEAD_DIM
    for p in range(IDX_HEADS // 2):
        qp = qi_ref[:, p * LANES:(p + 1) * LANES]
        qsel_ref[2 * p] = _head_select(qp, lo, True)
        qsel_ref[2 * p + 1] = _head_select(qp, lo, False)
    wt_ref[...] = wi_ref[...].T * (IDX_HEADS ** -0.5 * IDX_DIM ** -0.5)
    ki = ki_ref[0:kv_len, :]
    lo_k = lax.broadcasted_iota(jnp.int32, (kv_len, LANES), 1) < IDX_DIM
    kib_ref[0:kv_len, :] = jnp.where(lo_k, ki, pltpu.roll(ki, IDX_DIM, 1)).astype(BF16)
    sc_ref[0:kv_len, :] = jnp.zeros((kv_len, TQ), F32)

    def head_pair(hp, carry):
        kib = kib_ref[0:kv_len, :]
        acc = sc_ref[0:kv_len, :]
        for hh in (2 * hp, 2 * hp + 1):
            logit = _dot_nt(kib, qsel_ref[hh])
            acc = acc + jnp.maximum(logit, 0.0) * wt_ref[pl.ds(IDX_DIM + hh, 1), :]
        sc_ref[0:kv_len, :] = acc
        return carry

    lax.fori_loop(0, IDX_HEADS // 2, head_pair, 0)

    key_i = lax.broadcasted_iota(jnp.int32, (TQ, TQ), 0)
    qry_i = lax.broadcasted_iota(jnp.int32, (TQ, TQ), 1)
    if c > 0:
        keys_ref[0:c * TQ, :] = _sortable(sc_ref[0:c * TQ, :])
    keys_ref[c * TQ:kv_len, :] = _sortable(jnp.where(key_i <= qry_i, sc_ref[c * TQ:kv_len, :], NEG_INF))

    def count(pred):
        return jnp.sum(pred(keys_ref[0:kv_len, :]).astype(jnp.int32), axis=0, keepdims=True)

    def search(it, thr):
        trial = thr ^ (jnp.int32(1) << (31 - it))
        return jnp.where(count(lambda kj: kj >= trial) >= n_sel, trial, thr)

    thr = lax.fori_loop(0, 32, search, jnp.full((1, TQ), jnp.iinfo(jnp.int32).min, jnp.int32))
    need = (n_sel - count(lambda kj: kj > thr)).astype(F32)
    earlier = (qry_i < key_i).astype(BF16)
    base = jnp.zeros((1, TQ), F32)
    for j in range(c + 1):
        kj = keys_ref[j * TQ:(j + 1) * TQ, :]
        eq = kj == thr
        eq_f = jnp.where(eq, 1.0, 0.0)
        rank = base + _dot(earlier, eq_f.astype(BF16))
        sel = (kj > thr) | (eq & (rank < need))
        mask_ref[:, j * TQ:(j + 1) * TQ] = jnp.where(sel, 0.0, NEG_INF).T.astype(mask_ref.dtype)
        base = base + jnp.sum(eq_f, axis=0, keepdims=True)
    s_len = mask_ref.shape[1]
    if kv_len < s_len:
        mask_ref[:, kv_len:] = jnp.full((TQ, s_len - kv_len), NEG_INF, mask_ref.dtype)


def _dsa_select_kernel(qi_ref, ki_ref, wi_ref, mask_ref, qsel_ref, kib_ref, wt_ref, sc_ref, keys_ref, *, n_sel, nq):
    i = pl.program_id(1)
    for c in range(nq):
        pl.when(i == c)(functools.partial(
            _dsa_select, c, qi_ref, ki_ref, wi_ref, qsel_ref, kib_ref, wt_ref, sc_ref, keys_ref, mask_ref, n_sel))


def _dsa_attend_kernel(tab_ref, q_ref, k_ref, v_ref, mask_ref, bias_ref, o_ref):
    g = pl.program_id(1)
    last = NUM_BUCKETS - 1
    far_bias = jnp.concatenate(
        [jnp.full((TQ, 1), tab_ref[last, 4 * g + hh] * LOG2_E, F32) for hh in range(4)], axis=0)

    def scores(c):
        rows = slice(c * TQ, (c + 1) * TQ)
        kv_len = (c + 1) * TQ
        s = _causal_scores(_stack_heads(q_ref[rows, :]), k_ref, kv_len, bias_ref, far_bias)
        return s + jnp.concatenate([mask_ref[rows, 0:kv_len].astype(F32)] * 4, axis=0)

    def finish(c, p):
        v_ext = _paired_values(v_ref[0:(c + 1) * TQ, :])
        outs = []
        for pr in range(2):
            r0 = 2 * pr * TQ
            p_both = jnp.concatenate([p[r0:r0 + TQ], p[r0 + TQ:r0 + 2 * TQ]], axis=1)
            res = _dot(p_both, v_ext)
            outs.append(res[:, :LANES] / res[:, LANES:])
        o_ref[c * TQ:(c + 1) * TQ, :] = jnp.concatenate(outs, axis=1).astype(o_ref.dtype)

    _softmax_pipeline(q_ref.shape[0] // TQ, scores, finish)


def dsa_attention(q, kv, qi, z_idx, table, bias_causal, batch):
    t = q.shape[0]
    s = t // batch
    nq = s // TQ
    n_sel = min(TOPK_MAX, s // 4)
    mask = pl.pallas_call(
        functools.partial(_dsa_select_kernel, n_sel=n_sel, nq=nq),
        grid=(batch, nq),
        in_specs=[
            pl.BlockSpec((TQ, IDX_HEADS * IDX_DIM), lambda b, i: (b * nq + i, 0)),
            pl.BlockSpec((s, LANES), lambda b, i: (b, 0)),
            pl.BlockSpec((TQ, LANES), lambda b, i: (b * nq + i, 0)),
        ],
        out_specs=pl.BlockSpec((TQ, s), lambda b, i: (b * nq + i, 0)),
        out_shape=jax.ShapeDtypeStruct((t, s), BF16),
        scratch_shapes=[
            pltpu.VMEM((IDX_HEADS, TQ, LANES), BF16),
            pltpu.VMEM((s, LANES), BF16),
            pltpu.VMEM((LANES, TQ), F32),
            pltpu.VMEM((s, TQ), F32),
            pltpu.VMEM((s, TQ), jnp.int32),
        ],
        compiler_params=_params("parallel", "parallel"),
        name="dsa_select",
    )(qi, z_idx, z_idx)
    return pl.pallas_call(
        _dsa_attend_kernel,
        grid=(batch, N_KV_HEADS),
        in_specs=[
            pl.BlockSpec(memory_space=pltpu.SMEM),
            pl.BlockSpec((s, 4 * HEAD_DIM), lambda b, g: (b, g)),
            pl.BlockSpec((s, LANES), lambda b, g: (b, g)),
            pl.BlockSpec((s, LANES), lambda b, g: (b, N_KV_HEADS + g)),
            pl.BlockSpec((s, s), lambda b, g: (b, 0)),
            pl.BlockSpec((4, TQ, 2 * TQ), lambda b, g: (g, 0, 0)),
        ],
        out_specs=pl.BlockSpec((s, 4 * HEAD_DIM), lambda b, g: (b, g)),
        out_shape=jax.ShapeDtypeStruct((t, N_HEADS * HEAD_DIM), BF16),
        compiler_params=_params("parallel", "parallel"),
        name="dsa_attend",
    )(table, q, kv, kv, mask, bias_causal)


def _lower_bound_kernel(x_ref, o_ref):
    x = x_ref[...]
    e = jnp.exp(x - jnp.max(x, axis=0, keepdims=True))
    soft = e / jnp.sum(e, axis=0, keepdims=True)
    run = soft[0:1]
    o_ref[0:1, :] = run - soft[0:1]
    for r in range(1, x.shape[0]):
        run = run + soft[r:r + 1]
        o_ref[r:r + 1, :] = run - soft[0:1]


def hgrn_lower_bounds(logits):
    return pl.pallas_call(
        _lower_bound_kernel,
        out_shape=jax.ShapeDtypeStruct(logits.shape, F32),
        name="hgrn_lower_bounds",
    )(logits)


def kernel(x, rel_bias_table, hgrn_lb_logits, norm_g, ffn_w_up, ffn_conv, ffn_w_down, swa_w_in, swa_w_out,
           swa_sinks, diff_w_in, diff_w_out, diff_lambda, diff_subln_g, hgrn_w_in, hgrn_w_out, hgrn_norm_g,
           dsa_w_in, dsa_w_out):
    batch, seq, d = x.shape
    t = batch * seq
    depth = norm_g.shape[0]
    n_mixers = 4
    bias_causal, bias_window = bias_tiles(rel_bias_table)
    lb_all = hgrn_lower_bounds(hgrn_lb_logits)
    kv = N_KV_HEADS * HEAD_DIM
    q_scale = HEAD_DIM ** -0.5 * LOG2_E

    x2 = x.reshape(t, d)
    h = None
    for i in range(depth):
        kind, j = i % n_mixers, i // n_mixers
        if kind == 0:
            w = swa_w_in[j]
            src, g0 = (x2, norm_g[i, 0]) if h is None else (h, None)
            q = matmul(src, w, BF16, n_cols=d, scale=q_scale, norm_g=g0)
            kvd = matmul(src, w, BF16, col0=d, n_cols=2 * kv, dup_heads=True, norm_g=g0)
            o = swa_attention(q, kvd, swa_sinks[j], bias_window, batch)
            w_out = swa_w_out
        elif kind == 1:
            w = diff_w_in[j]
            q = matmul(h, w, BF16, n_cols=d, scale=q_scale)
            kvp = matmul(h, w, BF16, col0=d)
            o = diff_attention(q, kvp, rel_bias_table, diff_lambda[j], diff_subln_g[j], bias_causal, batch, i)
            w_out = diff_w_out
        elif kind == 2:
            z = matmul(h, hgrn_w_in[j], F32)
            o = hgrn_recurrence(z, lb_all[i], hgrn_norm_g[j], batch)
            w_out = hgrn_w_out
        else:
            w = dsa_w_in[j]
            n_qi = IDX_HEADS * IDX_DIM
            q = matmul(h, w, BF16, n_cols=d, scale=q_scale)
            kvd = matmul(h, w, BF16, col0=d, n_cols=2 * kv, dup_heads=True)
            qi = matmul(h, w, BF16, col0=d + 2 * kv, n_cols=n_qi)
            w_idx = jnp.pad(w[:, d + 2 * kv + n_qi:], ((0, 0), (0, LANES - IDX_DIM - IDX_HEADS)))
            z_idx = matmul(h, w_idx, F32)
            o = dsa_attention(q, kvd, qi, z_idx, rel_bias_table, bias_causal, batch)
            w_out = dsa_w_out
        x2, h = proj_residual(o, w_out.astype(BF16), j, x2, norm_g[i, 1], norm_g[i, 2])
        a, w_down = ffn_up(h, ffn_w_up, ffn_conv, ffn_w_down, i, batch)
        x2, h = proj_residual(a, w_down, 0, x2, norm_g[i, 3], norm_g[(i + 1) % depth, 0])
    return x2.reshape(batch, seq, d)
```

```python
import functools
import math

import jax
import jax.numpy as jnp
import numpy as np
from jax import lax
from jax.experimental import pallas as pl
from jax.experimental.pallas import tpu as pltpu

D_MODEL = 2048
HEAD_DIM = 64
N_HEADS = 32
N_KV_HEADS = 8
WINDOW = 128
DIFF_HEADS = 16
HGRN_HEADS = 16
HGRN_HEAD_DIM = 128
HGRN_CHUNK = 64
IDX_HEADS = 16
IDX_DIM = 64
TOPK_MAX = 256
D_FF = 5632
CONV_WIDTH = 3
NUM_BUCKETS = 32
MAX_DISTANCE = 128
RMS_EPS = 1e-6

LANES = 128
TQ = 256
VMEM_LIMIT = 56 * 1024 * 1024
NEG_INF = float("-inf")
LOG2_E = math.log2(math.e)
BF16 = jnp.bfloat16
F32 = jnp.float32


def _params(*sem):
    return pltpu.CompilerParams(dimension_semantics=sem, vmem_limit_bytes=VMEM_LIMIT)


def _dot(a, b):
    return jnp.dot(a, b, preferred_element_type=F32)


def _dot_nt(a, b):
    return lax.dot_general(a, b, (((1,), (1,)), ((), ())), preferred_element_type=F32)


def _dot_tn(a, b):
    return lax.dot_general(a, b, (((0,), (0,)), ((), ())), preferred_element_type=F32)


def _rms(x, g):
    return x * lax.rsqrt(jnp.mean(x * x, axis=-1, keepdims=True) + RMS_EPS) * g


def _silu(x):
    return x * (1.0 / (1.0 + jnp.exp(-x)))


def _dup_heads(w):
    lo = lax.broadcasted_iota(jnp.int32, (w.shape[0], LANES), 1) < HEAD_DIM
    parts = []
    for p in range(w.shape[1] // LANES):
        blk = w[:, p * LANES:(p + 1) * LANES]
        swapped = pltpu.roll(blk, HEAD_DIM, 1)
        parts += [jnp.where(lo, blk, swapped), jnp.where(lo, swapped, blk)]
    return jnp.concatenate(parts, axis=1)


def _mm_kernel(x_ref, g_ref, w_ref, o_ref, wb_ref, *, scale, dup_heads, norm):
    @pl.when(pl.program_id(1) == 0)
    def _():
        w = w_ref[...]
        wb_ref[...] = (_dup_heads(w) if dup_heads else w).astype(BF16)

    x = _rms(x_ref[...], g_ref[...]).astype(BF16) if norm else x_ref[...]
    y = _dot(x, wb_ref[...])
    o_ref[...] = (y if scale == 1.0 else y * scale).astype(o_ref.dtype)


def matmul(x, w, out_dtype, *, col0=0, n_cols=None, scale=1.0, dup_heads=False, norm_g=None):
    t, k = x.shape
    n_cols = w.shape[1] - col0 if n_cols is None else n_cols
    tn_out = next(c for c in (1024, 512, 256, 128) if (n_cols * (2 if dup_heads else 1)) % c == 0)
    tn_in = tn_out // 2 if dup_heads else tn_out
    assert n_cols % tn_in == 0 and col0 % tn_in == 0, (n_cols, col0, tn_in)
    out_bytes = jnp.dtype(out_dtype).itemsize
    tm = next(r for r in (2048, 1024, 512, 256) if t % r == 0 and
              2 * r * k * x.dtype.itemsize + 2 * k * tn_in * 4 + k * tn_out * 2
              + 2 * r * tn_out * out_bytes + r * tn_out * 4 <= VMEM_LIMIT)
    n_out = n_cols * (2 if dup_heads else 1)
    norm = norm_g is not None
    g = norm_g.reshape(1, k) if norm else jnp.ones((1, k), F32)
    return pl.pallas_call(
        functools.partial(_mm_kernel, scale=scale, dup_heads=dup_heads, norm=norm),
        grid=(n_cols // tn_in, t // tm),
        in_specs=[pl.BlockSpec((tm, k), lambda j, i: (i, 0)),
                  pl.BlockSpec((1, k), lambda j, i: (0, 0)),
                  pl.BlockSpec((k, tn_in), lambda j, i: (0, col0 // tn_in + j))],
        out_specs=pl.BlockSpec((tm, tn_out), lambda j, i: (i, j)),
        out_shape=jax.ShapeDtypeStruct((t, n_out), out_dtype),
        scratch_shapes=[pltpu.VMEM((k, tn_out), BF16)],
        compiler_params=_params("parallel", "arbitrary"),
        name="in_proj",
    )(x, g, w)


PROJ_SUB_ROWS = 128


def _proj_res_kernel(a_ref, w_ref, x_ref, go_ref, gn_ref, xo_ref, h_ref):
    tm = a_ref.shape[0]
    sub = PROJ_SUB_ROWS if tm >= 4 * PROJ_SUB_ROWS else tm
    ys = [_dot(a_ref[r:r + sub, :], w_ref[...]) for r in range(0, tm, sub)]
    for n, r in enumerate(range(0, tm, sub)):
        xn = x_ref[r:r + sub, :] + _rms(ys[n], go_ref[...])
        xo_ref[r:r + sub, :] = xn
        h_ref[r:r + sub, :] = _rms(xn, gn_ref[...]).astype(h_ref.dtype)


def _proj_rows(k, d):
    for tm in (512, 256, 128):
        streamed = 2 * (tm * k * 2 + 2 * tm * d * 4 + tm * d * 2)
        temporaries = 3 * tm * d * 4
        if k * d * 2 + streamed + temporaries <= VMEM_LIMIT - (4 << 20):
            return tm
    raise ValueError("projection weight does not fit in VMEM")


def proj_residual(a, w, layer, x, g_out, g_next):
    t, k = a.shape
    d = w.shape[2]
    tm = _proj_rows(k, d)
    return pl.pallas_call(
        _proj_res_kernel,
        grid=(t // tm,),
        in_specs=[
            pl.BlockSpec((tm, k), lambda i: (i, 0)),
            pl.BlockSpec((None, k, d), lambda i: (layer, 0, 0), pipeline_mode=pl.Buffered(1)),
            pl.BlockSpec((tm, d), lambda i: (i, 0)),
            pl.BlockSpec((1, d), lambda i: (0, 0)),
            pl.BlockSpec((1, d), lambda i: (0, 0)),
        ],
        out_specs=[pl.BlockSpec((tm, d), lambda i: (i, 0)), pl.BlockSpec((tm, d), lambda i: (i, 0))],
        out_shape=[jax.ShapeDtypeStruct((t, d), F32), jax.ShapeDtypeStruct((t, d), BF16)],
        compiler_params=_params("parallel"),
        name="proj_residual",
    )(a, w, x, g_out.reshape(1, d), g_next.reshape(1, d))


FFN_ROWS = 256
FFN_COLS = 512


def _ffn_up_kernel(h_ref, wg_ref, wv_ref, cg_ref, cv_ref, wd_ref, o_ref, wdb_ref):
    wdb_ref[...] = wd_ref[...].astype(BF16)
    s = h_ref.shape[0]
    rows = min(FFN_ROWS, s)
    tn = o_ref.shape[1]
    row = lax.broadcasted_iota(jnp.int32, (rows, 1), 0)
    wg, wv = wg_ref[...].astype(BF16), wv_ref[...].astype(BF16)
    cg, cv = cg_ref[...], cv_ref[...]

    def conv(u, tail, c):
        u1 = jnp.where(row >= 1, pltpu.roll(u, 1, 0), tail[1:2])
        u2 = jnp.where(row >= 2, pltpu.roll(u, 2, 0), jnp.where(row == 1, tail[1:2], tail[0:1]))
        return (c[0:1] * u2 + c[1:2] * u1) + c[2:3] * u

    tail_g = tail_v = jnp.zeros((CONV_WIDTH - 1, tn), F32)
    for r in range(s // rows):
        hr = h_ref[r * rows:(r + 1) * rows, :]
        ug, uv = _dot(hr, wg), _dot(hr, wv)
        out = _silu(conv(ug, tail_g, cg)) * conv(uv, tail_v, cv)
        o_ref[r * rows:(r + 1) * rows, :] = out.astype(o_ref.dtype)
        tail_g, tail_v = ug[rows - 2:rows], uv[rows - 2:rows]


def ffn_up(h, w_up, conv_w, w_down, layer, batch):
    t, d = h.shape
    s = t // batch
    tn = FFN_COLS
    nj = D_FF // tn
    return pl.pallas_call(
        _ffn_up_kernel,
        grid=(nj, batch),
        in_specs=[
            pl.BlockSpec((s, d), lambda j, b: (b, 0)),
            pl.BlockSpec((None, d, tn), lambda j, b: (layer, 0, j)),
            pl.BlockSpec((None, d, tn), lambda j, b: (layer, 0, nj + j)),
            pl.BlockSpec((None, CONV_WIDTH, tn), lambda j, b: (layer, 0, j)),
            pl.BlockSpec((None, CONV_WIDTH, tn), lambda j, b: (layer, 0, nj + j)),
            pl.BlockSpec((None, tn, d), lambda j, b: (layer, j, 0)),
        ],
        out_specs=[pl.BlockSpec((s, tn), lambda j, b: (b, j)), pl.BlockSpec((None, tn, d), lambda j, b: (0, j, 0))],
        out_shape=[jax.ShapeDtypeStruct((t, D_FF), BF16), jax.ShapeDtypeStruct((1, D_FF, d), BF16)],
        compiler_params=_params("arbitrary", "arbitrary"),
        name="ffn_up",
    )(h, w_up, w_up, conv_w, conv_w, w_down)


def _bucket_thresholds():
    max_exact = NUM_BUCKETS // 2
    n = np.arange(1, 2 * MAX_DISTANCE, dtype=np.float64)
    large = max_exact + np.floor(np.log(n / max_exact) / math.log(MAX_DISTANCE / max_exact) * (NUM_BUCKETS - max_exact))
    bucket = np.where(n < max_exact, n, np.minimum(large, NUM_BUCKETS - 1)).astype(np.int64)
    return tuple(int(n[bucket >= b][0]) for b in range(max_exact + 1, NUM_BUCKETS))


BUCKET_THRESHOLDS = _bucket_thresholds()


BIAS_MAPS = 8


def _bias_tile_kernel(tab_ref, causal_ref, window_ref):
    m0 = pl.program_id(0) * BIAS_MAPS
    r = lax.broadcasted_iota(jnp.int32, (TQ, 2 * TQ), 0)
    c = lax.broadcasted_iota(jnp.int32, (TQ, 2 * TQ), 1)
    d = r - c + TQ
    n = jnp.maximum(d, 0)
    max_exact = NUM_BUCKETS // 2
    large = jnp.full_like(n, max_exact)
    for thr in BUCKET_THRESHOLDS:
        large = large + (n >= thr).astype(jnp.int32)
    bucket = jnp.where(n < max_exact, n, large)
    for mm in range(BIAS_MAPS):
        val = jnp.zeros((TQ, 2 * TQ), F32)
        for b in range(NUM_BUCKETS):
            val = jnp.where(bucket == b, tab_ref[b, m0 + mm] * LOG2_E, val)
        causal = jnp.where(d >= 0, val, NEG_INF)
        causal_ref[mm] = causal
        window_ref[mm] = jnp.where(d < WINDOW, causal, NEG_INF)


def bias_tiles(table):
    shape = jax.ShapeDtypeStruct((N_HEADS, TQ, 2 * TQ), F32)
    return pl.pallas_call(
        _bias_tile_kernel,
        grid=(N_HEADS // BIAS_MAPS,),
        in_specs=[pl.BlockSpec(memory_space=pltpu.SMEM)],
        out_specs=[pl.BlockSpec((BIAS_MAPS, TQ, 2 * TQ), lambda m: (m, 0, 0))] * 2,
        out_shape=[shape, shape],
        compiler_params=_params("parallel"),
        name="bias_tiles",
    )(table)


def _head_select(x, lo, first):
    zero = jnp.zeros_like(x)
    return jnp.where(lo, x, zero) if first else jnp.where(lo, zero, x)


def _stack_heads(q):
    rows = q.shape[0]
    lo = lax.broadcasted_iota(jnp.int32, (rows, LANES), 1) < HEAD_DIM
    parts = []
    for p in range(q.shape[1] // LANES):
        qp = q[:, p * LANES:(p + 1) * LANES]
        parts += [_head_select(qp, lo, True), _head_select(qp, lo, False)]
    return jnp.concatenate(parts, axis=0)


def _pair_selector(n_keys):
    r = lax.broadcasted_iota(jnp.int32, (2 * n_keys, LANES), 0)
    c = lax.broadcasted_iota(jnp.int32, (2 * n_keys, LANES), 1)
    return jnp.where((r < n_keys) == (c < HEAD_DIM), 1.0, 0.0).astype(BF16)


def _paired_values(vv):
    lo = lax.broadcasted_iota(jnp.int32, vv.shape, 1) < HEAD_DIM
    v_both = jnp.concatenate([_head_select(vv, lo, True), _head_select(vv, lo, False)], axis=0)
    return jnp.concatenate([v_both, _pair_selector(vv.shape[0])], axis=1)


SWA_BLOCK = WINDOW


def _swa_kernel(sink_ref, q_ref, kp_ref, ko_ref, vp_ref, vo_ref, bias_ref, o_ref):
    i = pl.program_id(1)
    blk = SWA_BLOCK
    col = lax.broadcasted_iota(jnp.int32, (1, 2 * blk), 1)
    no_prev = (col < blk) & (i == 0)
    lo = lax.broadcasted_iota(jnp.int32, (blk, LANES), 1) < HEAD_DIM

    def scores(g):
        stack = _stack_heads(q_ref[:, g * 4 * HEAD_DIM:(g + 1) * 4 * HEAD_DIM])
        cols = slice(g * LANES, (g + 1) * LANES)
        kk = jnp.concatenate([kp_ref[:, cols], ko_ref[:, cols]], axis=0)
        s = _dot_nt(stack, kk) + bias_ref[4 * g:4 * g + 4].reshape(4 * blk, 2 * blk)
        return jnp.where(no_prev, NEG_INF, s)

    def softmax(g, s):
        sink = jnp.concatenate([jnp.full((blk, 1), sink_ref[4 * g + hh] * LOG2_E, F32) for hh in range(4)], axis=0)
        m = jnp.maximum(jnp.max(s, axis=-1, keepdims=True), sink)
        return jnp.exp2(s - m).astype(BF16), jnp.exp2(sink - m)

    def finish(g, e, e_sink):
        cols = slice(g * LANES, (g + 1) * LANES)
        v_ext = _paired_values(jnp.concatenate([vp_ref[:, cols], vo_ref[:, cols]], axis=0))
        outs = []
        for p in range(2):
            r0 = 2 * p * blk
            e_both = jnp.concatenate([e[r0:r0 + blk], e[r0 + blk:r0 + 2 * blk]], axis=1)
            res = _dot(e_both, v_ext)
            den = res[:, LANES:] + jnp.where(lo, e_sink[r0:r0 + blk], e_sink[r0 + blk:r0 + 2 * blk])
            outs.append(res[:, :LANES] / den)
        o_ref[:, g * 4 * HEAD_DIM:(g + 1) * 4 * HEAD_DIM] = jnp.concatenate(outs, axis=1).astype(o_ref.dtype)

    s_next = scores(0)
    prev = None
    for g in range(N_KV_HEADS):
        s = s_next
        if g + 1 < N_KV_HEADS:
            s_next = scores(g + 1)
        cur = softmax(g, s)
        if prev is not None:
            finish(g - 1, *prev)
        prev = cur
    finish(N_KV_HEADS - 1, *prev)


def swa_attention(q, kv, sinks, bias_window, batch):
    t, dq = q.shape
    blk = SWA_BLOCK
    nq = t // batch // blk
    dk = 2 * N_KV_HEADS * HEAD_DIM

    def own(col):
        return lambda b, i: (b * nq + i, col)

    def prev(col):
        return lambda b, i: (b * nq + jnp.maximum(i - 1, 0), col)

    bias_spec = pl.BlockSpec((N_HEADS, blk, 2 * blk), lambda b, i: (0, TQ // blk - 1, TQ // blk - 1))
    return pl.pallas_call(
        _swa_kernel,
        grid=(batch, nq),
        in_specs=[
            pl.BlockSpec(memory_space=pltpu.SMEM),
            pl.BlockSpec((blk, dq), own(0)),
            pl.BlockSpec((blk, dk), prev(0)),
            pl.BlockSpec((blk, dk), own(0)),
            pl.BlockSpec((blk, dk), prev(1)),
            pl.BlockSpec((blk, dk), own(1)),
            bias_spec,
        ],
        out_specs=pl.BlockSpec((blk, dq), own(0)),
        out_shape=jax.ShapeDtypeStruct((t, dq), BF16),
        compiler_params=_params("parallel", "parallel"),
        name="swa_attention",
    )(sinks, q, kv, kv, kv, kv, bias_window)


def _causal_scores(stack, k_ref, kv_len, bias_ref, far_bias):
    rows = stack.shape[0]
    s = _dot_nt(stack, k_ref[0:kv_len, :])
    if kv_len == TQ:
        return s + bias_ref[:, :, TQ:2 * TQ].reshape(rows, TQ)
    near = bias_ref[...].reshape(rows, 2 * TQ)
    if kv_len == 2 * TQ:
        return s + near
    n_far = kv_len - 2 * TQ
    return jnp.concatenate([s[:, :n_far] + far_bias, s[:, n_far:] + near], axis=1)


def _softmax_pipeline(n_blocks, scores, finish):
    order = list(range(n_blocks - 1, -1, -1))
    s_next = scores(order[0])
    p_prev = None
    for n in range(n_blocks):
        s = s_next
        if n + 1 < n_blocks:
            s_next = scores(order[n + 1])
        p = jnp.exp2(s - jnp.max(s, axis=-1, keepdims=True)).astype(BF16)
        if p_prev is not None:
            finish(order[n - 1], p_prev)
        p_prev = p
    finish(order[-1], p_prev)


def _diff_kernel(tab_ref, lam_ref, g_ref, q_ref, k_ref, v_ref, bias_ref, o_ref, *, lam_init):
    h = pl.program_id(0)
    last = NUM_BUCKETS - 1
    far_bias = jnp.concatenate([jnp.full((TQ, 1), tab_ref[last, 2 * h] * LOG2_E, F32),
                                jnp.full((TQ, 1), tab_ref[last, 2 * h + 1] * LOG2_E, F32)], axis=0)
    lf = lam_ref[...]
    lam = (jnp.exp(jnp.sum(lf[0:1] * lf[1:2], axis=-1, keepdims=True))
           - jnp.exp(jnp.sum(lf[2:3] * lf[3:4], axis=-1, keepdims=True)) + lam_init)
    ones = jnp.ones((q_ref.shape[0], LANES), BF16)

    def scores(c):
        stack = _stack_heads(q_ref[c * TQ:(c + 1) * TQ, :])
        return _causal_scores(stack, k_ref, (c + 1) * TQ, bias_ref, far_bias)

    def finish(c, p):
        kv_len = (c + 1) * TQ
        v_ext = jnp.concatenate([v_ref[0:kv_len, :], ones[0:kv_len]], axis=1)
        res = _dot(p, v_ext)
        o12 = res[:, :LANES] / res[:, LANES:]
        o = o12[:TQ] - lam * o12[TQ:]
        o_ref[c * TQ:(c + 1) * TQ, :] = (_rms(o, g_ref[...]) * (1.0 - lam_init)).astype(o_ref.dtype)

    _softmax_pipeline(q_ref.shape[0] // TQ, scores, finish)


def diff_attention(q, kv, table, lambdas, subln_g, bias_causal, batch, layer_idx):
    t = q.shape[0]
    s = t // batch
    lam_init = 0.8 - 0.6 * math.exp(-0.3 * layer_idx)
    return pl.pallas_call(
        functools.partial(_diff_kernel, lam_init=lam_init),
        grid=(DIFF_HEADS, batch),
        in_specs=[
            pl.BlockSpec(memory_space=pltpu.SMEM),
            pl.BlockSpec((4, HEAD_DIM), lambda h, b: (0, 0)),
            pl.BlockSpec((1, LANES), lambda h, b: (0, 0)),
            pl.BlockSpec((s, LANES), lambda h, b: (b, h)),
            pl.BlockSpec((s, LANES), lambda h, b: (b, h)),
            pl.BlockSpec((s, LANES), lambda h, b: (b, DIFF_HEADS + h)),
            pl.BlockSpec((2, TQ, 2 * TQ), lambda h, b: (h, 0, 0)),
        ],
        out_specs=pl.BlockSpec((s, LANES), lambda h, b: (b, h)),
        out_shape=jax.ShapeDtypeStruct((t, DIFF_HEADS * 2 * HEAD_DIM), BF16),
        compiler_params=_params("parallel", "parallel"),
        name="diff_attention",
    )(table, lambdas, subln_g.reshape(1, LANES), q, kv, kv, bias_causal)


HGRN_LEVELS = tuple(2 ** e for e in range(int(math.log2(HGRN_CHUNK)) - 1, -1, -1))
HGRN_GROUP = 256


def _hgrn_kernel(lb_ref, g_ref, q_ref, f_ref, i_ref, og_ref, o_ref,
                 qf_ref, kf_ref, bq_ref, kd_ref, dec_ref, v_ref):
    s_len, hd = q_ref.shape
    c_len = HGRN_CHUNK
    n_lvl = len(HGRN_LEVELS)
    grp = min(HGRN_GROUP, s_len)
    rc = lax.broadcasted_iota(jnp.int32, (grp, hd), 0) & (c_len - 1)
    lb = lb_ref[...]
    log_lb, log_1m_lb = jnp.log(lb), jnp.log1p(-lb)

    def prepare(gi):
        rows = slice(gi * grp, (gi + 1) * grp)
        z = f_ref[rows, :]
        u = jnp.exp(-jnp.abs(z))
        w = 1.0 + u
        log_sig = jnp.minimum(z, 0.0) - jnp.log(w)
        c = log_1m_lb + log_sig
        delta = log_lb - c
        log_f = jnp.where(jnp.isnan(delta), log_lb + c,
                          jnp.maximum(log_lb, c) + jnp.log(1.0 + jnp.exp(-jnp.abs(delta))))
        inv_w = 1.0 / w
        key = (1.0 - lb) * jnp.where(z >= 0.0, u * inv_w, inv_w)
        qs = _silu(q_ref[rows, :])
        v_ref[rows, :] = i_ref[rows, :].astype(BF16)

        b = log_f * LOG2_E
        for d in (1, 2, 4, 8, 16, 32):
            b = b + jnp.where(rc >= d, pltpu.roll(b, d, 0), 0.0)

        bq_ref[rows, :] = (qs * jnp.exp2(b)).astype(BF16)
        qf_ref[n_lvl, rows, :] = qs.astype(BF16)
        kf_ref[n_lvl, rows, :] = key.astype(BF16)

        r_m = jnp.where(rc >= 1, pltpu.roll(b, 1, 0), 0.0)
        for m in (1, 2, 4, 8, 16, 32):
            if m > 1:
                half = m // 2
                r_m = jnp.where((rc & half) != 0, pltpu.roll(r_m, half, 0), r_m)
            e_m = pltpu.roll(r_m, grp - m, 0)
            lvl = HGRN_LEVELS.index(m)
            qf_ref[lvl, rows, :] = (qs * jnp.exp2(b - r_m)).astype(BF16)
            kf_ref[lvl, rows, :] = (key * jnp.exp2(jnp.minimum(e_m - b, 0.0))).astype(BF16)
        b3 = b.reshape(grp // c_len, c_len, hd)
        b_end = jnp.broadcast_to(b3[:, c_len - 1:c_len, :], b3.shape).reshape(grp, hd)
        kd_ref[rows, :] = (key * jnp.exp2(b_end - b)).astype(BF16)
        dec_ref[rows, :] = jnp.exp2(b_end)

    ti = lax.broadcasted_iota(jnp.int32, (c_len, c_len), 0)
    si = lax.broadcasted_iota(jnp.int32, (c_len, c_len), 1)
    masks = [((ti // m) % 2 == 1) & (si // m == ti // m - 1) for m in HGRN_LEVELS] + [ti == si]
    g = g_ref[...]

    def chunk(ci, st):
        rows = slice(ci * c_len, (ci + 1) * c_len)
        att = jnp.zeros((c_len, c_len), F32)
        for lvl in range(n_lvl + 1):
            att = jnp.where(masks[lvl], _dot_nt(qf_ref[lvl, rows, :], kf_ref[lvl, rows, :]), att)
        vc = v_ref[rows, :]
        o = _dot(att.astype(BF16), vc) + _dot_nt(bq_ref[rows, :], st.astype(BF16))
        o_ref[rows, :] = (_rms(o, g) * _silu(og_ref[rows, :])).astype(o_ref.dtype)
        return st * dec_ref[ci * c_len:ci * c_len + 1, :] + _dot_tn(vc, kd_ref[rows, :])

    n_groups = s_len // grp
    per_group = grp // c_len
    st = jnp.zeros((hd, hd), F32)
    prepare(0)
    for gi in range(n_groups):
        if gi + 1 < n_groups:
            prepare(gi + 1)
        for ci in range(gi * per_group, (gi + 1) * per_group):
            st = chunk(ci, st)


def hgrn_recurrence(z, lb, norm_g, batch):
    t = z.shape[0]
    s = t // batch
    hd = HGRN_HEAD_DIM
    n_lvl = len(HGRN_LEVELS)

    def part(p):
        return pl.BlockSpec((s, hd), lambda b, h: (b, p * HGRN_HEADS + h))

    return pl.pallas_call(
        _hgrn_kernel,
        grid=(batch, HGRN_HEADS),
        in_specs=[pl.BlockSpec((1, hd), lambda b, h: (0, h)), pl.BlockSpec((1, hd), lambda b, h: (0, 0)),
                  part(0), part(1), part(2), part(3)],
        out_specs=pl.BlockSpec((s, hd), lambda b, h: (b, h)),
        out_shape=jax.ShapeDtypeStruct((t, HGRN_HEADS * hd), BF16),
        scratch_shapes=[
            pltpu.VMEM((n_lvl + 1, s, hd), BF16),
            pltpu.VMEM((n_lvl + 1, s, hd), BF16),
            pltpu.VMEM((s, hd), BF16),
            pltpu.VMEM((s, hd), BF16),
            pltpu.VMEM((s, hd), F32),
            pltpu.VMEM((s, hd), BF16),
        ],
        compiler_params=_params("parallel", "parallel"),
        name="hgrn_recurrence",
    )(lb.reshape(1, -1), norm_g.reshape(1, hd), z, z, z, z)


def _sortable(score):
    score = jnp.where(score == 0.0, 0.0, score)
    bits = lax.bitcast_convert_type(score, jnp.int32)
    return bits ^ ((bits >> 31) & jnp.int32(0x7FFFFFFF))


def _dsa_select(c, qi_ref, ki_ref, wi_ref, qsel_ref, kib_ref, wt_ref, sc_ref, keys_ref, mask_ref, n_sel):
    kv_len = (c + 1) * TQ
    lo = lax.broadcasted_iota(jnp.int32, (TQ, LANES), 1) < HEAD_DIM
    for p in range(IDX_HEADS // 2):
        qp = qi_ref[:, p * LANES:(p + 1) * LANES]
        qsel_ref[2 * p] = _head_select(qp, lo, True)
        qsel_ref[2 * p + 1] = _head_select(qp, lo, False)
    wt_ref[...] = wi_ref[...].T * (IDX_HEADS ** -0.5 * IDX_DIM ** -0.5)
    ki = ki_ref[0:kv_len, :]
    lo_k = lax.broadcasted_iota(jnp.int32, (kv_len, LANES), 1) < IDX_DIM
    kib_ref[0:kv_len, :] = jnp.where(lo_k, ki, pltpu.roll(ki, IDX_DIM, 1)).astype(BF16)
    sc_ref[0:kv_len, :] = jnp.zeros((kv_len, TQ), F32)

    def head_pair(hp, carry):
        kib = kib_ref[0:kv_len, :]
        acc = sc_ref[0:kv_len, :]
        for hh in (2 * hp, 2 * hp + 1):
            logit = _dot_nt(kib, qsel_ref[hh])
            acc = acc + jnp.maximum(logit, 0.0) * wt_ref[pl.ds(IDX_DIM + hh, 1), :]
        sc_ref[0:kv_len, :] = acc
        return carry

    lax.fori_loop(0, IDX_HEADS // 2, head_pair, 0)

    key_i = lax.broadcasted_iota(jnp.int32, (TQ, TQ), 0)
    qry_i = lax.broadcasted_iota(jnp.int32, (TQ, TQ), 1)
    if c > 0:
        keys_ref[0:c * TQ, :] = _sortable(sc_ref[0:c * TQ, :])
    keys_ref[c * TQ:kv_len, :] = _sortable(jnp.where(key_i <= qry_i, sc_ref[c * TQ:kv_len, :], NEG_INF))

    def count(pred):
        return jnp.sum(pred(keys_ref[0:kv_len, :]).astype(jnp.int32), axis=0, keepdims=True)

    def search(it, thr):
        trial = thr ^ (jnp.int32(1) << (31 - it))
        return jnp.where(count(lambda kj: kj >= trial) >= n_sel, trial, thr)

    thr = lax.fori_loop(0, 32, search, jnp.full((1, TQ), jnp.iinfo(jnp.int32).min, jnp.int32))
    need = (n_sel - count(lambda kj: kj > thr)).astype(F32)
    earlier = (qry_i < key_i).astype(BF16)
    base = jnp.zeros((1, TQ), F32)
    for j in range(c + 1):
        kj = keys_ref[j * TQ:(j + 1) * TQ, :]
        eq = kj == thr
        eq_f = jnp.where(eq, 1.0, 0.0)
        rank = base + _dot(earlier, eq_f.astype(BF16))
        sel = (kj > thr) | (eq & (rank < need))
        mask_ref[:, j * TQ:(j + 1) * TQ] = jnp.where(sel, 0.0, NEG_INF).T.astype(mask_ref.dtype)
        base = base + jnp.sum(eq_f, axis=0, keepdims=True)
    s_len = mask_ref.shape[1]
    if kv_len < s_len:
        mask_ref[:, kv_len:] = jnp.full((TQ, s_len - kv_len), NEG_INF, mask_ref.dtype)


def _dsa_select_kernel(qi_ref, ki_ref, wi_ref, mask_ref, qsel_ref, kib_ref, wt_ref, sc_ref, keys_ref, *, n_sel, nq):
    i = pl.program_id(1)
    for c in range(nq):
        pl.when(i == c)(functools.partial(
            _dsa_select, c, qi_ref, ki_ref, wi_ref, qsel_ref, kib_ref, wt_ref, sc_ref, keys_ref, mask_ref, n_sel))


def _dsa_attend_kernel(tab_ref, q_ref, k_ref, v_ref, mask_ref, bias_ref, o_ref):
    g = pl.program_id(1)
    last = NUM_BUCKETS - 1
    far_bias = jnp.concatenate(
        [jnp.full((TQ, 1), tab_ref[last, 4 * g + hh] * LOG2_E, F32) for hh in range(4)], axis=0)

    def scores(c):
        rows = slice(c * TQ, (c + 1) * TQ)
        kv_len = (c + 1) * TQ
        s = _causal_scores(_stack_heads(q_ref[rows, :]), k_ref, kv_len, bias_ref, far_bias)
        return s + jnp.concatenate([mask_ref[rows, 0:kv_len].astype(F32)] * 4, axis=0)

    def finish(c, p):
        v_ext = _paired_values(v_ref[0:(c + 1) * TQ, :])
        outs = []
        for pr in range(2):
            r0 = 2 * pr * TQ
            p_both = jnp.concatenate([p[r0:r0 + TQ], p[r0 + TQ:r0 + 2 * TQ]], axis=1)
            res = _dot(p_both, v_ext)
            outs.append(res[:, :LANES] / res[:, LANES:])
        o_ref[c * TQ:(c + 1) * TQ, :] = jnp.concatenate(outs, axis=1).astype(o_ref.dtype)

    _softmax_pipeline(q_ref.shape[0] // TQ, scores, finish)


def dsa_attention(q, kv, qi, z_idx, table, bias_causal, batch):
    t = q.shape[0]
    s = t // batch
    nq = s // TQ
    n_sel = min(TOPK_MAX, s // 4)
    mask = pl.pallas_call(
        functools.partial(_dsa_select_kernel, n_sel=n_sel, nq=nq),
        grid=(batch, nq),
        in_specs=[
            pl.BlockSpec((TQ, IDX_HEADS * IDX_DIM), lambda b, i: (b * nq + i, 0)),
            pl.BlockSpec((s, LANES), lambda b, i: (b, 0)),
            pl.BlockSpec((TQ, LANES), lambda b, i: (b * nq + i, 0)),
        ],
        out_specs=pl.BlockSpec((TQ, s), lambda b, i: (b * nq + i, 0)),
        out_shape=jax.ShapeDtypeStruct((t, s), BF16),
        scratch_shapes=[
            pltpu.VMEM((IDX_HEADS, TQ, LANES), BF16),
            pltpu.VMEM((s, LANES), BF16),
            pltpu.VMEM((LANES, TQ), F32),
            pltpu.VMEM((s, TQ), F32),
            pltpu.VMEM((s, TQ), jnp.int32),
        ],
        compiler_params=_params("parallel", "parallel"),
        name="dsa_select",
    )(qi, z_idx, z_idx)
    return pl.pallas_call(
        _dsa_attend_kernel,
        grid=(batch, N_KV_HEADS),
        in_specs=[
            pl.BlockSpec(memory_space=pltpu.SMEM),
            pl.BlockSpec((s, 4 * HEAD_DIM), lambda b, g: (b, g)),
            pl.BlockSpec((s, LANES), lambda b, g: (b, g)),
            pl.BlockSpec((s, LANES), lambda b, g: (b, N_KV_HEADS + g)),
            pl.BlockSpec((s, s), lambda b, g: (b, 0)),
            pl.BlockSpec((4, TQ, 2 * TQ), lambda b, g: (g, 0, 0)),
        ],
        out_specs=pl.BlockSpec((s, 4 * HEAD_DIM), lambda b, g: (b, g)),
        out_shape=jax.ShapeDtypeStruct((t, N_HEADS * HEAD_DIM), BF16),
        compiler_params=_params("parallel", "parallel"),
        name="dsa_attend",
    )(table, q, kv, kv, mask, bias_causal)


def _lower_bound_kernel(x_ref, o_ref):
    x = x_ref[...]
    e = jnp.exp(x - jnp.max(x, axis=0, keepdims=True))
    soft = e / jnp.sum(e, axis=0, keepdims=True)
    run = soft[0:1]
    o_ref[0:1, :] = run - soft[0:1]
    for r in range(1, x.shape[0]):
        run = run + soft[r:r + 1]
        o_ref[r:r + 1, :] = run - soft[0:1]


def hgrn_lower_bounds(logits):
    return pl.pallas_call(
        _lower_bound_kernel,
        out_shape=jax.ShapeDtypeStruct(logits.shape, F32),
        name="hgrn_lower_bounds",
    )(logits)


def kernel(x, rel_bias_table, hgrn_lb_logits, norm_g, ffn_w_up, ffn_conv, ffn_w_down, swa_w_in, swa_w_out,
           swa_sinks, diff_w_in, diff_w_out, diff_lambda, diff_subln_g, hgrn_w_in, hgrn_w_out, hgrn_norm_g,
           dsa_w_in, dsa_w_out):
    batch, seq, d = x.shape
    t = batch * seq
    depth = norm_g.shape[0]
    n_mixers = 4
    bias_causal, bias_window = bias_tiles(rel_bias_table)
    lb_all = hgrn_lower_bounds(hgrn_lb_logits)
    kv = N_KV_HEADS * HEAD_DIM
    q_scale = HEAD_DIM ** -0.5 * LOG2_E

    x2 = x.reshape(t, d)
    h = None
    for i in range(depth):
        kind, j = i % n_mixers, i // n_mixers
        if kind == 0:
            w = swa_w_in[j]
            src, g0 = (x2, norm_g[i, 0]) if h is None else (h, None)
            q = matmul(src, w, BF16, n_cols=d, scale=q_scale, norm_g=g0)
            kvd = matmul(src, w, BF16, col0=d, n_cols=2 * kv, dup_heads=True, norm_g=g0)
            o = swa_attention(q, kvd, swa_sinks[j], bias_window, batch)
            w_out = swa_w_out
        elif kind == 1:
            w = diff_w_in[j]
            q = matmul(h, w, BF16, n_cols=d, scale=q_scale)
            kvp = matmul(h, w, BF16, col0=d)
            o = diff_attention(q, kvp, rel_bias_table, diff_lambda[j], diff_subln_g[j], bias_causal, batch, i)
            w_out = diff_w_out
        elif kind == 2:
            z = matmul(h, hgrn_w_in[j], F32)
            o = hgrn_recurrence(z, lb_all[i], hgrn_norm_g[j], batch)
            w_out = hgrn_w_out
        else:
            w = dsa_w_in[j]
            n_qi = IDX_HEADS * IDX_DIM
            q = matmul(h, w, BF16, n_cols=d, scale=q_scale)
            kvd = matmul(h, w, BF16, col0=d, n_cols=2 * kv, dup_heads=True)
            qi = matmul(h, w, BF16, col0=d + 2 * kv, n_cols=n_qi)
            w_idx = jnp.pad(w[:, d + 2 * kv + n_qi:], ((0, 0), (0, LANES - IDX_DIM - IDX_HEADS)))
            z_idx = matmul(h, w_idx, F32)
            o = dsa_attention(q, kvd, qi, z_idx, rel_bias_table, bias_causal, batch)
            w_out = dsa_w_out
        x2, h = proj_residual(o, w_out.astype(BF16), j, x2, norm_g[i, 1], norm_g[i, 2])
        a, w_down = ffn_up(h, ffn_w_up, ffn_conv, ffn_w_down, i, batch)
        x2, h = proj_residual(a, w_down, 0, x2, norm_g[i, 3], norm_g[(i + 1) % depth, 0])
    return x2.reshape(batch, seq, d)
```

```python
import functools
import math

import jax
import jax.numpy as jnp
import numpy as np
from jax import lax
from jax.experimental import pallas as pl
from jax.experimental.pallas import tpu as pltpu

D_MODEL = 2048
HEAD_DIM = 64
N_HEADS = 32
N_KV_HEADS = 8
WINDOW = 128
DIFF_HEADS = 16
HGRN_HEADS = 16
HGRN_HEAD_DIM = 128
HGRN_CHUNK = 64
IDX_HEADS = 16
IDX_DIM = 64
TOPK_MAX = 256
D_FF = 5632
CONV_WIDTH = 3
NUM_BUCKETS = 32
MAX_DISTANCE = 128
RMS_EPS = 1e-6

LANES = 128
TQ = 256
VMEM_LIMIT = 56 * 1024 * 1024
NEG_INF = float("-inf")
LOG2_E = math.log2(math.e)
BF16 = jnp.bfloat16
F32 = jnp.float32


def _params(*sem):
    return pltpu.CompilerParams(dimension_semantics=sem, vmem_limit_bytes=VMEM_LIMIT)


def _dot(a, b):
    return jnp.dot(a, b, preferred_element_type=F32)


def _dot_nt(a, b):
    return lax.dot_general(a, b, (((1,), (1,)), ((), ())), preferred_element_type=F32)


def _dot_tn(a, b):
    return lax.dot_general(a, b, (((0,), (0,)), ((), ())), preferred_element_type=F32)


def _rms(x, g):
    return x * lax.rsqrt(jnp.mean(x * x, axis=-1, keepdims=True) + RMS_EPS) * g


def _silu(x):
    return x * (1.0 / (1.0 + jnp.exp(-x)))


def _dup_heads(w):
    lo = lax.broadcasted_iota(jnp.int32, (w.shape[0], LANES), 1) < HEAD_DIM
    parts = []
    for p in range(w.shape[1] // LANES):
        blk = w[:, p * LANES:(p + 1) * LANES]
        swapped = pltpu.roll(blk, HEAD_DIM, 1)
        parts += [jnp.where(lo, blk, swapped), jnp.where(lo, swapped, blk)]
    return jnp.concatenate(parts, axis=1)


def _mm_kernel(x_ref, g_ref, w_ref, o_ref, wb_ref, *, scale, dup_heads, norm):
    @pl.when(pl.program_id(1) == 0)
    def _():
        w = w_ref[...]
        wb_ref[...] = (_dup_heads(w) if dup_heads else w).astype(BF16)

    x = _rms(x_ref[...], g_ref[...]).astype(BF16) if norm else x_ref[...]
    y = _dot(x, wb_ref[...])
    o_ref[...] = (y if scale == 1.0 else y * scale).astype(o_ref.dtype)


def matmul(x, w, out_dtype, *, col0=0, n_cols=None, scale=1.0, dup_heads=False, norm_g=None):
    t, k = x.shape
    n_cols = w.shape[1] - col0 if n_cols is None else n_cols
    tm = 1024
    tn_out = next(c for c in (1024, 512, 256, 128) if (n_cols * (2 if dup_heads else 1)) % c == 0)
    tn_in = tn_out // 2 if dup_heads else tn_out
    assert n_cols % tn_in == 0 and col0 % tn_in == 0, (n_cols, col0, tn_in)
    n_out = n_cols * (2 if dup_heads else 1)
    norm = norm_g is not None
    g = norm_g.reshape(1, k) if norm else jnp.ones((1, k), F32)
    return pl.pallas_call(
        functools.partial(_mm_kernel, scale=scale, dup_heads=dup_heads, norm=norm),
        grid=(n_cols // tn_in, t // tm),
        in_specs=[pl.BlockSpec((tm, k), lambda j, i: (i, 0)),
                  pl.BlockSpec((1, k), lambda j, i: (0, 0)),
                  pl.BlockSpec((k, tn_in), lambda j, i: (0, col0 // tn_in + j))],
        out_specs=pl.BlockSpec((tm, tn_out), lambda j, i: (i, j)),
        out_shape=jax.ShapeDtypeStruct((t, n_out), out_dtype),
        scratch_shapes=[pltpu.VMEM((k, tn_out), BF16)],
        compiler_params=_params("parallel", "arbitrary"),
        name="in_proj",
    )(x, g, w)


PROJ_SUB_ROWS = 128


def _proj_res_kernel(a_ref, w_ref, x_ref, go_ref, gn_ref, xo_ref, h_ref):
    tm = a_ref.shape[0]
    sub = PROJ_SUB_ROWS if tm >= 4 * PROJ_SUB_ROWS else tm
    ys = [_dot(a_ref[r:r + sub, :], w_ref[...]) for r in range(0, tm, sub)]
    for n, r in enumerate(range(0, tm, sub)):
        xn = x_ref[r:r + sub, :] + _rms(ys[n], go_ref[...])
        xo_ref[r:r + sub, :] = xn
        h_ref[r:r + sub, :] = _rms(xn, gn_ref[...]).astype(h_ref.dtype)


def _proj_rows(k, d):
    for tm in (512, 256, 128):
        streamed = 2 * (tm * k * 2 + 2 * tm * d * 4 + tm * d * 2)
        temporaries = 3 * tm * d * 4
        if k * d * 2 + streamed + temporaries <= VMEM_LIMIT - (4 << 20):
            return tm
    raise ValueError("projection weight does not fit in VMEM")


def proj_residual(a, w, layer, x, g_out, g_next):
    t, k = a.shape
    d = w.shape[2]
    tm = _proj_rows(k, d)
    return pl.pallas_call(
        _proj_res_kernel,
        grid=(t // tm,),
        in_specs=[
            pl.BlockSpec((tm, k), lambda i: (i, 0)),
            pl.BlockSpec((None, k, d), lambda i: (layer, 0, 0), pipeline_mode=pl.Buffered(1)),
            pl.BlockSpec((tm, d), lambda i: (i, 0)),
            pl.BlockSpec((1, d), lambda i: (0, 0)),
            pl.BlockSpec((1, d), lambda i: (0, 0)),
        ],
        out_specs=[pl.BlockSpec((tm, d), lambda i: (i, 0)), pl.BlockSpec((tm, d), lambda i: (i, 0))],
        out_shape=[jax.ShapeDtypeStruct((t, d), F32), jax.ShapeDtypeStruct((t, d), BF16)],
        compiler_params=_params("parallel"),
        name="proj_residual",
    )(a, w, x, g_out.reshape(1, d), g_next.reshape(1, d))


FFN_ROWS = 512
FFN_COLS = 512


def _ffn_up_kernel(h_ref, wg_ref, wv_ref, cg_ref, cv_ref, wd_ref, o_ref, wdb_ref):
    wdb_ref[...] = wd_ref[...].astype(BF16)
    s = h_ref.shape[0]
    rows = min(FFN_ROWS, s)
    tn = o_ref.shape[1]
    row = lax.broadcasted_iota(jnp.int32, (rows, 1), 0)
    wg, wv = wg_ref[...].astype(BF16), wv_ref[...].astype(BF16)
    cg, cv = cg_ref[...], cv_ref[...]

    def conv(u, tail, c):
        u1 = jnp.where(row >= 1, pltpu.roll(u, 1, 0), tail[1:2])
        u2 = jnp.where(row >= 2, pltpu.roll(u, 2, 0), jnp.where(row == 1, tail[1:2], tail[0:1]))
        return (c[0:1] * u2 + c[1:2] * u1) + c[2:3] * u

    tail_g = tail_v = jnp.zeros((CONV_WIDTH - 1, tn), F32)
    for r in range(s // rows):
        hr = h_ref[r * rows:(r + 1) * rows, :]
        ug, uv = _dot(hr, wg), _dot(hr, wv)
        out = _silu(conv(ug, tail_g, cg)) * conv(uv, tail_v, cv)
        o_ref[r * rows:(r + 1) * rows, :] = out.astype(o_ref.dtype)
        tail_g, tail_v = ug[rows - 2:rows], uv[rows - 2:rows]


def ffn_up(h, w_up, conv_w, w_down, layer, batch):
    t, d = h.shape
    s = t // batch
    tn = FFN_COLS
    nj = D_FF // tn
    return pl.pallas_call(
        _ffn_up_kernel,
        grid=(nj, batch),
        in_specs=[
            pl.BlockSpec((s, d), lambda j, b: (b, 0)),
            pl.BlockSpec((None, d, tn), lambda j, b: (layer, 0, j)),
            pl.BlockSpec((None, d, tn), lambda j, b: (layer, 0, nj + j)),
            pl.BlockSpec((None, CONV_WIDTH, tn), lambda j, b: (layer, 0, j)),
            pl.BlockSpec((None, CONV_WIDTH, tn), lambda j, b: (layer, 0, nj + j)),
            pl.BlockSpec((None, tn, d), lambda j, b: (layer, j, 0)),
        ],
        out_specs=[pl.BlockSpec((s, tn), lambda j, b: (b, j)), pl.BlockSpec((None, tn, d), lambda j, b: (0, j, 0))],
        out_shape=[jax.ShapeDtypeStruct((t, D_FF), BF16), jax.ShapeDtypeStruct((1, D_FF, d), BF16)],
        compiler_params=_params("arbitrary", "arbitrary"),
        name="ffn_up",
    )(h, w_up, w_up, conv_w, conv_w, w_down)


def _bucket_thresholds():
    max_exact = NUM_BUCKETS // 2
    n = np.arange(1, 2 * MAX_DISTANCE, dtype=np.float64)
    large = max_exact + np.floor(np.log(n / max_exact) / math.log(MAX_DISTANCE / max_exact) * (NUM_BUCKETS - max_exact))
    bucket = np.where(n < max_exact, n, np.minimum(large, NUM_BUCKETS - 1)).astype(np.int64)
    return tuple(int(n[bucket >= b][0]) for b in range(max_exact + 1, NUM_BUCKETS))


BUCKET_THRESHOLDS = _bucket_thresholds()


BIAS_MAPS = 8


def _bias_tile_kernel(tab_ref, causal_ref, window_ref):
    m0 = pl.program_id(0) * BIAS_MAPS
    r = lax.broadcasted_iota(jnp.int32, (TQ, 2 * TQ), 0)
    c = lax.broadcasted_iota(jnp.int32, (TQ, 2 * TQ), 1)
    d = r - c + TQ
    n = jnp.maximum(d, 0)
    max_exact = NUM_BUCKETS // 2
    large = jnp.full_like(n, max_exact)
    for thr in BUCKET_THRESHOLDS:
        large = large + (n >= thr).astype(jnp.int32)
    bucket = jnp.where(n < max_exact, n, large)
    for mm in range(BIAS_MAPS):
        val = jnp.zeros((TQ, 2 * TQ), F32)
        for b in range(NUM_BUCKETS):
            val = jnp.where(bucket == b, tab_ref[b, m0 + mm] * LOG2_E, val)
        causal = jnp.where(d >= 0, val, NEG_INF)
        causal_ref[mm] = causal
        window_ref[mm] = jnp.where(d < WINDOW, causal, NEG_INF)


def bias_tiles(table):
    shape = jax.ShapeDtypeStruct((N_HEADS, TQ, 2 * TQ), F32)
    return pl.pallas_call(
        _bias_tile_kernel,
        grid=(N_HEADS // BIAS_MAPS,),
        in_specs=[pl.BlockSpec(memory_space=pltpu.SMEM)],
        out_specs=[pl.BlockSpec((BIAS_MAPS, TQ, 2 * TQ), lambda m: (m, 0, 0))] * 2,
        out_shape=[shape, shape],
        compiler_params=_params("parallel"),
        name="bias_tiles",
    )(table)


def _head_select(x, lo, first):
    zero = jnp.zeros_like(x)
    return jnp.where(lo, x, zero) if first else jnp.where(lo, zero, x)


def _dup_head(blk, second):
    x = blk.astype(F32)
    lo = lax.broadcasted_iota(jnp.int32, x.shape, 1) < HEAD_DIM
    return jnp.where(jnp.logical_xor(lo, second), x, pltpu.roll(x, HEAD_DIM, 1)).astype(BF16)


def _stack_heads(q):
    rows = q.shape[0]
    lo = lax.broadcasted_iota(jnp.int32, (rows, LANES), 1) < HEAD_DIM
    parts = []
    for p in range(q.shape[1] // LANES):
        qp = q[:, p * LANES:(p + 1) * LANES]
        parts += [_head_select(qp, lo, True), _head_select(qp, lo, False)]
    return jnp.concatenate(parts, axis=0)


def _pair_selector(n_keys):
    r = lax.broadcasted_iota(jnp.int32, (2 * n_keys, LANES), 0)
    c = lax.broadcasted_iota(jnp.int32, (2 * n_keys, LANES), 1)
    return jnp.where((r < n_keys) == (c < HEAD_DIM), 1.0, 0.0).astype(BF16)


def _paired_values(vv):
    lo = lax.broadcasted_iota(jnp.int32, vv.shape, 1) < HEAD_DIM
    v_both = jnp.concatenate([_head_select(vv, lo, True), _head_select(vv, lo, False)], axis=0)
    return jnp.concatenate([v_both, _pair_selector(vv.shape[0])], axis=1)


SWA_BLOCK = WINDOW


def _swa_kernel(sink_ref, q_ref, kp_ref, ko_ref, vp_ref, vo_ref, bias_ref, o_ref):
    i = pl.program_id(1)
    blk = SWA_BLOCK
    col = lax.broadcasted_iota(jnp.int32, (1, 2 * blk), 1)
    no_prev = (col < blk) & (i == 0)
    lo = lax.broadcasted_iota(jnp.int32, (blk, LANES), 1) < HEAD_DIM

    def scores(g):
        stack = _stack_heads(q_ref[:, g * 4 * HEAD_DIM:(g + 1) * 4 * HEAD_DIM])
        cols = slice(g * LANES, (g + 1) * LANES)
        kk = jnp.concatenate([kp_ref[:, cols], ko_ref[:, cols]], axis=0)
        s = _dot_nt(stack, kk) + bias_ref[4 * g:4 * g + 4].reshape(4 * blk, 2 * blk)
        return jnp.where(no_prev, NEG_INF, s)

    def softmax(g, s):
        sink = jnp.concatenate([jnp.full((blk, 1), sink_ref[4 * g + hh] * LOG2_E, F32) for hh in range(4)], axis=0)
        m = jnp.maximum(jnp.max(s, axis=-1, keepdims=True), sink)
        return jnp.exp2(s - m).astype(BF16), jnp.exp2(sink - m)

    def finish(g, e, e_sink):
        cols = slice(g * LANES, (g + 1) * LANES)
        v_ext = _paired_values(jnp.concatenate([vp_ref[:, cols], vo_ref[:, cols]], axis=0))
        outs = []
        for p in range(2):
            r0 = 2 * p * blk
            e_both = jnp.concatenate([e[r0:r0 + blk], e[r0 + blk:r0 + 2 * blk]], axis=1)
            res = _dot(e_both, v_ext)
            den = res[:, LANES:] + jnp.where(lo, e_sink[r0:r0 + blk], e_sink[r0 + blk:r0 + 2 * blk])
            outs.append(res[:, :LANES] / den)
        o_ref[:, g * 4 * HEAD_DIM:(g + 1) * 4 * HEAD_DIM] = jnp.concatenate(outs, axis=1).astype(o_ref.dtype)

    s_next = scores(0)
    prev = None
    for g in range(N_KV_HEADS):
        s = s_next
        if g + 1 < N_KV_HEADS:
            s_next = scores(g + 1)
        cur = softmax(g, s)
        if prev is not None:
            finish(g - 1, *prev)
        prev = cur
    finish(N_KV_HEADS - 1, *prev)


def swa_attention(q, kv, sinks, bias_window, batch):
    t, dq = q.shape
    blk = SWA_BLOCK
    nq = t // batch // blk
    dk = 2 * N_KV_HEADS * HEAD_DIM

    def own(col):
        return lambda b, i: (b * nq + i, col)

    def prev(col):
        return lambda b, i: (b * nq + jnp.maximum(i - 1, 0), col)

    bias_spec = pl.BlockSpec((N_HEADS, blk, 2 * blk), lambda b, i: (0, TQ // blk - 1, TQ // blk - 1))
    return pl.pallas_call(
        _swa_kernel,
        grid=(batch, nq),
        in_specs=[
            pl.BlockSpec(memory_space=pltpu.SMEM),
            pl.BlockSpec((blk, dq), own(0)),
            pl.BlockSpec((blk, dk), prev(0)),
            pl.BlockSpec((blk, dk), own(0)),
            pl.BlockSpec((blk, dk), prev(1)),
            pl.BlockSpec((blk, dk), own(1)),
            bias_spec,
        ],
        out_specs=pl.BlockSpec((blk, dq), own(0)),
        out_shape=jax.ShapeDtypeStruct((t, dq), BF16),
        compiler_params=_params("parallel", "parallel"),
        name="swa_attention",
    )(sinks, q, kv, kv, kv, kv, bias_window)


def _causal_scores(stack, k_ref, kv_len, bias_ref, far_bias):
    rows = stack.shape[0]
    s = _dot_nt(stack, k_ref[0:kv_len, :])
    if kv_len == TQ:
        return s + bias_ref[:, :, TQ:2 * TQ].reshape(rows, TQ)
    near = bias_ref[...].reshape(rows, 2 * TQ)
    if kv_len == 2 * TQ:
        return s + near
    n_far = kv_len - 2 * TQ
    return jnp.concatenate([s[:, :n_far] + far_bias, s[:, n_far:] + near], axis=1)


def _softmax_pipeline(n_blocks, scores, finish):
    order = list(range(n_blocks - 1, -1, -1))
    s_next = scores(order[0])
    p_prev = None
    for n in range(n_blocks):
        s = s_next
        if n + 1 < n_blocks:
            s_next = scores(order[n + 1])
        p = jnp.exp2(s - jnp.max(s, axis=-1, keepdims=True)).astype(BF16)
        if p_prev is not None:
            finish(order[n - 1], p_prev)
        p_prev = p
    finish(order[-1], p_prev)


def _diff_kernel(tab_ref, lam_ref, g_ref, q_ref, k_ref, v_ref, bias_ref, o_ref, *, lam_init):
    h = pl.program_id(0)
    last = NUM_BUCKETS - 1
    far_bias = jnp.concatenate([jnp.full((TQ, 1), tab_ref[last, 2 * h] * LOG2_E, F32),
                                jnp.full((TQ, 1), tab_ref[last, 2 * h + 1] * LOG2_E, F32)], axis=0)
    lf = lam_ref[...]
    lam = (jnp.exp(jnp.sum(lf[0:1] * lf[1:2], axis=-1, keepdims=True))
           - jnp.exp(jnp.sum(lf[2:3] * lf[3:4], axis=-1, keepdims=True)) + lam_init)
    ones = jnp.ones((q_ref.shape[0], LANES), BF16)

    def scores(c):
        stack = _stack_heads(q_ref[c * TQ:(c + 1) * TQ, :])
        return _causal_scores(stack, k_ref, (c + 1) * TQ, bias_ref, far_bias)

    def finish(c, p):
        kv_len = (c + 1) * TQ
        v_ext = jnp.concatenate([v_ref[0:kv_len, :], ones[0:kv_len]], axis=1)
        res = _dot(p, v_ext)
        o12 = res[:, :LANES] / res[:, LANES:]
        o = o12[:TQ] - lam * o12[TQ:]
        o_ref[c * TQ:(c + 1) * TQ, :] = (_rms(o, g_ref[...]) * (1.0 - lam_init)).astype(o_ref.dtype)

    _softmax_pipeline(q_ref.shape[0] // TQ, scores, finish)


def diff_attention(q, kv, table, lambdas, subln_g, bias_causal, batch, layer_idx):
    t = q.shape[0]
    s = t // batch
    lam_init = 0.8 - 0.6 * math.exp(-0.3 * layer_idx)
    return pl.pallas_call(
        functools.partial(_diff_kernel, lam_init=lam_init),
        grid=(DIFF_HEADS, batch),
        in_specs=[
            pl.BlockSpec(memory_space=pltpu.SMEM),
            pl.BlockSpec((4, HEAD_DIM), lambda h, b: (0, 0)),
            pl.BlockSpec((1, LANES), lambda h, b: (0, 0)),
            pl.BlockSpec((s, LANES), lambda h, b: (b, h)),
            pl.BlockSpec((s, LANES), lambda h, b: (b, h)),
            pl.BlockSpec((s, LANES), lambda h, b: (b, DIFF_HEADS + h)),
            pl.BlockSpec((2, TQ, 2 * TQ), lambda h, b: (h, 0, 0)),
        ],
        out_specs=pl.BlockSpec((s, LANES), lambda h, b: (b, h)),
        out_shape=jax.ShapeDtypeStruct((t, DIFF_HEADS * 2 * HEAD_DIM), BF16),
        compiler_params=_params("parallel", "parallel"),
        name="diff_attention",
    )(table, lambdas, subln_g.reshape(1, LANES), q, kv, kv, bias_causal)


HGRN_LEVELS = tuple(2 ** e for e in range(int(math.log2(HGRN_CHUNK)) - 1, -1, -1))
HGRN_GROUP = 256


def _hgrn_kernel(lb_ref, g_ref, q_ref, f_ref, i_ref, og_ref, o_ref,
                 qf_ref, kf_ref, bq_ref, kd_ref, dec_ref, v_ref):
    s_len, hd = q_ref.shape
    c_len = HGRN_CHUNK
    n_lvl = len(HGRN_LEVELS)
    grp = min(HGRN_GROUP, s_len)
    rc = lax.broadcasted_iota(jnp.int32, (grp, hd), 0) & (c_len - 1)
    lb = lb_ref[...]
    log_lb, log_1m_lb = jnp.log(lb), jnp.log1p(-lb)

    def prepare(gi):
        rows = slice(gi * grp, (gi + 1) * grp)
        z = f_ref[rows, :]
        u = jnp.exp(-jnp.abs(z))
        w = 1.0 + u
        log_sig = jnp.minimum(z, 0.0) - jnp.log(w)
        c = log_1m_lb + log_sig
        delta = log_lb - c
        log_f = jnp.where(jnp.isnan(delta), log_lb + c,
                          jnp.maximum(log_lb, c) + jnp.log(1.0 + jnp.exp(-jnp.abs(delta))))
        inv_w = 1.0 / w
        key = (1.0 - lb) * jnp.where(z >= 0.0, u * inv_w, inv_w)
        qs = _silu(q_ref[rows, :])
        v_ref[rows, :] = i_ref[rows, :].astype(BF16)

        b = log_f * LOG2_E
        for d in (1, 2, 4, 8, 16, 32):
            b = b + jnp.where(rc >= d, pltpu.roll(b, d, 0), 0.0)

        bq_ref[rows, :] = (qs * jnp.exp2(b)).astype(BF16)
        qf_ref[n_lvl, rows, :] = qs.astype(BF16)
        kf_ref[n_lvl, rows, :] = key.astype(BF16)

        r_m = jnp.where(rc >= 1, pltpu.roll(b, 1, 0), 0.0)
        for m in (1, 2, 4, 8, 16, 32):
            if m > 1:
                half = m // 2
                r_m = jnp.where((rc & half) != 0, pltpu.roll(r_m, half, 0), r_m)
            e_m = pltpu.roll(r_m, grp - m, 0)
            lvl = HGRN_LEVELS.index(m)
            qf_ref[lvl, rows, :] = (qs * jnp.exp2(b - r_m)).astype(BF16)
            kf_ref[lvl, rows, :] = (key * jnp.exp2(jnp.minimum(e_m - b, 0.0))).astype(BF16)
        b3 = b.reshape(grp // c_len, c_len, hd)
        b_end = jnp.broadcast_to(b3[:, c_len - 1:c_len, :], b3.shape).reshape(grp, hd)
        kd_ref[rows, :] = (key * jnp.exp2(b_end - b)).astype(BF16)
        dec_ref[rows, :] = jnp.exp2(b_end)

    ti = lax.broadcasted_iota(jnp.int32, (c_len, c_len), 0)
    si = lax.broadcasted_iota(jnp.int32, (c_len, c_len), 1)
    masks = [((ti // m) % 2 == 1) & (si // m == ti // m - 1) for m in HGRN_LEVELS] + [ti == si]
    g = g_ref[...]

    def chunk(ci, st):
        rows = slice(ci * c_len, (ci + 1) * c_len)
        att = jnp.zeros((c_len, c_len), F32)
        for lvl in range(n_lvl + 1):
            att = jnp.where(masks[lvl], _dot_nt(qf_ref[lvl, rows, :], kf_ref[lvl, rows, :]), att)
        vc = v_ref[rows, :]
        o = _dot(att.astype(BF16), vc) + _dot_nt(bq_ref[rows, :], st.astype(BF16))
        o_ref[rows, :] = (_rms(o, g) * _silu(og_ref[rows, :])).astype(o_ref.dtype)
        return st * dec_ref[ci * c_len:ci * c_len + 1, :] + _dot_tn(vc, kd_ref[rows, :])

    n_groups = s_len // grp
    per_group = grp // c_len
    st = jnp.zeros((hd, hd), F32)
    prepare(0)
    for gi in range(n_groups):
        if gi + 1 < n_groups:
            prepare(gi + 1)
        for ci in range(gi * per_group, (gi + 1) * per_group):
            st = chunk(ci, st)


def hgrn_recurrence(z, lb, norm_g, batch):
    t = z.shape[0]
    s = t // batch
    hd = HGRN_HEAD_DIM
    n_lvl = len(HGRN_LEVELS)

    def part(p):
        return pl.BlockSpec((s, hd), lambda b, h: (b, p * HGRN_HEADS + h))

    return pl.pallas_call(
        _hgrn_kernel,
        grid=(batch, HGRN_HEADS),
        in_specs=[pl.BlockSpec((1, hd), lambda b, h: (0, h)), pl.BlockSpec((1, hd), lambda b, h: (0, 0)),
                  part(0), part(1), part(2), part(3)],
        out_specs=pl.BlockSpec((s, hd), lambda b, h: (b, h)),
        out_shape=jax.ShapeDtypeStruct((t, HGRN_HEADS * hd), BF16),
        scratch_shapes=[
            pltpu.VMEM((n_lvl + 1, s, hd), BF16),
            pltpu.VMEM((n_lvl + 1, s, hd), BF16),
            pltpu.VMEM((s, hd), BF16),
            pltpu.VMEM((s, hd), BF16),
            pltpu.VMEM((s, hd), F32),
            pltpu.VMEM((s, hd), BF16),
        ],
        compiler_params=_params("parallel", "parallel"),
        name="hgrn_recurrence",
    )(lb.reshape(1, -1), norm_g.reshape(1, hd), z, z, z, z)


def _sortable(score):
    score = jnp.where(score == 0.0, 0.0, score)
    bits = lax.bitcast_convert_type(score, jnp.int32)
    return bits ^ ((bits >> 31) & jnp.int32(0x7FFFFFFF))


def _dsa_select(c, qi_ref, ki_ref, wi_ref, qsel_ref, kib_ref, wt_ref, sc_ref, keys_ref, mask_ref, n_sel):
    kv_len = (c + 1) * TQ
    lo = lax.broadcasted_iota(jnp.int32, (TQ, LANES), 1) < HEAD_DIM
    for p in range(IDX_HEADS // 2):
        qp = qi_ref[:, p * LANES:(p + 1) * LANES]
        qsel_ref[2 * p] = _head_select(qp, lo, True)
        qsel_ref[2 * p + 1] = _head_select(qp, lo, False)
    wt_ref[...] = wi_ref[...].T * (IDX_HEADS ** -0.5 * IDX_DIM ** -0.5)
    ki = ki_ref[0:kv_len, :]
    lo_k = lax.broadcasted_iota(jnp.int32, (kv_len, LANES), 1) < IDX_DIM
    kib_ref[0:kv_len, :] = jnp.where(lo_k, ki, pltpu.roll(ki, IDX_DIM, 1)).astype(BF16)
    sc_ref[0:kv_len, :] = jnp.zeros((kv_len, TQ), F32)

    def head_pair(hp, carry):
        kib = kib_ref[0:kv_len, :]
        acc = sc_ref[0:kv_len, :]
        for hh in (2 * hp, 2 * hp + 1):
            logit = _dot_nt(kib, qsel_ref[hh])
            acc = acc + jnp.maximum(logit, 0.0) * wt_ref[pl.ds(IDX_DIM + hh, 1), :]
        sc_ref[0:kv_len, :] = acc
        return carry

    lax.fori_loop(0, IDX_HEADS // 2, head_pair, 0)

    key_i = lax.broadcasted_iota(jnp.int32, (TQ, TQ), 0)
    qry_i = lax.broadcasted_iota(jnp.int32, (TQ, TQ), 1)
    if c > 0:
        keys_ref[0:c * TQ, :] = _sortable(sc_ref[0:c * TQ, :])
    keys_ref[c * TQ:kv_len, :] = _sortable(jnp.where(key_i <= qry_i, sc_ref[c * TQ:kv_len, :], NEG_INF))

    def count(pred):
        return jnp.sum(pred(keys_ref[0:kv_len, :]).astype(jnp.int32), axis=0, keepdims=True)

    def search(it, thr):
        trial = thr ^ (jnp.int32(1) << (31 - it))
        return jnp.where(count(lambda kj: kj >= trial) >= n_sel, trial, thr)

    thr = lax.fori_loop(0, 32, search, jnp.full((1, TQ), jnp.iinfo(jnp.int32).min, jnp.int32))
    need = (n_sel - count(lambda kj: kj > thr)).astype(F32)
    earlier = (qry_i < key_i).astype(BF16)
    base = jnp.zeros((1, TQ), F32)
    for j in range(c + 1):
        kj = keys_ref[j * TQ:(j + 1) * TQ, :]
        eq = kj == thr
        eq_f = jnp.where(eq, 1.0, 0.0)
        rank = base + _dot(earlier, eq_f.astype(BF16))
        sel = (kj > thr) | (eq & (rank < need))
        mask_ref[:, j * TQ:(j + 1) * TQ] = jnp.where(sel, 0.0, NEG_INF).T.astype(mask_ref.dtype)
        base = base + jnp.sum(eq_f, axis=0, keepdims=True)
    s_len = mask_ref.shape[1]
    if kv_len < s_len:
        mask_ref[:, kv_len:] = jnp.full((TQ, s_len - kv_len), NEG_INF, mask_ref.dtype)


def _dsa_select_kernel(qi_ref, ki_ref, wi_ref, mask_ref, qsel_ref, kib_ref, wt_ref, sc_ref, keys_ref, *, n_sel, nq):
    i = pl.program_id(1)
    for c in range(nq):
        pl.when(i == c)(functools.partial(
            _dsa_select, c, qi_ref, ki_ref, wi_ref, qsel_ref, kib_ref, wt_ref, sc_ref, keys_ref, mask_ref, n_sel))


def _dsa_attend_kernel(tab_ref, q_ref, kpair_ref, vpair_ref, mask_ref, bias_ref, o_ref, k_ref, v_ref):
    g = pl.program_id(1)
    k_ref[...] = _dup_head(kpair_ref[...], g % 2 == 1)
    v_ref[...] = _dup_head(vpair_ref[...], g % 2 == 1)
    last = NUM_BUCKETS - 1
    far_bias = jnp.concatenate(
        [jnp.full((TQ, 1), tab_ref[last, 4 * g + hh] * LOG2_E, F32) for hh in range(4)], axis=0)

    def scores(c):
        rows = slice(c * TQ, (c + 1) * TQ)
        kv_len = (c + 1) * TQ
        s = _causal_scores(_stack_heads(q_ref[rows, :]), k_ref, kv_len, bias_ref, far_bias)
        return s + jnp.concatenate([mask_ref[rows, 0:kv_len].astype(F32)] * 4, axis=0)

    def finish(c, p):
        v_ext = _paired_values(v_ref[0:(c + 1) * TQ, :])
        outs = []
        for pr in range(2):
            r0 = 2 * pr * TQ
            p_both = jnp.concatenate([p[r0:r0 + TQ], p[r0 + TQ:r0 + 2 * TQ]], axis=1)
            res = _dot(p_both, v_ext)
            outs.append(res[:, :LANES] / res[:, LANES:])
        o_ref[c * TQ:(c + 1) * TQ, :] = jnp.concatenate(outs, axis=1).astype(o_ref.dtype)

    _softmax_pipeline(q_ref.shape[0] // TQ, scores, finish)


def dsa_attention(q, kv, qi, z_idx, table, bias_causal, batch):
    t = q.shape[0]
    s = t // batch
    nq = s // TQ
    n_sel = min(TOPK_MAX, s // 4)
    mask = pl.pallas_call(
        functools.partial(_dsa_select_kernel, n_sel=n_sel, nq=nq),
        grid=(batch, nq),
        in_specs=[
            pl.BlockSpec((TQ, IDX_HEADS * IDX_DIM), lambda b, i: (b * nq + i, 0)),
            pl.BlockSpec((s, LANES), lambda b, i: (b, 0)),
            pl.BlockSpec((TQ, LANES), lambda b, i: (b * nq + i, 0)),
        ],
        out_specs=pl.BlockSpec((TQ, s), lambda b, i: (b * nq + i, 0)),
        out_shape=jax.ShapeDtypeStruct((t, s), BF16),
        scratch_shapes=[
            pltpu.VMEM((IDX_HEADS, TQ, LANES), BF16),
            pltpu.VMEM((s, LANES), BF16),
            pltpu.VMEM((LANES, TQ), F32),
            pltpu.VMEM((s, TQ), F32),
            pltpu.VMEM((s, TQ), jnp.int32),
        ],
        compiler_params=_params("parallel", "parallel"),
        name="dsa_select",
    )(qi, z_idx, z_idx)
    return pl.pallas_call(
        _dsa_attend_kernel,
        grid=(batch, N_KV_HEADS),
        in_specs=[
            pl.BlockSpec(memory_space=pltpu.SMEM),
            pl.BlockSpec((s, 4 * HEAD_DIM), lambda b, g: (b, g)),
            pl.BlockSpec((s, LANES), lambda b, g: (b, g // 2)),
            pl.BlockSpec((s, LANES), lambda b, g: (b, N_KV_HEADS // 2 + g // 2)),
            pl.BlockSpec((s, s), lambda b, g: (b, 0)),
            pl.BlockSpec((4, TQ, 2 * TQ), lambda b, g: (g, 0, 0)),
        ],
        out_specs=pl.BlockSpec((s, 4 * HEAD_DIM), lambda b, g: (b, g)),
        out_shape=jax.ShapeDtypeStruct((t, N_HEADS * HEAD_DIM), BF16),
        scratch_shapes=[pltpu.VMEM((s, LANES), BF16), pltpu.VMEM((s, LANES), BF16)],
        compiler_params=_params("parallel", "parallel"),
        name="dsa_attend",
    )(table, q, kv, kv, mask, bias_causal)


def _lower_bound_kernel(x_ref, o_ref):
    x = x_ref[...]
    e = jnp.exp(x - jnp.max(x, axis=0, keepdims=True))
    soft = e / jnp.sum(e, axis=0, keepdims=True)
    run = soft[0:1]
    o_ref[0:1, :] = run - soft[0:1]
    for r in range(1, x.shape[0]):
        run = run + soft[r:r + 1]
        o_ref[r:r + 1, :] = run - soft[0:1]


def hgrn_lower_bounds(logits):
    return pl.pallas_call(
        _lower_bound_kernel,
        out_shape=jax.ShapeDtypeStruct(logits.shape, F32),
        name="hgrn_lower_bounds",
    )(logits)


def kernel(x, rel_bias_table, hgrn_lb_logits, norm_g, ffn_w_up, ffn_conv, ffn_w_down, swa_w_in, swa_w_out,
           swa_sinks, diff_w_in, diff_w_out, diff_lambda, diff_subln_g, hgrn_w_in, hgrn_w_out, hgrn_norm_g,
           dsa_w_in, dsa_w_out):
    batch, seq, d = x.shape
    t = batch * seq
    depth = norm_g.shape[0]
    n_mixers = 4
    bias_causal, bias_window = bias_tiles(rel_bias_table)
    lb_all = hgrn_lower_bounds(hgrn_lb_logits)
    kv = N_KV_HEADS * HEAD_DIM
    q_scale = HEAD_DIM ** -0.5 * LOG2_E

    x2 = x.reshape(t, d)
    h = None
    for i in range(depth):
        kind, j = i % n_mixers, i // n_mixers
        if kind == 0:
            w = swa_w_in[j]
            src, g0 = (x2, norm_g[i, 0]) if h is None else (h, None)
            q = matmul(src, w, BF16, n_cols=d, scale=q_scale, norm_g=g0)
            kvd = matmul(src, w, BF16, col0=d, n_cols=2 * kv, dup_heads=True, norm_g=g0)
            o = swa_attention(q, kvd, swa_sinks[j], bias_window, batch)
            w_out = swa_w_out
        elif kind == 1:
            w = diff_w_in[j]
            q = matmul(h, w, BF16, n_cols=d, scale=q_scale)
            kvp = matmul(h, w, BF16, col0=d)
            o = diff_attention(q, kvp, rel_bias_table, diff_lambda[j], diff_subln_g[j], bias_causal, batch, i)
            w_out = diff_w_out
        elif kind == 2:
            z = matmul(h, hgrn_w_in[j], F32)
            o = hgrn_recurrence(z, lb_all[i], hgrn_norm_g[j], batch)
            w_out = hgrn_w_out
        else:
            w = dsa_w_in[j]
            n_qi = IDX_HEADS * IDX_DIM
            q = matmul(h, w, BF16, n_cols=d, scale=q_scale)
            kvd = matmul(h, w, BF16, col0=d, n_cols=2 * kv)
            qi = matmul(h, w, BF16, col0=d + 2 * kv, n_cols=n_qi)
            w_idx = jnp.pad(w[:, d + 2 * kv + n_qi:], ((0, 0), (0, LANES - IDX_DIM - IDX_HEADS)))
            z_idx = matmul(h, w_idx, F32)
            o = dsa_attention(q, kvd, qi, z_idx, rel_bias_table, bias_causal, batch)
            w_out = dsa_w_out
        x2, h = proj_residual(o, w_out.astype(BF16), j, x2, norm_g[i, 1], norm_g[i, 2])
        a, w_down = ffn_up(h, ffn_w_up, ffn_conv, ffn_w_down, i, batch)
        x2, h = proj_residual(a, w_down, 0, x2, norm_g[i, 3], norm_g[(i + 1) % depth, 0])
    return x2.reshape(batch, seq, d)
```

```python
import functools
import math

import jax
import jax.numpy as jnp
import numpy as np
from jax import lax
from jax.experimental import pallas as pl
from jax.experimental.pallas import tpu as pltpu

D_MODEL = 2048
HEAD_DIM = 64
N_HEADS = 32
N_KV_HEADS = 8
WINDOW = 128
DIFF_HEADS = 16
HGRN_HEADS = 16
HGRN_HEAD_DIM = 128
HGRN_CHUNK = 64
IDX_HEADS = 16
IDX_DIM = 64
TOPK_MAX = 256
D_FF = 5632
CONV_WIDTH = 3
NUM_BUCKETS = 32
MAX_DISTANCE = 128
RMS_EPS = 1e-6

LANES = 128
TQ = 256
VMEM_LIMIT = 56 * 1024 * 1024
NEG_INF = float("-inf")
LOG2_E = math.log2(math.e)
BF16 = jnp.bfloat16
F32 = jnp.float32


def _params(*sem):
    return pltpu.CompilerParams(dimension_semantics=sem, vmem_limit_bytes=VMEM_LIMIT)


def _dot(a, b):
    return jnp.dot(a, b, preferred_element_type=F32)


def _dot_nt(a, b):
    return lax.dot_general(a, b, (((1,), (1,)), ((), ())), preferred_element_type=F32)


def _dot_tn(a, b):
    return lax.dot_general(a, b, (((0,), (0,)), ((), ())), preferred_element_type=F32)


def _rms(x, g):
    return x * lax.rsqrt(jnp.mean(x * x, axis=-1, keepdims=True) + RMS_EPS) * g


def _silu(x):
    return x * (1.0 / (1.0 + jnp.exp(-x)))


def _dup_heads(w):
    lo = lax.broadcasted_iota(jnp.int32, (w.shape[0], LANES), 1) < HEAD_DIM
    parts = []
    for p in range(w.shape[1] // LANES):
        blk = w[:, p * LANES:(p + 1) * LANES]
        swapped = pltpu.roll(blk, HEAD_DIM, 1)
        parts += [jnp.where(lo, blk, swapped), jnp.where(lo, swapped, blk)]
    return jnp.concatenate(parts, axis=1)


def _mm_kernel(x_ref, g_ref, w_ref, o_ref, wb_ref, *, scale, dup_heads, norm):
    @pl.when(pl.program_id(1) == 0)
    def _():
        w = w_ref[...]
        wb_ref[...] = (_dup_heads(w) if dup_heads else w).astype(BF16)

    x = _rms(x_ref[...], g_ref[...]).astype(BF16) if norm else x_ref[...]
    y = _dot(x, wb_ref[...])
    o_ref[...] = (y if scale == 1.0 else y * scale).astype(o_ref.dtype)


def matmul(x, w, out_dtype, *, col0=0, n_cols=None, scale=1.0, dup_heads=False, norm_g=None):
    t, k = x.shape
    n_cols = w.shape[1] - col0 if n_cols is None else n_cols
    tm = 1024
    tn_out = next(c for c in (1024, 512, 256, 128) if (n_cols * (2 if dup_heads else 1)) % c == 0)
    tn_in = tn_out // 2 if dup_heads else tn_out
    assert n_cols % tn_in == 0 and col0 % tn_in == 0, (n_cols, col0, tn_in)
    n_out = n_cols * (2 if dup_heads else 1)
    norm = norm_g is not None
    g = norm_g.reshape(1, k) if norm else jnp.ones((1, k), F32)
    return pl.pallas_call(
        functools.partial(_mm_kernel, scale=scale, dup_heads=dup_heads, norm=norm),
        grid=(n_cols // tn_in, t // tm),
        in_specs=[pl.BlockSpec((tm, k), lambda j, i: (i, 0)),
                  pl.BlockSpec((1, k), lambda j, i: (0, 0)),
                  pl.BlockSpec((k, tn_in), lambda j, i: (0, col0 // tn_in + j))],
        out_specs=pl.BlockSpec((tm, tn_out), lambda j, i: (i, j)),
        out_shape=jax.ShapeDtypeStruct((t, n_out), out_dtype),
        scratch_shapes=[pltpu.VMEM((k, tn_out), BF16)],
        compiler_params=_params("parallel", "arbitrary"),
        name="in_proj",
    )(x, g, w)


PROJ_SUB_ROWS = 128


def _proj_res_kernel(a_ref, w_ref, x_ref, go_ref, gn_ref, xo_ref, h_ref):
    tm = a_ref.shape[0]
    sub = PROJ_SUB_ROWS if tm >= 4 * PROJ_SUB_ROWS else tm
    ys = [_dot(a_ref[r:r + sub, :], w_ref[...]) for r in range(0, tm, sub)]
    for n, r in enumerate(range(0, tm, sub)):
        xn = x_ref[r:r + sub, :] + _rms(ys[n], go_ref[...])
        xo_ref[r:r + sub, :] = xn
        h_ref[r:r + sub, :] = _rms(xn, gn_ref[...]).astype(h_ref.dtype)


def _proj_rows(k, d):
    for tm in (512, 256, 128):
        streamed = 2 * (tm * k * 2 + 2 * tm * d * 4 + tm * d * 2)
        temporaries = 3 * tm * d * 4
        if k * d * 2 + streamed + temporaries <= VMEM_LIMIT - (4 << 20):
            return tm
    raise ValueError("projection weight does not fit in VMEM")


def proj_residual(a, w, layer, x, g_out, g_next):
    t, k = a.shape
    d = w.shape[2]
    tm = _proj_rows(k, d)
    return pl.pallas_call(
        _proj_res_kernel,
        grid=(t // tm,),
        in_specs=[
            pl.BlockSpec((tm, k), lambda i: (i, 0)),
            pl.BlockSpec((None, k, d), lambda i: (layer, 0, 0), pipeline_mode=pl.Buffered(1)),
            pl.BlockSpec((tm, d), lambda i: (i, 0)),
            pl.BlockSpec((1, d), lambda i: (0, 0)),
            pl.BlockSpec((1, d), lambda i: (0, 0)),
        ],
        out_specs=[pl.BlockSpec((tm, d), lambda i: (i, 0)), pl.BlockSpec((tm, d), lambda i: (i, 0))],
        out_shape=[jax.ShapeDtypeStruct((t, d), F32), jax.ShapeDtypeStruct((t, d), BF16)],
        compiler_params=_params("parallel"),
        name="proj_residual",
    )(a, w, x, g_out.reshape(1, d), g_next.reshape(1, d))


FFN_ROWS = 512
FFN_COLS = 512


def _ffn_up_kernel(h_ref, wg_ref, wv_ref, cg_ref, cv_ref, wd_ref, o_ref, wdb_ref):
    wdb_ref[...] = wd_ref[...].astype(BF16)
    s = h_ref.shape[0]
    rows = min(FFN_ROWS, s)
    tn = o_ref.shape[1]
    row = lax.broadcasted_iota(jnp.int32, (rows, 1), 0)
    wg, wv = wg_ref[...].astype(BF16), wv_ref[...].astype(BF16)
    cg, cv = cg_ref[...], cv_ref[...]

    def conv(u, tail, c):
        u1 = jnp.where(row >= 1, pltpu.roll(u, 1, 0), tail[1:2])
        u2 = jnp.where(row >= 2, pltpu.roll(u, 2, 0), jnp.where(row == 1, tail[1:2], tail[0:1]))
        return (c[0:1] * u2 + c[1:2] * u1) + c[2:3] * u

    tail_g = tail_v = jnp.zeros((CONV_WIDTH - 1, tn), F32)
    for r in range(s // rows):
        hr = h_ref[r * rows:(r + 1) * rows, :]
        ug, uv = _dot(hr, wg), _dot(hr, wv)
        out = _silu(conv(ug, tail_g, cg)) * conv(uv, tail_v, cv)
        o_ref[r * rows:(r + 1) * rows, :] = out.astype(o_ref.dtype)
        tail_g, tail_v = ug[rows - 2:rows], uv[rows - 2:rows]


def ffn_up(h, w_up, conv_w, w_down, layer, batch):
    t, d = h.shape
    s = t // batch
    tn = FFN_COLS
    nj = D_FF // tn
    return pl.pallas_call(
        _ffn_up_kernel,
        grid=(nj, batch),
        in_specs=[
            pl.BlockSpec((s, d), lambda j, b: (b, 0)),
            pl.BlockSpec((None, d, tn), lambda j, b: (layer, 0, j)),
            pl.BlockSpec((None, d, tn), lambda j, b: (layer, 0, nj + j)),
            pl.BlockSpec((None, CONV_WIDTH, tn), lambda j, b: (layer, 0, j)),
            pl.BlockSpec((None, CONV_WIDTH, tn), lambda j, b: (layer, 0, nj + j)),
            pl.BlockSpec((None, tn, d), lambda j, b: (layer, j, 0)),
        ],
        out_specs=[pl.BlockSpec((s, tn), lambda j, b: (b, j)), pl.BlockSpec((None, tn, d), lambda j, b: (0, j, 0))],
        out_shape=[jax.ShapeDtypeStruct((t, D_FF), BF16), jax.ShapeDtypeStruct((1, D_FF, d), BF16)],
        compiler_params=_params("arbitrary", "arbitrary"),
        name="ffn_up",
    )(h, w_up, w_up, conv_w, conv_w, w_down)


def _bucket_thresholds():
    max_exact = NUM_BUCKETS // 2
    n = np.arange(1, 2 * MAX_DISTANCE, dtype=np.float64)
    large = max_exact + np.floor(np.log(n / max_exact) / math.log(MAX_DISTANCE / max_exact) * (NUM_BUCKETS - max_exact))
    bucket = np.where(n < max_exact, n, np.minimum(large, NUM_BUCKETS - 1)).astype(np.int64)
    return tuple(int(n[bucket >= b][0]) for b in range(max_exact + 1, NUM_BUCKETS))


BUCKET_THRESHOLDS = _bucket_thresholds()


BIAS_MAPS = 8


def _bias_tile_kernel(tab_ref, causal_ref, window_ref):
    m0 = pl.program_id(0) * BIAS_MAPS
    r = lax.broadcasted_iota(jnp.int32, (TQ, 2 * TQ), 0)
    c = lax.broadcasted_iota(jnp.int32, (TQ, 2 * TQ), 1)
    d = r - c + TQ
    n = jnp.maximum(d, 0)
    max_exact = NUM_BUCKETS // 2
    large = jnp.full_like(n, max_exact)
    for thr in BUCKET_THRESHOLDS:
        large = large + (n >= thr).astype(jnp.int32)
    bucket = jnp.where(n < max_exact, n, large)
    for mm in range(BIAS_MAPS):
        val = jnp.zeros((TQ, 2 * TQ), F32)
        for b in range(NUM_BUCKETS):
            val = jnp.where(bucket == b, tab_ref[b, m0 + mm] * LOG2_E, val)
        causal = jnp.where(d >= 0, val, NEG_INF)
        causal_ref[mm] = causal
        window_ref[mm] = jnp.where(d < WINDOW, causal, NEG_INF)


def bias_tiles(table):
    shape = jax.ShapeDtypeStruct((N_HEADS, TQ, 2 * TQ), F32)
    return pl.pallas_call(
        _bias_tile_kernel,
        grid=(N_HEADS // BIAS_MAPS,),
        in_specs=[pl.BlockSpec(memory_space=pltpu.SMEM)],
        out_specs=[pl.BlockSpec((BIAS_MAPS, TQ, 2 * TQ), lambda m: (m, 0, 0))] * 2,
        out_shape=[shape, shape],
        compiler_params=_params("parallel"),
        name="bias_tiles",
    )(table)


def _head_select(x, lo, first):
    zero = jnp.zeros_like(x)
    return jnp.where(lo, x, zero) if first else jnp.where(lo, zero, x)


def _dup_head(blk, second):
    x = blk.astype(F32)
    lo = lax.broadcasted_iota(jnp.int32, x.shape, 1) < HEAD_DIM
    return jnp.where(jnp.logical_xor(lo, second), x, pltpu.roll(x, HEAD_DIM, 1)).astype(BF16)


def _stack_heads(q):
    rows = q.shape[0]
    lo = lax.broadcasted_iota(jnp.int32, (rows, LANES), 1) < HEAD_DIM
    parts = []
    for p in range(q.shape[1] // LANES):
        qp = q[:, p * LANES:(p + 1) * LANES]
        parts += [_head_select(qp, lo, True), _head_select(qp, lo, False)]
    return jnp.concatenate(parts, axis=0)


def _pair_selector(n_keys):
    r = lax.broadcasted_iota(jnp.int32, (2 * n_keys, LANES), 0)
    c = lax.broadcasted_iota(jnp.int32, (2 * n_keys, LANES), 1)
    return jnp.where((r < n_keys) == (c < HEAD_DIM), 1.0, 0.0).astype(BF16)


def _paired_values(vv):
    lo = lax.broadcasted_iota(jnp.int32, vv.shape, 1) < HEAD_DIM
    v_both = jnp.concatenate([_head_select(vv, lo, True), _head_select(vv, lo, False)], axis=0)
    return jnp.concatenate([v_both, _pair_selector(vv.shape[0])], axis=1)


SWA_BLOCK = WINDOW


def _swa_kernel(sink_ref, q_ref, kp_ref, ko_ref, vp_ref, vo_ref, bias_ref, o_ref):
    i = pl.program_id(1)
    blk = SWA_BLOCK
    col = lax.broadcasted_iota(jnp.int32, (1, 2 * blk), 1)
    no_prev = (col < blk) & (i == 0)
    lo = lax.broadcasted_iota(jnp.int32, (blk, LANES), 1) < HEAD_DIM

    def scores(g):
        stack = _stack_heads(q_ref[:, g * 4 * HEAD_DIM:(g + 1) * 4 * HEAD_DIM])
        cols = slice(g * LANES, (g + 1) * LANES)
        kk = jnp.concatenate([kp_ref[:, cols], ko_ref[:, cols]], axis=0)
        s = _dot_nt(stack, kk) + bias_ref[4 * g:4 * g + 4].reshape(4 * blk, 2 * blk)
        return jnp.where(no_prev, NEG_INF, s)

    def softmax(g, s):
        sink = jnp.concatenate([jnp.full((blk, 1), sink_ref[4 * g + hh] * LOG2_E, F32) for hh in range(4)], axis=0)
        m = jnp.maximum(jnp.max(s, axis=-1, keepdims=True), sink)
        return jnp.exp2(s - m).astype(BF16), jnp.exp2(sink - m)

    def finish(g, e, e_sink):
        cols = slice(g * LANES, (g + 1) * LANES)
        v_ext = _paired_values(jnp.concatenate([vp_ref[:, cols], vo_ref[:, cols]], axis=0))
        outs = []
        for p in range(2):
            r0 = 2 * p * blk
            e_both = jnp.concatenate([e[r0:r0 + blk], e[r0 + blk:r0 + 2 * blk]], axis=1)
            res = _dot(e_both, v_ext)
            den = res[:, LANES:] + jnp.where(lo, e_sink[r0:r0 + blk], e_sink[r0 + blk:r0 + 2 * blk])
            outs.append(res[:, :LANES] / den)
        o_ref[:, g * 4 * HEAD_DIM:(g + 1) * 4 * HEAD_DIM] = jnp.concatenate(outs, axis=1).astype(o_ref.dtype)

    s_next = scores(0)
    prev = None
    for g in range(N_KV_HEADS):
        s = s_next
        if g + 1 < N_KV_HEADS:
            s_next = scores(g + 1)
        cur = softmax(g, s)
        if prev is not None:
            finish(g - 1, *prev)
        prev = cur
    finish(N_KV_HEADS - 1, *prev)


def swa_attention(q, kv, sinks, bias_window, batch):
    t, dq = q.shape
    blk = SWA_BLOCK
    nq = t // batch // blk
    dk = 2 * N_KV_HEADS * HEAD_DIM

    def own(col):
        return lambda b, i: (b * nq + i, col)

    def prev(col):
        return lambda b, i: (b * nq + jnp.maximum(i - 1, 0), col)

    bias_spec = pl.BlockSpec((N_HEADS, blk, 2 * blk), lambda b, i: (0, TQ // blk - 1, TQ // blk - 1))
    return pl.pallas_call(
        _swa_kernel,
        grid=(batch, nq),
        in_specs=[
            pl.BlockSpec(memory_space=pltpu.SMEM),
            pl.BlockSpec((blk, dq), own(0)),
            pl.BlockSpec((blk, dk), prev(0)),
            pl.BlockSpec((blk, dk), own(0)),
            pl.BlockSpec((blk, dk), prev(1)),
            pl.BlockSpec((blk, dk), own(1)),
            bias_spec,
        ],
        out_specs=pl.BlockSpec((blk, dq), own(0)),
        out_shape=jax.ShapeDtypeStruct((t, dq), BF16),
        compiler_params=_params("parallel", "parallel"),
        name="swa_attention",
    )(sinks, q, kv, kv, kv, kv, bias_window)


def _causal_scores(stack, k_ref, kv_len, bias_ref, far_bias):
    rows = stack.shape[0]
    s = _dot_nt(stack, k_ref[0:kv_len, :])
    if kv_len == TQ:
        return s + bias_ref[:, :, TQ:2 * TQ].reshape(rows, TQ)
    near = bias_ref[...].reshape(rows, 2 * TQ)
    if kv_len == 2 * TQ:
        return s + near
    n_far = kv_len - 2 * TQ
    return jnp.concatenate([s[:, :n_far] + far_bias, s[:, n_far:] + near], axis=1)


def _softmax_pipeline(n_blocks, scores, finish):
    order = list(range(n_blocks - 1, -1, -1))
    s_next = scores(order[0])
    p_prev = None
    for n in range(n_blocks):
        s = s_next
        if n + 1 < n_blocks:
            s_next = scores(order[n + 1])
        p = jnp.exp2(s - jnp.max(s, axis=-1, keepdims=True)).astype(BF16)
        if p_prev is not None:
            finish(order[n - 1], p_prev)
        p_prev = p
    finish(order[-1], p_prev)


def _diff_kernel(tab_ref, lam_ref, g_ref, q_ref, k_ref, v_ref, bias_ref, o_ref, *, lam_init):
    h = pl.program_id(0)
    last = NUM_BUCKETS - 1
    far_bias = jnp.concatenate([jnp.full((TQ, 1), tab_ref[last, 2 * h] * LOG2_E, F32),
                                jnp.full((TQ, 1), tab_ref[last, 2 * h + 1] * LOG2_E, F32)], axis=0)
    lf = lam_ref[...]
    lam = (jnp.exp(jnp.sum(lf[0:1] * lf[1:2], axis=-1, keepdims=True))
           - jnp.exp(jnp.sum(lf[2:3] * lf[3:4], axis=-1, keepdims=True)) + lam_init)
    ones = jnp.ones((q_ref.shape[0], LANES), BF16)

    def scores(c):
        stack = _stack_heads(q_ref[c * TQ:(c + 1) * TQ, :])
        return _causal_scores(stack, k_ref, (c + 1) * TQ, bias_ref, far_bias)

    def finish(c, p):
        kv_len = (c + 1) * TQ
        v_ext = jnp.concatenate([v_ref[0:kv_len, :], ones[0:kv_len]], axis=1)
        res = _dot(p, v_ext)
        o12 = res[:, :LANES] / res[:, LANES:]
        o = o12[:TQ] - lam * o12[TQ:]
        o_ref[c * TQ:(c + 1) * TQ, :] = (_rms(o, g_ref[...]) * (1.0 - lam_init)).astype(o_ref.dtype)

    _softmax_pipeline(q_ref.shape[0] // TQ, scores, finish)


def diff_attention(q, kv, table, lambdas, subln_g, bias_causal, batch, layer_idx):
    t = q.shape[0]
    s = t // batch
    lam_init = 0.8 - 0.6 * math.exp(-0.3 * layer_idx)
    return pl.pallas_call(
        functools.partial(_diff_kernel, lam_init=lam_init),
        grid=(DIFF_HEADS, batch),
        in_specs=[
            pl.BlockSpec(memory_space=pltpu.SMEM),
            pl.BlockSpec((4, HEAD_DIM), lambda h, b: (0, 0)),
            pl.BlockSpec((1, LANES), lambda h, b: (0, 0)),
            pl.BlockSpec((s, LANES), lambda h, b: (b, h)),
            pl.BlockSpec((s, LANES), lambda h, b: (b, h)),
            pl.BlockSpec((s, LANES), lambda h, b: (b, DIFF_HEADS + h)),
            pl.BlockSpec((2, TQ, 2 * TQ), lambda h, b: (h, 0, 0)),
        ],
        out_specs=pl.BlockSpec((s, LANES), lambda h, b: (b, h)),
        out_shape=jax.ShapeDtypeStruct((t, DIFF_HEADS * 2 * HEAD_DIM), BF16),
        compiler_params=_params("parallel", "parallel"),
        name="diff_attention",
    )(table, lambdas, subln_g.reshape(1, LANES), q, kv, kv, bias_causal)


HGRN_LEVELS = tuple(2 ** e for e in range(int(math.log2(HGRN_CHUNK)) - 1, -1, -1))
HGRN_GROUP = 256


def _hgrn_kernel(lb_ref, g_ref, q_ref, f_ref, i_ref, og_ref, o_ref,
                 qf_ref, kf_ref, bq_ref, kd_ref, dec_ref, v_ref):
    s_len, hd = q_ref.shape
    c_len = HGRN_CHUNK
    n_lvl = len(HGRN_LEVELS)
    grp = min(HGRN_GROUP, s_len)
    rc = lax.broadcasted_iota(jnp.int32, (grp, hd), 0) & (c_len - 1)
    lb = lb_ref[...]
    log_lb, log_1m_lb = jnp.log(lb), jnp.log1p(-lb)

    def prepare(gi):
        rows = slice(gi * grp, (gi + 1) * grp)
        z = f_ref[rows, :]
        u = jnp.exp(-jnp.abs(z))
        w = 1.0 + u
        log_sig = jnp.minimum(z, 0.0) - jnp.log(w)
        c = log_1m_lb + log_sig
        delta = log_lb - c
        log_f = jnp.where(jnp.isnan(delta), log_lb + c,
                          jnp.maximum(log_lb, c) + jnp.log(1.0 + jnp.exp(-jnp.abs(delta))))
        inv_w = 1.0 / w
        key = (1.0 - lb) * jnp.where(z >= 0.0, u * inv_w, inv_w)
        qs = _silu(q_ref[rows, :])
        v_ref[rows, :] = i_ref[rows, :].astype(BF16)

        b = log_f * LOG2_E
        for d in (1, 2, 4, 8, 16, 32):
            b = b + jnp.where(rc >= d, pltpu.roll(b, d, 0), 0.0)

        bq_ref[rows, :] = (qs * jnp.exp2(b)).astype(BF16)
        qf_ref[n_lvl, rows, :] = qs.astype(BF16)
        kf_ref[n_lvl, rows, :] = key.astype(BF16)

        r_m = jnp.where(rc >= 1, pltpu.roll(b, 1, 0), 0.0)
        for m in (1, 2, 4, 8, 16, 32):
            if m > 1:
                half = m // 2
                r_m = jnp.where((rc & half) != 0, pltpu.roll(r_m, half, 0), r_m)
            e_m = pltpu.roll(r_m, grp - m, 0)
            lvl = HGRN_LEVELS.index(m)
            qf_ref[lvl, rows, :] = (qs * jnp.exp2(b - r_m)).astype(BF16)
            kf_ref[lvl, rows, :] = (key * jnp.exp2(jnp.minimum(e_m - b, 0.0))).astype(BF16)
        b3 = b.reshape(grp // c_len, c_len, hd)
        b_end = jnp.broadcast_to(b3[:, c_len - 1:c_len, :], b3.shape).reshape(grp, hd)
        kd_ref[rows, :] = (key * jnp.exp2(b_end - b)).astype(BF16)
        dec_ref[rows, :] = jnp.exp2(b_end)

    ti = lax.broadcasted_iota(jnp.int32, (c_len, c_len), 0)
    si = lax.broadcasted_iota(jnp.int32, (c_len, c_len), 1)
    masks = [((ti // m) % 2 == 1) & (si // m == ti // m - 1) for m in HGRN_LEVELS] + [ti == si]
    g = g_ref[...]

    def chunk(ci, st):
        rows = slice(ci * c_len, (ci + 1) * c_len)
        att = jnp.zeros((c_len, c_len), F32)
        for lvl in range(n_lvl + 1):
            att = jnp.where(masks[lvl], _dot_nt(qf_ref[lvl, rows, :], kf_ref[lvl, rows, :]), att)
        vc = v_ref[rows, :]
        o = _dot(att.astype(BF16), vc) + _dot_nt(bq_ref[rows, :], st.astype(BF16))
        o_ref[rows, :] = (_rms(o, g) * _silu(og_ref[rows, :])).astype(o_ref.dtype)
        return st * dec_ref[ci * c_len:ci * c_len + 1, :] + _dot_tn(vc, kd_ref[rows, :])

    n_groups = s_len // grp
    per_group = grp // c_len
    st = jnp.zeros((hd, hd), F32)
    prepare(0)
    for gi in range(n_groups):
        if gi + 1 < n_groups:
            prepare(gi + 1)
        for ci in range(gi * per_group, (gi + 1) * per_group):
            st = chunk(ci, st)


def hgrn_recurrence(z, lb, norm_g, batch):
    t = z.shape[0]
    s = t // batch
    hd = HGRN_HEAD_DIM
    n_lvl = len(HGRN_LEVELS)

    def part(p):
        return pl.BlockSpec((s, hd), lambda b, h: (b, p * HGRN_HEADS + h))

    return pl.pallas_call(
        _hgrn_kernel,
        grid=(batch, HGRN_HEADS),
        in_specs=[pl.BlockSpec((1, hd), lambda b, h: (0, h)), pl.BlockSpec((1, hd), lambda b, h: (0, 0)),
                  part(0), part(1), part(2), part(3)],
        out_specs=pl.BlockSpec((s, hd), lambda b, h: (b, h)),
        out_shape=jax.ShapeDtypeStruct((t, HGRN_HEADS * hd), BF16),
        scratch_shapes=[
            pltpu.VMEM((n_lvl + 1, s, hd), BF16),
            pltpu.VMEM((n_lvl + 1, s, hd), BF16),
            pltpu.VMEM((s, hd), BF16),
            pltpu.VMEM((s, hd), BF16),
            pltpu.VMEM((s, hd), F32),
            pltpu.VMEM((s, hd), BF16),
        ],
        compiler_params=_params("parallel", "parallel"),
        name="hgrn_recurrence",
    )(lb.reshape(1, -1), norm_g.reshape(1, hd), z, z, z, z)


def _sortable(score):
    score = jnp.where(score == 0.0, 0.0, score)
    bits = lax.bitcast_convert_type(score, jnp.int32)
    return bits ^ ((bits >> 31) & jnp.int32(0x7FFFFFFF))


def _dsa_select(c, qi_ref, ki_ref, wi_ref, qsel_ref, kib_ref, wt_ref, sc_ref, keys_ref, mask_ref, n_sel):
    kv_len = (c + 1) * TQ
    s_len = mask_ref.shape[1]
    if kv_len <= n_sel:
        mask_ref[:, 0:kv_len] = jnp.zeros((TQ, kv_len), mask_ref.dtype)
        if kv_len < s_len:
            mask_ref[:, kv_len:] = jnp.full((TQ, s_len - kv_len), NEG_INF, mask_ref.dtype)
        return
    lo = lax.broadcasted_iota(jnp.int32, (TQ, LANES), 1) < HEAD_DIM
    for p in range(IDX_HEADS // 2):
        qp = qi_ref[:, p * LANES:(p + 1) * LANES]
        qsel_ref[2 * p] = _head_select(qp, lo, True)
        qsel_ref[2 * p + 1] = _head_select(qp, lo, False)
    wt_ref[...] = wi_ref[...].T * (IDX_HEADS ** -0.5 * IDX_DIM ** -0.5)
    ki = ki_ref[0:kv_len, :]
    lo_k = lax.broadcasted_iota(jnp.int32, (kv_len, LANES), 1) < IDX_DIM
    kib_ref[0:kv_len, :] = jnp.where(lo_k, ki, pltpu.roll(ki, IDX_DIM, 1)).astype(BF16)
    sc_ref[0:kv_len, :] = jnp.zeros((kv_len, TQ), F32)

    def head_pair(hp, carry):
        kib = kib_ref[0:kv_len, :]
        acc = sc_ref[0:kv_len, :]
        for hh in (2 * hp, 2 * hp + 1):
            logit = _dot_nt(kib, qsel_ref[hh])
            acc = acc + jnp.maximum(logit, 0.0) * wt_ref[pl.ds(IDX_DIM + hh, 1), :]
        sc_ref[0:kv_len, :] = acc
        return carry

    lax.fori_loop(0, IDX_HEADS // 2, head_pair, 0)

    key_i = lax.broadcasted_iota(jnp.int32, (TQ, TQ), 0)
    qry_i = lax.broadcasted_iota(jnp.int32, (TQ, TQ), 1)
    if c > 0:
        keys_ref[0:c * TQ, :] = _sortable(sc_ref[0:c * TQ, :])
    keys_ref[c * TQ:kv_len, :] = _sortable(jnp.where(key_i <= qry_i, sc_ref[c * TQ:kv_len, :], NEG_INF))

    def count(pred):
        return jnp.sum(pred(keys_ref[0:kv_len, :]).astype(jnp.int32), axis=0, keepdims=True)

    def search(it, thr):
        trial = thr ^ (jnp.int32(1) << (31 - it))
        return jnp.where(count(lambda kj: kj >= trial) >= n_sel, trial, thr)

    thr = lax.fori_loop(0, 32, search, jnp.full((1, TQ), jnp.iinfo(jnp.int32).min, jnp.int32))
    need = (n_sel - count(lambda kj: kj > thr)).astype(F32)
    earlier = (qry_i < key_i).astype(BF16)
    base = jnp.zeros((1, TQ), F32)
    for j in range(c + 1):
        kj = keys_ref[j * TQ:(j + 1) * TQ, :]
        eq = kj == thr
        eq_f = jnp.where(eq, 1.0, 0.0)
        rank = base + _dot(earlier, eq_f.astype(BF16))
        sel = (kj > thr) | (eq & (rank < need))
        mask_ref[:, j * TQ:(j + 1) * TQ] = jnp.where(sel, 0.0, NEG_INF).T.astype(mask_ref.dtype)
        base = base + jnp.sum(eq_f, axis=0, keepdims=True)
    if kv_len < s_len:
        mask_ref[:, kv_len:] = jnp.full((TQ, s_len - kv_len), NEG_INF, mask_ref.dtype)


def _dsa_select_kernel(qi_ref, ki_ref, wi_ref, mask_ref, qsel_ref, kib_ref, wt_ref, sc_ref, keys_ref, *, n_sel, nq):
    i = pl.program_id(1)
    for c in range(nq):
        pl.when(i == c)(functools.partial(
            _dsa_select, c, qi_ref, ki_ref, wi_ref, qsel_ref, kib_ref, wt_ref, sc_ref, keys_ref, mask_ref, n_sel))


def _dsa_attend_kernel(tab_ref, q_ref, kpair_ref, vpair_ref, mask_ref, bias_ref, o_ref, k_ref, v_ref):
    g = pl.program_id(1)
    k_ref[...] = _dup_head(kpair_ref[...], g % 2 == 1)
    v_ref[...] = _dup_head(vpair_ref[...], g % 2 == 1)
    last = NUM_BUCKETS - 1
    far_bias = jnp.concatenate(
        [jnp.full((TQ, 1), tab_ref[last, 4 * g + hh] * LOG2_E, F32) for hh in range(4)], axis=0)

    def scores(c):
        rows = slice(c * TQ, (c + 1) * TQ)
        kv_len = (c + 1) * TQ
        s = _causal_scores(_stack_heads(q_ref[rows, :]), k_ref, kv_len, bias_ref, far_bias)
        return s + jnp.concatenate([mask_ref[rows, 0:kv_len].astype(F32)] * 4, axis=0)

    def finish(c, p):
        v_ext = _paired_values(v_ref[0:(c + 1) * TQ, :])
        outs = []
        for pr in range(2):
            r0 = 2 * pr * TQ
            p_both = jnp.concatenate([p[r0:r0 + TQ], p[r0 + TQ:r0 + 2 * TQ]], axis=1)
            res = _dot(p_both, v_ext)
            outs.append(res[:, :LANES] / res[:, LANES:])
        o_ref[c * TQ:(c + 1) * TQ, :] = jnp.concatenate(outs, axis=1).astype(o_ref.dtype)

    _softmax_pipeline(q_ref.shape[0] // TQ, scores, finish)


def dsa_attention(q, kv, qi, z_idx, table, bias_causal, batch):
    t = q.shape[0]
    s = t // batch
    nq = s // TQ
    n_sel = min(TOPK_MAX, s // 4)
    mask = pl.pallas_call(
        functools.partial(_dsa_select_kernel, n_sel=n_sel, nq=nq),
        grid=(batch, nq),
        in_specs=[
            pl.BlockSpec((TQ, IDX_HEADS * IDX_DIM), lambda b, i: (b * nq + i, 0)),
            pl.BlockSpec((s, LANES), lambda b, i: (b, 0)),
            pl.BlockSpec((TQ, LANES), lambda b, i: (b * nq + i, 0)),
        ],
        out_specs=pl.BlockSpec((TQ, s), lambda b, i: (b * nq + i, 0)),
        out_shape=jax.ShapeDtypeStruct((t, s), BF16),
        scratch_shapes=[
            pltpu.VMEM((IDX_HEADS, TQ, LANES), BF16),
            pltpu.VMEM((s, LANES), BF16),
            pltpu.VMEM((LANES, TQ), F32),
            pltpu.VMEM((s, TQ), F32),
            pltpu.VMEM((s, TQ), jnp.int32),
        ],
        compiler_params=_params("parallel", "parallel"),
        name="dsa_select",
    )(qi, z_idx, z_idx)
    return pl.pallas_call(
        _dsa_attend_kernel,
        grid=(batch, N_KV_HEADS),
        in_specs=[
            pl.BlockSpec(memory_space=pltpu.SMEM),
            pl.BlockSpec((s, 4 * HEAD_DIM), lambda b, g: (b, g)),
            pl.BlockSpec((s, LANES), lambda b, g: (b, g // 2)),
            pl.BlockSpec((s, LANES), lambda b, g: (b, N_KV_HEADS // 2 + g // 2)),
            pl.BlockSpec((s, s), lambda b, g: (b, 0)),
            pl.BlockSpec((4, TQ, 2 * TQ), lambda b, g: (g, 0, 0)),
        ],
        out_specs=pl.BlockSpec((s, 4 * HEAD_DIM), lambda b, g: (b, g)),
        out_shape=jax.ShapeDtypeStruct((t, N_HEADS * HEAD_DIM), BF16),
        scratch_shapes=[pltpu.VMEM((s, LANES), BF16), pltpu.VMEM((s, LANES), BF16)],
        compiler_params=_params("parallel", "parallel"),
        name="dsa_attend",
    )(table, q, kv, kv, mask, bias_causal)


def _lower_bound_kernel(x_ref, o_ref):
    x = x_ref[...]
    e = jnp.exp(x - jnp.max(x, axis=0, keepdims=True))
    soft = e / jnp.sum(e, axis=0, keepdims=True)
    run = soft[0:1]
    o_ref[0:1, :] = run - soft[0:1]
    for r in range(1, x.shape[0]):
        run = run + soft[r:r + 1]
        o_ref[r:r + 1, :] = run - soft[0:1]


def hgrn_lower_bounds(logits):
    return pl.pallas_call(
        _lower_bound_kernel,
        out_shape=jax.ShapeDtypeStruct(logits.shape, F32),
        name="hgrn_lower_bounds",
    )(logits)


def kernel(x, rel_bias_table, hgrn_lb_logits, norm_g, ffn_w_up, ffn_conv, ffn_w_down, swa_w_in, swa_w_out,
           swa_sinks, diff_w_in, diff_w_out, diff_lambda, diff_subln_g, hgrn_w_in, hgrn_w_out, hgrn_norm_g,
           dsa_w_in, dsa_w_out):
    batch, seq, d = x.shape
    t = batch * seq
    depth = norm_g.shape[0]
    n_mixers = 4
    bias_causal, bias_window = bias_tiles(rel_bias_table)
    lb_all = hgrn_lower_bounds(hgrn_lb_logits)
    kv = N_KV_HEADS * HEAD_DIM
    q_scale = HEAD_DIM ** -0.5 * LOG2_E

    x2 = x.reshape(t, d)
    h = None
    for i in range(depth):
        kind, j = i % n_mixers, i // n_mixers
        if kind == 0:
            w = swa_w_in[j]
            src, g0 = (x2, norm_g[i, 0]) if h is None else (h, None)
            q = matmul(src, w, BF16, n_cols=d, scale=q_scale, norm_g=g0)
            kvd = matmul(src, w, BF16, col0=d, n_cols=2 * kv, dup_heads=True, norm_g=g0)
            o = swa_attention(q, kvd, swa_sinks[j], bias_window, batch)
            w_out = swa_w_out
        elif kind == 1:
            w = diff_w_in[j]
            q = matmul(h, w, BF16, n_cols=d, scale=q_scale)
            kvp = matmul(h, w, BF16, col0=d)
            o = diff_attention(q, kvp, rel_bias_table, diff_lambda[j], diff_subln_g[j], bias_causal, batch, i)
            w_out = diff_w_out
        elif kind == 2:
            z = matmul(h, hgrn_w_in[j], F32)
            o = hgrn_recurrence(z, lb_all[i], hgrn_norm_g[j], batch)
            w_out = hgrn_w_out
        else:
            w = dsa_w_in[j]
            n_qi = IDX_HEADS * IDX_DIM
            q = matmul(h, w, BF16, n_cols=d, scale=q_scale)
            kvd = matmul(h, w, BF16, col0=d, n_cols=2 * kv)
            qi = matmul(h, w, BF16, col0=d + 2 * kv, n_cols=n_qi)
            w_idx = jnp.pad(w[:, d + 2 * kv + n_qi:], ((0, 0), (0, LANES - IDX_DIM - IDX_HEADS)))
            z_idx = matmul(h, w_idx, F32)
            o = dsa_attention(q, kvd, qi, z_idx, rel_bias_table, bias_causal, batch)
            w_out = dsa_w_out
        x2, h = proj_residual(o, w_out.astype(BF16), j, x2, norm_g[i, 1], norm_g[i, 2])
        a, w_down = ffn_up(h, ffn_w_up, ffn_conv, ffn_w_down, i, batch)
        x2, h = proj_residual(a, w_down, 0, x2, norm_g[i, 3], norm_g[(i + 1) % depth, 0])
    return x2.reshape(batch, seq, d)
```

```python
import functools
import math

import jax
import jax.numpy as jnp
import numpy as np
from jax import lax
from jax.experimental import pallas as pl
from jax.experimental.pallas import tpu as pltpu

D_MODEL = 2048
HEAD_DIM = 64
N_HEADS = 32
N_KV_HEADS = 8
WINDOW = 128
DIFF_HEADS = 16
HGRN_HEADS = 16
HGRN_HEAD_DIM = 128
HGRN_CHUNK = 64
IDX_HEADS = 16
IDX_DIM = 64
TOPK_MAX = 256
D_FF = 5632
CONV_WIDTH = 3
NUM_BUCKETS = 32
MAX_DISTANCE = 128
RMS_EPS = 1e-6

LANES = 128
TQ = 256
VMEM_LIMIT = 56 * 1024 * 1024
NEG_INF = float("-inf")
LOG2_E = math.log2(math.e)
BF16 = jnp.bfloat16
F32 = jnp.float32


def _params(*sem):
    return pltpu.CompilerParams(dimension_semantics=sem, vmem_limit_bytes=VMEM_LIMIT)


def _dot(a, b):
    return jnp.dot(a, b, preferred_element_type=F32)


def _dot_nt(a, b):
    return lax.dot_general(a, b, (((1,), (1,)), ((), ())), preferred_element_type=F32)


def _dot_tn(a, b):
    return lax.dot_general(a, b, (((0,), (0,)), ((), ())), preferred_element_type=F32)


def _rms(x, g):
    return x * lax.rsqrt(jnp.mean(x * x, axis=-1, keepdims=True) + RMS_EPS) * g


def _silu(x):
    return x * (1.0 / (1.0 + jnp.exp(-x)))


def _dup_heads(w):
    lo = lax.broadcasted_iota(jnp.int32, (w.shape[0], LANES), 1) < HEAD_DIM
    parts = []
    for p in range(w.shape[1] // LANES):
        blk = w[:, p * LANES:(p + 1) * LANES]
        swapped = pltpu.roll(blk, HEAD_DIM, 1)
        parts += [jnp.where(lo, blk, swapped), jnp.where(lo, swapped, blk)]
    return jnp.concatenate(parts, axis=1)


def _mm_kernel(x_ref, g_ref, w_ref, o_ref, wb_ref, *, scale, dup_heads, norm):
    @pl.when(pl.program_id(1) == 0)
    def _():
        w = w_ref[...]
        wb_ref[...] = (_dup_heads(w) if dup_heads else w).astype(BF16)

    x = _rms(x_ref[...], g_ref[...]).astype(BF16) if norm else x_ref[...]
    y = _dot(x, wb_ref[...])
    o_ref[...] = (y if scale == 1.0 else y * scale).astype(o_ref.dtype)


def matmul(x, w, out_dtype, *, col0=0, n_cols=None, scale=1.0, dup_heads=False, norm_g=None):
    t, k = x.shape
    n_cols = w.shape[1] - col0 if n_cols is None else n_cols
    tm = 1024
    tn_out = next(c for c in (1024, 512, 256, 128) if (n_cols * (2 if dup_heads else 1)) % c == 0)
    tn_in = tn_out // 2 if dup_heads else tn_out
    assert n_cols % tn_in == 0 and col0 % tn_in == 0, (n_cols, col0, tn_in)
    n_out = n_cols * (2 if dup_heads else 1)
    norm = norm_g is not None
    g = norm_g.reshape(1, k) if norm else jnp.ones((1, k), F32)
    return pl.pallas_call(
        functools.partial(_mm_kernel, scale=scale, dup_heads=dup_heads, norm=norm),
        grid=(n_cols // tn_in, t // tm),
        in_specs=[pl.BlockSpec((tm, k), lambda j, i: (i, 0)),
                  pl.BlockSpec((1, k), lambda j, i: (0, 0)),
                  pl.BlockSpec((k, tn_in), lambda j, i: (0, col0 // tn_in + j))],
        out_specs=pl.BlockSpec((tm, tn_out), lambda j, i: (i, j)),
        out_shape=jax.ShapeDtypeStruct((t, n_out), out_dtype),
        scratch_shapes=[pltpu.VMEM((k, tn_out), BF16)],
        compiler_params=_params("parallel", "arbitrary"),
        name="in_proj",
    )(x, g, w)


PROJ_SUB_ROWS = 128


def _proj_res_kernel(a_ref, w_ref, x_ref, go_ref, gn_ref, xo_ref, h_ref):
    tm = a_ref.shape[0]
    sub = PROJ_SUB_ROWS if tm >= 4 * PROJ_SUB_ROWS else tm
    ys = [_dot(a_ref[r:r + sub, :], w_ref[...]) for r in range(0, tm, sub)]
    for n, r in enumerate(range(0, tm, sub)):
        xn = x_ref[r:r + sub, :] + _rms(ys[n], go_ref[...])
        xo_ref[r:r + sub, :] = xn
        h_ref[r:r + sub, :] = _rms(xn, gn_ref[...]).astype(h_ref.dtype)


def _proj_rows(k, d):
    for tm in (512, 256, 128):
        streamed = 2 * (tm * k * 2 + 2 * tm * d * 4 + tm * d * 2)
        temporaries = 3 * tm * d * 4
        if k * d * 2 + streamed + temporaries <= VMEM_LIMIT - (4 << 20):
            return tm
    raise ValueError("projection weight does not fit in VMEM")


def proj_residual(a, w, layer, x, g_out, g_next):
    t, k = a.shape
    d = w.shape[2]
    tm = _proj_rows(k, d)
    return pl.pallas_call(
        _proj_res_kernel,
        grid=(t // tm,),
        in_specs=[
            pl.BlockSpec((tm, k), lambda i: (i, 0)),
            pl.BlockSpec((None, k, d), lambda i: (layer, 0, 0), pipeline_mode=pl.Buffered(1)),
            pl.BlockSpec((tm, d), lambda i: (i, 0)),
            pl.BlockSpec((1, d), lambda i: (0, 0)),
            pl.BlockSpec((1, d), lambda i: (0, 0)),
        ],
        out_specs=[pl.BlockSpec((tm, d), lambda i: (i, 0)), pl.BlockSpec((tm, d), lambda i: (i, 0))],
        out_shape=[jax.ShapeDtypeStruct((t, d), F32), jax.ShapeDtypeStruct((t, d), BF16)],
        compiler_params=_params("parallel"),
        name="proj_residual",
    )(a, w, x, g_out.reshape(1, d), g_next.reshape(1, d))


FFN_ROWS = 512
FFN_COLS = 512


def _ffn_up_kernel(h_ref, wg_ref, wv_ref, cg_ref, cv_ref, wd_ref, o_ref, wdb_ref):
    @pl.when(pl.program_id(1) == 0)
    def _():
        wdb_ref[...] = wd_ref[...].astype(BF16)

    s = h_ref.shape[0]
    rows = min(FFN_ROWS, s)
    tn = o_ref.shape[1]
    row = lax.broadcasted_iota(jnp.int32, (rows, 1), 0)
    wg, wv = wg_ref[...].astype(BF16), wv_ref[...].astype(BF16)
    cg, cv = cg_ref[...], cv_ref[...]

    def conv(u, tail, c):
        u1 = jnp.where(row >= 1, pltpu.roll(u, 1, 0), tail[1:2])
        u2 = jnp.where(row >= 2, pltpu.roll(u, 2, 0), jnp.where(row == 1, tail[1:2], tail[0:1]))
        return (c[0:1] * u2 + c[1:2] * u1) + c[2:3] * u

    tail_g = tail_v = jnp.zeros((CONV_WIDTH - 1, tn), F32)
    for r in range(s // rows):
        hr = h_ref[r * rows:(r + 1) * rows, :]
        ug, uv = _dot(hr, wg), _dot(hr, wv)
        out = _silu(conv(ug, tail_g, cg)) * conv(uv, tail_v, cv)
        o_ref[r * rows:(r + 1) * rows, :] = out.astype(o_ref.dtype)
        tail_g, tail_v = ug[rows - 2:rows], uv[rows - 2:rows]


def ffn_up(h, w_up, conv_w, w_down, layer, batch):
    t, d = h.shape
    s = t // batch
    tn = FFN_COLS
    nj = D_FF // tn
    return pl.pallas_call(
        _ffn_up_kernel,
        grid=(nj, batch),
        in_specs=[
            pl.BlockSpec((s, d), lambda j, b: (b, 0)),
            pl.BlockSpec((None, d, tn), lambda j, b: (layer, 0, j)),
            pl.BlockSpec((None, d, tn), lambda j, b: (layer, 0, nj + j)),
            pl.BlockSpec((None, CONV_WIDTH, tn), lambda j, b: (layer, 0, j)),
            pl.BlockSpec((None, CONV_WIDTH, tn), lambda j, b: (layer, 0, nj + j)),
            pl.BlockSpec((None, tn, d), lambda j, b: (layer, j, 0)),
        ],
        out_specs=[pl.BlockSpec((s, tn), lambda j, b: (b, j)), pl.BlockSpec((None, tn, d), lambda j, b: (0, j, 0))],
        out_shape=[jax.ShapeDtypeStruct((t, D_FF), BF16), jax.ShapeDtypeStruct((1, D_FF, d), BF16)],
        compiler_params=_params("arbitrary", "arbitrary"),
        name="ffn_up",
    )(h, w_up, w_up, conv_w, conv_w, w_down)


def _bucket_thresholds():
    max_exact = NUM_BUCKETS // 2
    n = np.arange(1, 2 * MAX_DISTANCE, dtype=np.float64)
    large = max_exact + np.floor(np.log(n / max_exact) / math.log(MAX_DISTANCE / max_exact) * (NUM_BUCKETS - max_exact))
    bucket = np.where(n < max_exact, n, np.minimum(large, NUM_BUCKETS - 1)).astype(np.int64)
    return tuple(int(n[bucket >= b][0]) for b in range(max_exact + 1, NUM_BUCKETS))


BUCKET_THRESHOLDS = _bucket_thresholds()


BIAS_MAPS = 8


def _bias_tile_kernel(tab_ref, causal_ref, window_ref):
    m0 = pl.program_id(0) * BIAS_MAPS
    r = lax.broadcasted_iota(jnp.int32, (TQ, 2 * TQ), 0)
    c = lax.broadcasted_iota(jnp.int32, (TQ, 2 * TQ), 1)
    d = r - c + TQ
    n = jnp.maximum(d, 0)
    max_exact = NUM_BUCKETS // 2
    large = jnp.full_like(n, max_exact)
    for thr in BUCKET_THRESHOLDS:
        large = large + (n >= thr).astype(jnp.int32)
    bucket = jnp.where(n < max_exact, n, large)
    for mm in range(BIAS_MAPS):
        val = jnp.zeros((TQ, 2 * TQ), F32)
        for b in range(NUM_BUCKETS):
            val = jnp.where(bucket == b, tab_ref[b, m0 + mm] * LOG2_E, val)
        causal = jnp.where(d >= 0, val, NEG_INF)
        causal_ref[mm] = causal
        window_ref[mm] = jnp.where(d < WINDOW, causal, NEG_INF)


def bias_tiles(table):
    shape = jax.ShapeDtypeStruct((N_HEADS, TQ, 2 * TQ), F32)
    return pl.pallas_call(
        _bias_tile_kernel,
        grid=(N_HEADS // BIAS_MAPS,),
        in_specs=[pl.BlockSpec(memory_space=pltpu.SMEM)],
        out_specs=[pl.BlockSpec((BIAS_MAPS, TQ, 2 * TQ), lambda m: (m, 0, 0))] * 2,
        out_shape=[shape, shape],
        compiler_params=_params("parallel"),
        name="bias_tiles",
    )(table)


def _head_select(x, lo, first):
    zero = jnp.zeros_like(x)
    return jnp.where(lo, x, zero) if first else jnp.where(lo, zero, x)


def _dup_head(blk, second):
    x = blk.astype(F32)
    lo = lax.broadcasted_iota(jnp.int32, x.shape, 1) < HEAD_DIM
    return jnp.where(jnp.logical_xor(lo, second), x, pltpu.roll(x, HEAD_DIM, 1)).astype(BF16)


def _stack_heads(q):
    rows = q.shape[0]
    lo = lax.broadcasted_iota(jnp.int32, (rows, LANES), 1) < HEAD_DIM
    parts = []
    for p in range(q.shape[1] // LANES):
        qp = q[:, p * LANES:(p + 1) * LANES]
        parts += [_head_select(qp, lo, True), _head_select(qp, lo, False)]
    return jnp.concatenate(parts, axis=0)


def _pair_selector(n_keys):
    r = lax.broadcasted_iota(jnp.int32, (2 * n_keys, LANES), 0)
    c = lax.broadcasted_iota(jnp.int32, (2 * n_keys, LANES), 1)
    return jnp.where((r < n_keys) == (c < HEAD_DIM), 1.0, 0.0).astype(BF16)


def _paired_values(vv):
    lo = lax.broadcasted_iota(jnp.int32, vv.shape, 1) < HEAD_DIM
    v_both = jnp.concatenate([_head_select(vv, lo, True), _head_select(vv, lo, False)], axis=0)
    return jnp.concatenate([v_both, _pair_selector(vv.shape[0])], axis=1)


SWA_BLOCK = WINDOW


def _swa_kernel(sink_ref, q_ref, kp_ref, ko_ref, vp_ref, vo_ref, bias_ref, o_ref):
    i = pl.program_id(1)
    blk = SWA_BLOCK
    col = lax.broadcasted_iota(jnp.int32, (1, 2 * blk), 1)
    no_prev = (col < blk) & (i == 0)
    lo = lax.broadcasted_iota(jnp.int32, (blk, LANES), 1) < HEAD_DIM

    def scores(g):
        stack = _stack_heads(q_ref[:, g * 4 * HEAD_DIM:(g + 1) * 4 * HEAD_DIM])
        cols = slice(g * LANES, (g + 1) * LANES)
        kk = jnp.concatenate([kp_ref[:, cols], ko_ref[:, cols]], axis=0)
        s = _dot_nt(stack, kk) + bias_ref[4 * g:4 * g + 4].reshape(4 * blk, 2 * blk)
        return jnp.where(no_prev, NEG_INF, s)

    def softmax(g, s):
        sink = jnp.concatenate([jnp.full((blk, 1), sink_ref[4 * g + hh] * LOG2_E, F32) for hh in range(4)], axis=0)
        m = jnp.maximum(jnp.max(s, axis=-1, keepdims=True), sink)
        return jnp.exp2(s - m).astype(BF16), jnp.exp2(sink - m)

    def finish(g, e, e_sink):
        cols = slice(g * LANES, (g + 1) * LANES)
        v_ext = _paired_values(jnp.concatenate([vp_ref[:, cols], vo_ref[:, cols]], axis=0))
        outs = []
        for p in range(2):
            r0 = 2 * p * blk
            e_both = jnp.concatenate([e[r0:r0 + blk], e[r0 + blk:r0 + 2 * blk]], axis=1)
            res = _dot(e_both, v_ext)
            den = res[:, LANES:] + jnp.where(lo, e_sink[r0:r0 + blk], e_sink[r0 + blk:r0 + 2 * blk])
            outs.append(res[:, :LANES] / den)
        o_ref[:, g * 4 * HEAD_DIM:(g + 1) * 4 * HEAD_DIM] = jnp.concatenate(outs, axis=1).astype(o_ref.dtype)

    s_next = scores(0)
    prev = None
    for g in range(N_KV_HEADS):
        s = s_next
        if g + 1 < N_KV_HEADS:
            s_next = scores(g + 1)
        cur = softmax(g, s)
        if prev is not None:
            finish(g - 1, *prev)
        prev = cur
    finish(N_KV_HEADS - 1, *prev)


def swa_attention(q, kv, sinks, bias_window, batch):
    t, dq = q.shape
    blk = SWA_BLOCK
    nq = t // batch // blk
    dk = 2 * N_KV_HEADS * HEAD_DIM

    def own(col):
        return lambda b, i: (b * nq + i, col)

    def prev(col):
        return lambda b, i: (b * nq + jnp.maximum(i - 1, 0), col)

    bias_spec = pl.BlockSpec((N_HEADS, blk, 2 * blk), lambda b, i: (0, TQ // blk - 1, TQ // blk - 1))
    return pl.pallas_call(
        _swa_kernel,
        grid=(batch, nq),
        in_specs=[
            pl.BlockSpec(memory_space=pltpu.SMEM),
            pl.BlockSpec((blk, dq), own(0)),
            pl.BlockSpec((blk, dk), prev(0)),
            pl.BlockSpec((blk, dk), own(0)),
            pl.BlockSpec((blk, dk), prev(1)),
            pl.BlockSpec((blk, dk), own(1)),
            bias_spec,
        ],
        out_specs=pl.BlockSpec((blk, dq), own(0)),
        out_shape=jax.ShapeDtypeStruct((t, dq), BF16),
        compiler_params=_params("parallel", "parallel"),
        name="swa_attention",
    )(sinks, q, kv, kv, kv, kv, bias_window)


def _causal_scores(stack, k_ref, kv_len, bias_ref, far_bias):
    rows = stack.shape[0]
    s = _dot_nt(stack, k_ref[0:kv_len, :])
    if kv_len == TQ:
        return s + bias_ref[:, :, TQ:2 * TQ].reshape(rows, TQ)
    near = bias_ref[...].reshape(rows, 2 * TQ)
    if kv_len == 2 * TQ:
        return s + near
    n_far = kv_len - 2 * TQ
    return jnp.concatenate([s[:, :n_far] + far_bias, s[:, n_far:] + near], axis=1)


def _softmax_pipeline(n_blocks, scores, finish):
    order = list(range(n_blocks - 1, -1, -1))
    s_next = scores(order[0])
    p_prev = None
    for n in range(n_blocks):
        s = s_next
        if n + 1 < n_blocks:
            s_next = scores(order[n + 1])
        p = jnp.exp2(s - jnp.max(s, axis=-1, keepdims=True)).astype(BF16)
        if p_prev is not None:
            finish(order[n - 1], p_prev)
        p_prev = p
    finish(order[-1], p_prev)


def _diff_kernel(tab_ref, lam_ref, g_ref, q_ref, k_ref, v_ref, bias_ref, o_ref, *, lam_init):
    h = pl.program_id(0)
    last = NUM_BUCKETS - 1
    far_bias = jnp.concatenate([jnp.full((TQ, 1), tab_ref[last, 2 * h] * LOG2_E, F32),
                                jnp.full((TQ, 1), tab_ref[last, 2 * h + 1] * LOG2_E, F32)], axis=0)
    lf = lam_ref[...]
    lam = (jnp.exp(jnp.sum(lf[0:1] * lf[1:2], axis=-1, keepdims=True))
           - jnp.exp(jnp.sum(lf[2:3] * lf[3:4], axis=-1, keepdims=True)) + lam_init)
    ones = jnp.ones((q_ref.shape[0], LANES), BF16)

    def scores(c):
        stack = _stack_heads(q_ref[c * TQ:(c + 1) * TQ, :])
        return _causal_scores(stack, k_ref, (c + 1) * TQ, bias_ref, far_bias)

    def finish(c, p):
        kv_len = (c + 1) * TQ
        v_ext = jnp.concatenate([v_ref[0:kv_len, :], ones[0:kv_len]], axis=1)
        res = _dot(p, v_ext)
        o12 = res[:, :LANES] / res[:, LANES:]
        o = o12[:TQ] - lam * o12[TQ:]
        o_ref[c * TQ:(c + 1) * TQ, :] = (_rms(o, g_ref[...]) * (1.0 - lam_init)).astype(o_ref.dtype)

    _softmax_pipeline(q_ref.shape[0] // TQ, scores, finish)


def diff_attention(q, kv, table, lambdas, subln_g, bias_causal, batch, layer_idx):
    t = q.shape[0]
    s = t // batch
    lam_init = 0.8 - 0.6 * math.exp(-0.3 * layer_idx)
    return pl.pallas_call(
        functools.partial(_diff_kernel, lam_init=lam_init),
        grid=(DIFF_HEADS, batch),
        in_specs=[
            pl.BlockSpec(memory_space=pltpu.SMEM),
            pl.BlockSpec((4, HEAD_DIM), lambda h, b: (0, 0)),
            pl.BlockSpec((1, LANES), lambda h, b: (0, 0)),
            pl.BlockSpec((s, LANES), lambda h, b: (b, h)),
            pl.BlockSpec((s, LANES), lambda h, b: (b, h)),
            pl.BlockSpec((s, LANES), lambda h, b: (b, DIFF_HEADS + h)),
            pl.BlockSpec((2, TQ, 2 * TQ), lambda h, b: (h, 0, 0)),
        ],
        out_specs=pl.BlockSpec((s, LANES), lambda h, b: (b, h)),
        out_shape=jax.ShapeDtypeStruct((t, DIFF_HEADS * 2 * HEAD_DIM), BF16),
        compiler_params=_params("parallel", "parallel"),
        name="diff_attention",
    )(table, lambdas, subln_g.reshape(1, LANES), q, kv, kv, bias_causal)


HGRN_LEVELS = tuple(2 ** e for e in range(int(math.log2(HGRN_CHUNK)) - 1, -1, -1))
HGRN_GROUP = 256


def _hgrn_kernel(lb_ref, g_ref, q_ref, f_ref, i_ref, og_ref, o_ref,
                 qf_ref, kf_ref, bq_ref, kd_ref, dec_ref, v_ref):
    s_len, hd = q_ref.shape
    c_len = HGRN_CHUNK
    n_lvl = len(HGRN_LEVELS)
    grp = min(HGRN_GROUP, s_len)
    rc = lax.broadcasted_iota(jnp.int32, (grp, hd), 0) & (c_len - 1)
    lb = lb_ref[...]
    log_lb, log_1m_lb = jnp.log(lb), jnp.log1p(-lb)

    def prepare(gi):
        rows = slice(gi * grp, (gi + 1) * grp)
        z = f_ref[rows, :]
        u = jnp.exp(-jnp.abs(z))
        w = 1.0 + u
        log_sig = jnp.minimum(z, 0.0) - jnp.log(w)
        c = log_1m_lb + log_sig
        delta = log_lb - c
        log_f = jnp.where(jnp.isnan(delta), log_lb + c,
                          jnp.maximum(log_lb, c) + jnp.log(1.0 + jnp.exp(-jnp.abs(delta))))
        inv_w = 1.0 / w
        key = (1.0 - lb) * jnp.where(z >= 0.0, u * inv_w, inv_w)
        qs = _silu(q_ref[rows, :])
        v_ref[rows, :] = i_ref[rows, :].astype(BF16)

        b = log_f * LOG2_E
        for d in (1, 2, 4, 8, 16, 32):
            b = b + jnp.where(rc >= d, pltpu.roll(b, d, 0), 0.0)

        bq_ref[rows, :] = (qs * jnp.exp2(b)).astype(BF16)
        qf_ref[n_lvl, rows, :] = qs.astype(BF16)
        kf_ref[n_lvl, rows, :] = key.astype(BF16)

        r_m = jnp.where(rc >= 1, pltpu.roll(b, 1, 0), 0.0)
        for m in (1, 2, 4, 8, 16, 32):
            if m > 1:
                half = m // 2
                r_m = jnp.where((rc & half) != 0, pltpu.roll(r_m, half, 0), r_m)
            e_m = pltpu.roll(r_m, grp - m, 0)
            lvl = HGRN_LEVELS.index(m)
            qf_ref[lvl, rows, :] = (qs * jnp.exp2(b - r_m)).astype(BF16)
            kf_ref[lvl, rows, :] = (key * jnp.exp2(jnp.minimum(e_m - b, 0.0))).astype(BF16)
        b3 = b.reshape(grp // c_len, c_len, hd)
        b_end = jnp.broadcast_to(b3[:, c_len - 1:c_len, :], b3.shape).reshape(grp, hd)
        kd_ref[rows, :] = (key * jnp.exp2(b_end - b)).astype(BF16)
        dec_ref[rows, :] = jnp.exp2(b_end)

    ti = lax.broadcasted_iota(jnp.int32, (c_len, c_len), 0)
    si = lax.broadcasted_iota(jnp.int32, (c_len, c_len), 1)
    masks = [((ti // m) % 2 == 1) & (si // m == ti // m - 1) for m in HGRN_LEVELS] + [ti == si]
    g = g_ref[...]

    def chunk(ci, st):
        rows = slice(ci * c_len, (ci + 1) * c_len)
        att = jnp.zeros((c_len, c_len), F32)
        for lvl in range(n_lvl + 1):
            att = jnp.where(masks[lvl], _dot_nt(qf_ref[lvl, rows, :], kf_ref[lvl, rows, :]), att)
        vc = v_ref[rows, :]
        o = _dot(att.astype(BF16), vc) + _dot_nt(bq_ref[rows, :], st.astype(BF16))
        o_ref[rows, :] = (_rms(o, g) * _silu(og_ref[rows, :])).astype(o_ref.dtype)
        return st * dec_ref[ci * c_len:ci * c_len + 1, :] + _dot_tn(vc, kd_ref[rows, :])

    n_groups = s_len // grp
    per_group = grp // c_len
    st = jnp.zeros((hd, hd), F32)
    prepare(0)
    for gi in range(n_groups):
        if gi + 1 < n_groups:
            prepare(gi + 1)
        for ci in range(gi * per_group, (gi + 1) * per_group):
            st = chunk(ci, st)


def hgrn_recurrence(z, lb, norm_g, batch):
    t = z.shape[0]
    s = t // batch
    hd = HGRN_HEAD_DIM
    n_lvl = len(HGRN_LEVELS)

    def part(p):
        return pl.BlockSpec((s, hd), lambda b, h: (b, p * HGRN_HEADS + h))

    return pl.pallas_call(
        _hgrn_kernel,
        grid=(batch, HGRN_HEADS),
        in_specs=[pl.BlockSpec((1, hd), lambda b, h: (0, h)), pl.BlockSpec((1, hd), lambda b, h: (0, 0)),
                  part(0), part(1), part(2), part(3)],
        out_specs=pl.BlockSpec((s, hd), lambda b, h: (b, h)),
        out_shape=jax.ShapeDtypeStruct((t, HGRN_HEADS * hd), BF16),
        scratch_shapes=[
            pltpu.VMEM((n_lvl + 1, s, hd), BF16),
            pltpu.VMEM((n_lvl + 1, s, hd), BF16),
            pltpu.VMEM((s, hd), BF16),
            pltpu.VMEM((s, hd), BF16),
            pltpu.VMEM((s, hd), F32),
            pltpu.VMEM((s, hd), BF16),
        ],
        compiler_params=_params("parallel", "parallel"),
        name="hgrn_recurrence",
    )(lb.reshape(1, -1), norm_g.reshape(1, hd), z, z, z, z)


def _sortable(score):
    score = jnp.where(score == 0.0, 0.0, score)
    bits = lax.bitcast_convert_type(score, jnp.int32)
    return bits ^ ((bits >> 31) & jnp.int32(0x7FFFFFFF))


def _dsa_select(c, qi_ref, ki_ref, wi_ref, qsel_ref, kib_ref, wt_ref, sc_ref, keys_ref, mask_ref, n_sel):
    kv_len = (c + 1) * TQ
    s_len = mask_ref.shape[1]
    if kv_len <= n_sel:
        mask_ref[:, 0:kv_len] = jnp.zeros((TQ, kv_len), mask_ref.dtype)
        if kv_len < s_len:
            mask_ref[:, kv_len:] = jnp.full((TQ, s_len - kv_len), NEG_INF, mask_ref.dtype)
        return
    lo = lax.broadcasted_iota(jnp.int32, (TQ, LANES), 1) < HEAD_DIM
    for p in range(IDX_HEADS // 2):
        qp = qi_ref[:, p * LANES:(p + 1) * LANES]
        qsel_ref[2 * p] = _head_select(qp, lo, True)
        qsel_ref[2 * p + 1] = _head_select(qp, lo, False)
    wt_ref[...] = wi_ref[...].T * (IDX_HEADS ** -0.5 * IDX_DIM ** -0.5)
    ki = ki_ref[0:kv_len, :]
    lo_k = lax.broadcasted_iota(jnp.int32, (kv_len, LANES), 1) < IDX_DIM
    kib_ref[0:kv_len, :] = jnp.where(lo_k, ki, pltpu.roll(ki, IDX_DIM, 1)).astype(BF16)
    sc_ref[0:kv_len, :] = jnp.zeros((kv_len, TQ), F32)

    def head_pair(hp, carry):
        kib = kib_ref[0:kv_len, :]
        acc = sc_ref[0:kv_len, :]
        for hh in (2 * hp, 2 * hp + 1):
            logit = _dot_nt(kib, qsel_ref[hh])
            acc = acc + jnp.maximum(logit, 0.0) * wt_ref[pl.ds(IDX_DIM + hh, 1), :]
        sc_ref[0:kv_len, :] = acc
        return carry

    lax.fori_loop(0, IDX_HEADS // 2, head_pair, 0)

    key_i = lax.broadcasted_iota(jnp.int32, (TQ, TQ), 0)
    qry_i = lax.broadcasted_iota(jnp.int32, (TQ, TQ), 1)
    if c > 0:
        keys_ref[0:c * TQ, :] = _sortable(sc_ref[0:c * TQ, :])
    keys_ref[c * TQ:kv_len, :] = _sortable(jnp.where(key_i <= qry_i, sc_ref[c * TQ:kv_len, :], NEG_INF))

    def count(pred):
        return jnp.sum(pred(keys_ref[0:kv_len, :]).astype(jnp.int32), axis=0, keepdims=True)

    def search(it, thr):
        trial = thr ^ (jnp.int32(1) << (31 - it))
        return jnp.where(count(lambda kj: kj >= trial) >= n_sel, trial, thr)

    thr = lax.fori_loop(0, 32, search, jnp.full((1, TQ), jnp.iinfo(jnp.int32).min, jnp.int32))
    need = (n_sel - count(lambda kj: kj > thr)).astype(F32)
    earlier = (qry_i < key_i).astype(BF16)
    base = jnp.zeros((1, TQ), F32)
    for j in range(c + 1):
        kj = keys_ref[j * TQ:(j + 1) * TQ, :]
        eq = kj == thr
        eq_f = jnp.where(eq, 1.0, 0.0)
        rank = base + _dot(earlier, eq_f.astype(BF16))
        sel = (kj > thr) | (eq & (rank < need))
        mask_ref[:, j * TQ:(j + 1) * TQ] = jnp.where(sel, 0.0, NEG_INF).T.astype(mask_ref.dtype)
        base = base + jnp.sum(eq_f, axis=0, keepdims=True)
    if kv_len < s_len:
        mask_ref[:, kv_len:] = jnp.full((TQ, s_len - kv_len), NEG_INF, mask_ref.dtype)


def _dsa_select_kernel(qi_ref, ki_ref, wi_ref, mask_ref, qsel_ref, kib_ref, wt_ref, sc_ref, keys_ref, *, n_sel, nq):
    i = pl.program_id(1)
    for c in range(nq):
        pl.when(i == c)(functools.partial(
            _dsa_select, c, qi_ref, ki_ref, wi_ref, qsel_ref, kib_ref, wt_ref, sc_ref, keys_ref, mask_ref, n_sel))


def _dsa_attend_kernel(tab_ref, q_ref, kpair_ref, vpair_ref, mask_ref, bias_ref, o_ref, k_ref, v_ref):
    g = pl.program_id(1)
    k_ref[...] = _dup_head(kpair_ref[...], g % 2 == 1)
    v_ref[...] = _dup_head(vpair_ref[...], g % 2 == 1)
    last = NUM_BUCKETS - 1
    far_bias = jnp.concatenate(
        [jnp.full((TQ, 1), tab_ref[last, 4 * g + hh] * LOG2_E, F32) for hh in range(4)], axis=0)

    def scores(c):
        rows = slice(c * TQ, (c + 1) * TQ)
        kv_len = (c + 1) * TQ
        s = _causal_scores(_stack_heads(q_ref[rows, :]), k_ref, kv_len, bias_ref, far_bias)
        return s + jnp.concatenate([mask_ref[rows, 0:kv_len].astype(F32)] * 4, axis=0)

    def finish(c, p):
        v_ext = _paired_values(v_ref[0:(c + 1) * TQ, :])
        outs = []
        for pr in range(2):
            r0 = 2 * pr * TQ
            p_both = jnp.concatenate([p[r0:r0 + TQ], p[r0 + TQ:r0 + 2 * TQ]], axis=1)
            res = _dot(p_both, v_ext)
            outs.append(res[:, :LANES] / res[:, LANES:])
        o_ref[c * TQ:(c + 1) * TQ, :] = jnp.concatenate(outs, axis=1).astype(o_ref.dtype)

    _softmax_pipeline(q_ref.shape[0] // TQ, scores, finish)


def dsa_attention(q, kv, qi, z_idx, table, bias_causal, batch):
    t = q.shape[0]
    s = t // batch
    nq = s // TQ
    n_sel = min(TOPK_MAX, s // 4)
    mask = pl.pallas_call(
        functools.partial(_dsa_select_kernel, n_sel=n_sel, nq=nq),
        grid=(batch, nq),
        in_specs=[
            pl.BlockSpec((TQ, IDX_HEADS * IDX_DIM), lambda b, i: (b * nq + i, 0)),
            pl.BlockSpec((s, LANES), lambda b, i: (b, 0)),
            pl.BlockSpec((TQ, LANES), lambda b, i: (b * nq + i, 0)),
        ],
        out_specs=pl.BlockSpec((TQ, s), lambda b, i: (b * nq + i, 0)),
        out_shape=jax.ShapeDtypeStruct((t, s), BF16),
        scratch_shapes=[
            pltpu.VMEM((IDX_HEADS, TQ, LANES), BF16),
            pltpu.VMEM((s, LANES), BF16),
            pltpu.VMEM((LANES, TQ), F32),
            pltpu.VMEM((s, TQ), F32),
            pltpu.VMEM((s, TQ), jnp.int32),
        ],
        compiler_params=_params("parallel", "parallel"),
        name="dsa_select",
    )(qi, z_idx, z_idx)
    return pl.pallas_call(
        _dsa_attend_kernel,
        grid=(batch, N_KV_HEADS),
        in_specs=[
            pl.BlockSpec(memory_space=pltpu.SMEM),
            pl.BlockSpec((s, 4 * HEAD_DIM), lambda b, g: (b, g)),
            pl.BlockSpec((s, LANES), lambda b, g: (b, g // 2)),
            pl.BlockSpec((s, LANES), lambda b, g: (b, N_KV_HEADS // 2 + g // 2)),
            pl.BlockSpec((s, s), lambda b, g: (b, 0)),
            pl.BlockSpec((4, TQ, 2 * TQ), lambda b, g: (g, 0, 0)),
        ],
        out_specs=pl.BlockSpec((s, 4 * HEAD_DIM), lambda b, g: (b, g)),
        out_shape=jax.ShapeDtypeStruct((t, N_HEADS * HEAD_DIM), BF16),
        scratch_shapes=[pltpu.VMEM((s, LANES), BF16), pltpu.VMEM((s, LANES), BF16)],
        compiler_params=_params("parallel", "parallel"),
        name="dsa_attend",
    )(table, q, kv, kv, mask, bias_causal)


def _lower_bound_kernel(x_ref, o_ref):
    x = x_ref[...]
    e = jnp.exp(x - jnp.max(x, axis=0, keepdims=True))
    soft = e / jnp.sum(e, axis=0, keepdims=True)
    run = soft[0:1]
    o_ref[0:1, :] = run - soft[0:1]
    for r in range(1, x.shape[0]):
        run = run + soft[r:r + 1]
        o_ref[r:r + 1, :] = run - soft[0:1]


def hgrn_lower_bounds(logits):
    return pl.pallas_call(
        _lower_bound_kernel,
        out_shape=jax.ShapeDtypeStruct(logits.shape, F32),
        name="hgrn_lower_bounds",
    )(logits)


def kernel(x, rel_bias_table, hgrn_lb_logits, norm_g, ffn_w_up, ffn_conv, ffn_w_down, swa_w_in, swa_w_out,
           swa_sinks, diff_w_in, diff_w_out, diff_lambda, diff_subln_g, hgrn_w_in, hgrn_w_out, hgrn_norm_g,
           dsa_w_in, dsa_w_out):
    batch, seq, d = x.shape
    t = batch * seq
    depth = norm_g.shape[0]
    n_mixers = 4
    bias_causal, bias_window = bias_tiles(rel_bias_table)
    lb_all = hgrn_lower_bounds(hgrn_lb_logits)
    kv = N_KV_HEADS * HEAD_DIM
    q_scale = HEAD_DIM ** -0.5 * LOG2_E

    x2 = x.reshape(t, d)
    h = None
    for i in range(depth):
        kind, j = i % n_mixers, i // n_mixers
        if kind == 0:
            w = swa_w_in[j]
            src, g0 = (x2, norm_g[i, 0]) if h is None else (h, None)
            q = matmul(src, w, BF16, n_cols=d, scale=q_scale, norm_g=g0)
            kvd = matmul(src, w, BF16, col0=d, n_cols=2 * kv, dup_heads=True, norm_g=g0)
            o = swa_attention(q, kvd, swa_sinks[j], bias_window, batch)
            w_out = swa_w_out
        elif kind == 1:
            w = diff_w_in[j]
            q = matmul(h, w, BF16, n_cols=d, scale=q_scale)
            kvp = matmul(h, w, BF16, col0=d)
            o = diff_attention(q, kvp, rel_bias_table, diff_lambda[j], diff_subln_g[j], bias_causal, batch, i)
            w_out = diff_w_out
        elif kind == 2:
            z = matmul(h, hgrn_w_in[j], F32)
            o = hgrn_recurrence(z, lb_all[i], hgrn_norm_g[j], batch)
            w_out = hgrn_w_out
        else:
            w = dsa_w_in[j]
            n_qi = IDX_HEADS * IDX_DIM
            q = matmul(h, w, BF16, n_cols=d, scale=q_scale)
            kvd = matmul(h, w, BF16, col0=d, n_cols=2 * kv)
            qi = matmul(h, w, BF16, col0=d + 2 * kv, n_cols=n_qi)
            w_idx = jnp.pad(w[:, d + 2 * kv + n_qi:], ((0, 0), (0, LANES - IDX_DIM - IDX_HEADS)))
            z_idx = matmul(h, w_idx, F32)
            o = dsa_attention(q, kvd, qi, z_idx, rel_bias_table, bias_causal, batch)
            w_out = dsa_w_out
        x2, h = proj_residual(o, w_out.astype(BF16), j, x2, norm_g[i, 1], norm_g[i, 2])
        a, w_down = ffn_up(h, ffn_w_up, ffn_conv, ffn_w_down, i, batch)
        x2, h = proj_residual(a, w_down, 0, x2, norm_g[i, 3], norm_g[(i + 1) % depth, 0])
    return x2.reshape(batch, seq, d)
```

```python
import functools
import math

import jax
import jax.numpy as jnp
import numpy as np
from jax import lax
from jax.experimental import pallas as pl
from jax.experimental.pallas import tpu as pltpu

D_MODEL = 2048
HEAD_DIM = 64
N_HEADS = 32
N_KV_HEADS = 8
WINDOW = 128
DIFF_HEADS = 16
HGRN_HEADS = 16
HGRN_HEAD_DIM = 128
HGRN_CHUNK = 64
IDX_HEADS = 16
IDX_DIM = 64
TOPK_MAX = 256
D_FF = 5632
CONV_WIDTH = 3
NUM_BUCKETS = 32
MAX_DISTANCE = 128
RMS_EPS = 1e-6

LANES = 128
TQ = 256
VMEM_LIMIT = 56 * 1024 * 1024
NEG_INF = float("-inf")
LOG2_E = math.log2(math.e)
BF16 = jnp.bfloat16
F32 = jnp.float32


def _params(*sem):
    return pltpu.CompilerParams(dimension_semantics=sem, vmem_limit_bytes=VMEM_LIMIT)


def _dot(a, b):
    return jnp.dot(a, b, preferred_element_type=F32)


def _dot_nt(a, b):
    return lax.dot_general(a, b, (((1,), (1,)), ((), ())), preferred_element_type=F32)


def _dot_tn(a, b):
    return lax.dot_general(a, b, (((0,), (0,)), ((), ())), preferred_element_type=F32)


def _rms(x, g):
    return x * lax.rsqrt(jnp.mean(x * x, axis=-1, keepdims=True) + RMS_EPS) * g


def _silu(x):
    return x * (1.0 / (1.0 + jnp.exp(-x)))


def _dup_heads(w):
    lo = lax.broadcasted_iota(jnp.int32, (w.shape[0], LANES), 1) < HEAD_DIM
    parts = []
    for p in range(w.shape[1] // LANES):
        blk = w[:, p * LANES:(p + 1) * LANES]
        swapped = pltpu.roll(blk, HEAD_DIM, 1)
        parts += [jnp.where(lo, blk, swapped), jnp.where(lo, swapped, blk)]
    return jnp.concatenate(parts, axis=1)


def _mm_kernel(x_ref, g_ref, w_ref, *rest, scale, dup_heads, norm, side_cast):
    if side_cast:
        c_ref, o_ref, cb_ref, wb_ref = rest
    else:
        o_ref, wb_ref = rest

    @pl.when(pl.program_id(1) == 0)
    def _():
        w = w_ref[...]
        wb_ref[...] = (_dup_heads(w) if dup_heads else w).astype(BF16)
        if side_cast:
            cb_ref[...] = c_ref[...].astype(BF16)

    x = _rms(x_ref[...], g_ref[...]).astype(BF16) if norm else x_ref[...]
    y = _dot(x, wb_ref[...])
    o_ref[...] = (y if scale == 1.0 else y * scale).astype(o_ref.dtype)


def matmul(x, w, out_dtype, *, col0=0, n_cols=None, scale=1.0, dup_heads=False, norm_g=None, cast_src=None):
    if cast_src is not None:
        return _matmul_side_cast(x, w, out_dtype, col0, n_cols, scale, norm_g, cast_src)
    t, k = x.shape
    n_cols = w.shape[1] - col0 if n_cols is None else n_cols
    tm = 1024
    tn_out = next(c for c in (1024, 512, 256, 128) if (n_cols * (2 if dup_heads else 1)) % c == 0)
    tn_in = tn_out // 2 if dup_heads else tn_out
    assert n_cols % tn_in == 0 and col0 % tn_in == 0, (n_cols, col0, tn_in)
    n_out = n_cols * (2 if dup_heads else 1)
    norm = norm_g is not None
    g = norm_g.reshape(1, k) if norm else jnp.ones((1, k), F32)
    return pl.pallas_call(
        functools.partial(_mm_kernel, scale=scale, dup_heads=dup_heads, norm=norm, side_cast=False),
        grid=(n_cols // tn_in, t // tm),
        in_specs=[pl.BlockSpec((tm, k), lambda j, i: (i, 0)),
                  pl.BlockSpec((1, k), lambda j, i: (0, 0)),
                  pl.BlockSpec((k, tn_in), lambda j, i: (0, col0 // tn_in + j))],
        out_specs=pl.BlockSpec((tm, tn_out), lambda j, i: (i, j)),
        out_shape=jax.ShapeDtypeStruct((t, n_out), out_dtype),
        scratch_shapes=[pltpu.VMEM((k, tn_out), BF16)],
        compiler_params=_params("parallel", "arbitrary"),
        name="in_proj",
    )(x, g, w)


def _matmul_side_cast(x, w, out_dtype, col0, n_cols, scale, norm_g, cast_src):
    t, k = x.shape
    n_cols = w.shape[1] - col0 if n_cols is None else n_cols
    tm = 1024
    tn = next(c for c in (1024, 512, 256, 128) if n_cols % c == 0)
    assert col0 % tn == 0, (col0, tn)
    n_tiles = n_cols // tn
    _, r, d = cast_src.shape
    rows = r // n_tiles
    norm = norm_g is not None
    g = norm_g.reshape(1, k) if norm else jnp.ones((1, k), F32)
    return pl.pallas_call(
        functools.partial(_mm_kernel, scale=scale, dup_heads=False, norm=norm, side_cast=True),
        grid=(n_tiles, t // tm),
        in_specs=[pl.BlockSpec((tm, k), lambda j, i: (i, 0)),
                  pl.BlockSpec((1, k), lambda j, i: (0, 0)),
                  pl.BlockSpec((k, tn), lambda j, i: (0, col0 // tn + j)),
                  pl.BlockSpec((None, rows, d), lambda j, i: (0, j, 0), pipeline_mode=pl.Buffered(1))],
        out_specs=[pl.BlockSpec((tm, tn), lambda j, i: (i, j)),
                   pl.BlockSpec((None, rows, d), lambda j, i: (0, j, 0))],
        out_shape=[jax.ShapeDtypeStruct((t, n_cols), out_dtype), jax.ShapeDtypeStruct((1, r, d), BF16)],
        scratch_shapes=[pltpu.VMEM((k, tn), BF16)],
        compiler_params=_params("arbitrary", "arbitrary"),
        name="in_proj",
    )(x, g, w, cast_src)


PROJ_SUB_ROWS = 128


def _proj_res_kernel(a_ref, w_ref, x_ref, go_ref, gn_ref, xo_ref, h_ref):
    tm = a_ref.shape[0]
    sub = PROJ_SUB_ROWS if tm >= 4 * PROJ_SUB_ROWS else tm
    ys = [_dot(a_ref[r:r + sub, :], w_ref[...]) for r in range(0, tm, sub)]
    for n, r in enumerate(range(0, tm, sub)):
        xn = x_ref[r:r + sub, :] + _rms(ys[n], go_ref[...])
        xo_ref[r:r + sub, :] = xn
        h_ref[r:r + sub, :] = _rms(xn, gn_ref[...]).astype(h_ref.dtype)


def _proj_rows(k, d):
    for tm in (512, 256, 128):
        streamed = 2 * (tm * k * 2 + 2 * tm * d * 4 + tm * d * 2)
        temporaries = 3 * tm * d * 4
        if k * d * 2 + streamed + temporaries <= VMEM_LIMIT - (4 << 20):
            return tm
    raise ValueError("projection weight does not fit in VMEM")


def proj_residual(a, w, layer, x, g_out, g_next):
    t, k = a.shape
    d = w.shape[2]
    tm = _proj_rows(k, d)
    return pl.pallas_call(
        _proj_res_kernel,
        grid=(t // tm,),
        in_specs=[
            pl.BlockSpec((tm, k), lambda i: (i, 0)),
            pl.BlockSpec((None, k, d), lambda i: (layer, 0, 0), pipeline_mode=pl.Buffered(1)),
            pl.BlockSpec((tm, d), lambda i: (i, 0)),
            pl.BlockSpec((1, d), lambda i: (0, 0)),
            pl.BlockSpec((1, d), lambda i: (0, 0)),
        ],
        out_specs=[pl.BlockSpec((tm, d), lambda i: (i, 0)), pl.BlockSpec((tm, d), lambda i: (i, 0))],
        out_shape=[jax.ShapeDtypeStruct((t, d), F32), jax.ShapeDtypeStruct((t, d), BF16)],
        compiler_params=_params("parallel"),
        name="proj_residual",
    )(a, w, x, g_out.reshape(1, d), g_next.reshape(1, d))


FFN_ROWS = 512
FFN_COLS = 512


def _ffn_up_kernel(h_ref, wg_ref, wv_ref, cg_ref, cv_ref, wd_ref, o_ref, wdb_ref):
    @pl.when(pl.program_id(1) == 0)
    def _():
        wdb_ref[...] = wd_ref[...].astype(BF16)

    s = h_ref.shape[0]
    rows = min(FFN_ROWS, s)
    tn = o_ref.shape[1]
    row = lax.broadcasted_iota(jnp.int32, (rows, 1), 0)
    wg, wv = wg_ref[...].astype(BF16), wv_ref[...].astype(BF16)
    cg, cv = cg_ref[...], cv_ref[...]

    def conv(u, tail, c):
        u1 = jnp.where(row >= 1, pltpu.roll(u, 1, 0), tail[1:2])
        u2 = jnp.where(row >= 2, pltpu.roll(u, 2, 0), jnp.where(row == 1, tail[1:2], tail[0:1]))
        return (c[0:1] * u2 + c[1:2] * u1) + c[2:3] * u

    tail_g = tail_v = jnp.zeros((CONV_WIDTH - 1, tn), F32)
    for r in range(s // rows):
        hr = h_ref[r * rows:(r + 1) * rows, :]
        ug, uv = _dot(hr, wg), _dot(hr, wv)
        out = _silu(conv(ug, tail_g, cg)) * conv(uv, tail_v, cv)
        o_ref[r * rows:(r + 1) * rows, :] = out.astype(o_ref.dtype)
        tail_g, tail_v = ug[rows - 2:rows], uv[rows - 2:rows]


def ffn_up(h, w_up, conv_w, w_down, layer, batch):
    t, d = h.shape
    s = t // batch
    tn = FFN_COLS
    nj = D_FF // tn
    return pl.pallas_call(
        _ffn_up_kernel,
        grid=(nj, batch),
        in_specs=[
            pl.BlockSpec((s, d), lambda j, b: (b, 0)),
            pl.BlockSpec((None, d, tn), lambda j, b: (layer, 0, j)),
            pl.BlockSpec((None, d, tn), lambda j, b: (layer, 0, nj + j)),
            pl.BlockSpec((None, CONV_WIDTH, tn), lambda j, b: (layer, 0, j)),
            pl.BlockSpec((None, CONV_WIDTH, tn), lambda j, b: (layer, 0, nj + j)),
            pl.BlockSpec((None, tn, d), lambda j, b: (layer, j, 0)),
        ],
        out_specs=[pl.BlockSpec((s, tn), lambda j, b: (b, j)), pl.BlockSpec((None, tn, d), lambda j, b: (0, j, 0))],
        out_shape=[jax.ShapeDtypeStruct((t, D_FF), BF16), jax.ShapeDtypeStruct((1, D_FF, d), BF16)],
        compiler_params=_params("arbitrary", "arbitrary"),
        name="ffn_up",
    )(h, w_up, w_up, conv_w, conv_w, w_down)


def _bucket_thresholds():
    max_exact = NUM_BUCKETS // 2
    n = np.arange(1, 2 * MAX_DISTANCE, dtype=np.float64)
    large = max_exact + np.floor(np.log(n / max_exact) / math.log(MAX_DISTANCE / max_exact) * (NUM_BUCKETS - max_exact))
    bucket = np.where(n < max_exact, n, np.minimum(large, NUM_BUCKETS - 1)).astype(np.int64)
    return tuple(int(n[bucket >= b][0]) for b in range(max_exact + 1, NUM_BUCKETS))


BUCKET_THRESHOLDS = _bucket_thresholds()


BIAS_MAPS = 8


def _bias_tile_kernel(tab_ref, causal_ref, window_ref):
    m0 = pl.program_id(0) * BIAS_MAPS
    r = lax.broadcasted_iota(jnp.int32, (TQ, 2 * TQ), 0)
    c = lax.broadcasted_iota(jnp.int32, (TQ, 2 * TQ), 1)
    d = r - c + TQ
    n = jnp.maximum(d, 0)
    max_exact = NUM_BUCKETS // 2
    large = jnp.full_like(n, max_exact)
    for thr in BUCKET_THRESHOLDS:
        large = large + (n >= thr).astype(jnp.int32)
    bucket = jnp.where(n < max_exact, n, large)
    for mm in range(BIAS_MAPS):
        val = jnp.zeros((TQ, 2 * TQ), F32)
        for b in range(NUM_BUCKETS):
            val = jnp.where(bucket == b, tab_ref[b, m0 + mm] * LOG2_E, val)
        causal = jnp.where(d >= 0, val, NEG_INF)
        causal_ref[mm] = causal
        window_ref[mm] = jnp.where(d < WINDOW, causal, NEG_INF)


def bias_tiles(table):
    shape = jax.ShapeDtypeStruct((N_HEADS, TQ, 2 * TQ), F32)
    return pl.pallas_call(
        _bias_tile_kernel,
        grid=(N_HEADS // BIAS_MAPS,),
        in_specs=[pl.BlockSpec(memory_space=pltpu.SMEM)],
        out_specs=[pl.BlockSpec((BIAS_MAPS, TQ, 2 * TQ), lambda m: (m, 0, 0))] * 2,
        out_shape=[shape, shape],
        compiler_params=_params("parallel"),
        name="bias_tiles",
    )(table)


def _head_select(x, lo, first):
    zero = jnp.zeros_like(x)
    return jnp.where(lo, x, zero) if first else jnp.where(lo, zero, x)


def _dup_head(blk, second):
    x = blk.astype(F32)
    lo = lax.broadcasted_iota(jnp.int32, x.shape, 1) < HEAD_DIM
    return jnp.where(jnp.logical_xor(lo, second), x, pltpu.roll(x, HEAD_DIM, 1)).astype(BF16)


def _stack_heads(q):
    rows = q.shape[0]
    lo = lax.broadcasted_iota(jnp.int32, (rows, LANES), 1) < HEAD_DIM
    parts = []
    for p in range(q.shape[1] // LANES):
        qp = q[:, p * LANES:(p + 1) * LANES]
        parts += [_head_select(qp, lo, True), _head_select(qp, lo, False)]
    return jnp.concatenate(parts, axis=0)


def _pair_selector(n_keys):
    r = lax.broadcasted_iota(jnp.int32, (2 * n_keys, LANES), 0)
    c = lax.broadcasted_iota(jnp.int32, (2 * n_keys, LANES), 1)
    return jnp.where((r < n_keys) == (c < HEAD_DIM), 1.0, 0.0).astype(BF16)


def _paired_values(vv):
    lo = lax.broadcasted_iota(jnp.int32, vv.shape, 1) < HEAD_DIM
    v_both = jnp.concatenate([_head_select(vv, lo, True), _head_select(vv, lo, False)], axis=0)
    return jnp.concatenate([v_both, _pair_selector(vv.shape[0])], axis=1)


SWA_BLOCK = WINDOW


def _swa_kernel(sink_ref, q_ref, kp_ref, ko_ref, vp_ref, vo_ref, bias_ref, o_ref):
    i = pl.program_id(1)
    blk = SWA_BLOCK
    col = lax.broadcasted_iota(jnp.int32, (1, 2 * blk), 1)
    no_prev = (col < blk) & (i == 0)
    lo = lax.broadcasted_iota(jnp.int32, (blk, LANES), 1) < HEAD_DIM

    def scores(g):
        stack = _stack_heads(q_ref[:, g * 4 * HEAD_DIM:(g + 1) * 4 * HEAD_DIM])
        cols = slice(g * LANES, (g + 1) * LANES)
        kk = jnp.concatenate([kp_ref[:, cols], ko_ref[:, cols]], axis=0)
        s = _dot_nt(stack, kk) + bias_ref[4 * g:4 * g + 4].reshape(4 * blk, 2 * blk)
        return jnp.where(no_prev, NEG_INF, s)

    def softmax(g, s):
        sink = jnp.concatenate([jnp.full((blk, 1), sink_ref[4 * g + hh] * LOG2_E, F32) for hh in range(4)], axis=0)
        m = jnp.maximum(jnp.max(s, axis=-1, keepdims=True), sink)
        return jnp.exp2(s - m).astype(BF16), jnp.exp2(sink - m)

    def finish(g, e, e_sink):
        cols = slice(g * LANES, (g + 1) * LANES)
        v_ext = _paired_values(jnp.concatenate([vp_ref[:, cols], vo_ref[:, cols]], axis=0))
        outs = []
        for p in range(2):
            r0 = 2 * p * blk
            e_both = jnp.concatenate([e[r0:r0 + blk], e[r0 + blk:r0 + 2 * blk]], axis=1)
            res = _dot(e_both, v_ext)
            den = res[:, LANES:] + jnp.where(lo, e_sink[r0:r0 + blk], e_sink[r0 + blk:r0 + 2 * blk])
            outs.append(res[:, :LANES] / den)
        o_ref[:, g * 4 * HEAD_DIM:(g + 1) * 4 * HEAD_DIM] = jnp.concatenate(outs, axis=1).astype(o_ref.dtype)

    s_next = scores(0)
    prev = None
    for g in range(N_KV_HEADS):
        s = s_next
        if g + 1 < N_KV_HEADS:
            s_next = scores(g + 1)
        cur = softmax(g, s)
        if prev is not None:
            finish(g - 1, *prev)
        prev = cur
    finish(N_KV_HEADS - 1, *prev)


def swa_attention(q, kv, sinks, bias_window, batch):
    t, dq = q.shape
    blk = SWA_BLOCK
    nq = t // batch // blk
    dk = 2 * N_KV_HEADS * HEAD_DIM

    def own(col):
        return lambda b, i: (b * nq + i, col)

    def prev(col):
        return lambda b, i: (b * nq + jnp.maximum(i - 1, 0), col)

    bias_spec = pl.BlockSpec((N_HEADS, blk, 2 * blk), lambda b, i: (0, TQ // blk - 1, TQ // blk - 1))
    return pl.pallas_call(
        _swa_kernel,
        grid=(batch, nq),
        in_specs=[
            pl.BlockSpec(memory_space=pltpu.SMEM),
            pl.BlockSpec((blk, dq), own(0)),
            pl.BlockSpec((blk, dk), prev(0)),
            pl.BlockSpec((blk, dk), own(0)),
            pl.BlockSpec((blk, dk), prev(1)),
            pl.BlockSpec((blk, dk), own(1)),
            bias_spec,
        ],
        out_specs=pl.BlockSpec((blk, dq), own(0)),
        out_shape=jax.ShapeDtypeStruct((t, dq), BF16),
        compiler_params=_params("parallel", "parallel"),
        name="swa_attention",
    )(sinks, q, kv, kv, kv, kv, bias_window)


def _causal_scores(stack, k_ref, kv_len, bias_ref, far_bias):
    rows = stack.shape[0]
    s = _dot_nt(stack, k_ref[0:kv_len, :])
    if kv_len == TQ:
        return s + bias_ref[:, :, TQ:2 * TQ].reshape(rows, TQ)
    near = bias_ref[...].reshape(rows, 2 * TQ)
    if kv_len == 2 * TQ:
        return s + near
    n_far = kv_len - 2 * TQ
    return jnp.concatenate([s[:, :n_far] + far_bias, s[:, n_far:] + near], axis=1)


def _softmax_pipeline(n_blocks, scores, finish):
    order = list(range(n_blocks - 1, -1, -1))
    s_next = scores(order[0])
    p_prev = None
    for n in range(n_blocks):
        s = s_next
        if n + 1 < n_blocks:
            s_next = scores(order[n + 1])
        p = jnp.exp2(s - jnp.max(s, axis=-1, keepdims=True)).astype(BF16)
        if p_prev is not None:
            finish(order[n - 1], p_prev)
        p_prev = p
    finish(order[-1], p_prev)


def _diff_kernel(tab_ref, lam_ref, g_ref, q_ref, k_ref, v_ref, bias_ref, o_ref, *, lam_init):
    h = pl.program_id(0)
    last = NUM_BUCKETS - 1
    far_bias = jnp.concatenate([jnp.full((TQ, 1), tab_ref[last, 2 * h] * LOG2_E, F32),
                                jnp.full((TQ, 1), tab_ref[last, 2 * h + 1] * LOG2_E, F32)], axis=0)
    lf = lam_ref[...]
    lam = (jnp.exp(jnp.sum(lf[0:1] * lf[1:2], axis=-1, keepdims=True))
           - jnp.exp(jnp.sum(lf[2:3] * lf[3:4], axis=-1, keepdims=True)) + lam_init)
    ones = jnp.ones((q_ref.shape[0], LANES), BF16)

    def scores(c):
        stack = _stack_heads(q_ref[c * TQ:(c + 1) * TQ, :])
        return _causal_scores(stack, k_ref, (c + 1) * TQ, bias_ref, far_bias)

    def finish(c, p):
        kv_len = (c + 1) * TQ
        v_ext = jnp.concatenate([v_ref[0:kv_len, :], ones[0:kv_len]], axis=1)
        res = _dot(p, v_ext)
        o12 = res[:, :LANES] / res[:, LANES:]
        o = o12[:TQ] - lam * o12[TQ:]
        o_ref[c * TQ:(c + 1) * TQ, :] = (_rms(o, g_ref[...]) * (1.0 - lam_init)).astype(o_ref.dtype)

    _softmax_pipeline(q_ref.shape[0] // TQ, scores, finish)


def diff_attention(q, kv, table, lambdas, subln_g, bias_causal, batch, layer_idx):
    t = q.shape[0]
    s = t // batch
    lam_init = 0.8 - 0.6 * math.exp(-0.3 * layer_idx)
    return pl.pallas_call(
        functools.partial(_diff_kernel, lam_init=lam_init),
        grid=(DIFF_HEADS, batch),
        in_specs=[
            pl.BlockSpec(memory_space=pltpu.SMEM),
            pl.BlockSpec((4, HEAD_DIM), lambda h, b: (0, 0)),
            pl.BlockSpec((1, LANES), lambda h, b: (0, 0)),
            pl.BlockSpec((s, LANES), lambda h, b: (b, h)),
            pl.BlockSpec((s, LANES), lambda h, b: (b, h)),
            pl.BlockSpec((s, LANES), lambda h, b: (b, DIFF_HEADS + h)),
            pl.BlockSpec((2, TQ, 2 * TQ), lambda h, b: (h, 0, 0)),
        ],
        out_specs=pl.BlockSpec((s, LANES), lambda h, b: (b, h)),
        out_shape=jax.ShapeDtypeStruct((t, DIFF_HEADS * 2 * HEAD_DIM), BF16),
        compiler_params=_params("parallel", "parallel"),
        name="diff_attention",
    )(table, lambdas, subln_g.reshape(1, LANES), q, kv, kv, bias_causal)


HGRN_LEVELS = tuple(2 ** e for e in range(int(math.log2(HGRN_CHUNK)) - 1, -1, -1))
HGRN_GROUP = 256


def _hgrn_kernel(lb_ref, g_ref, q_ref, f_ref, i_ref, og_ref, o_ref,
                 qf_ref, kf_ref, bq_ref, kd_ref, dec_ref, v_ref):
    s_len, hd = q_ref.shape
    c_len = HGRN_CHUNK
    n_lvl = len(HGRN_LEVELS)
    grp = min(HGRN_GROUP, s_len)
    rc = lax.broadcasted_iota(jnp.int32, (grp, hd), 0) & (c_len - 1)
    lb = lb_ref[...]
    log_lb, log_1m_lb = jnp.log(lb), jnp.log1p(-lb)

    def prepare(gi):
        rows = slice(gi * grp, (gi + 1) * grp)
        z = f_ref[rows, :]
        u = jnp.exp(-jnp.abs(z))
        w = 1.0 + u
        log_sig = jnp.minimum(z, 0.0) - jnp.log(w)
        c = log_1m_lb + log_sig
        delta = log_lb - c
        log_f = jnp.where(jnp.isnan(delta), log_lb + c,
                          jnp.maximum(log_lb, c) + jnp.log(1.0 + jnp.exp(-jnp.abs(delta))))
        inv_w = 1.0 / w
        key = (1.0 - lb) * jnp.where(z >= 0.0, u * inv_w, inv_w)
        qs = _silu(q_ref[rows, :])
        v_ref[rows, :] = i_ref[rows, :].astype(BF16)

        b = log_f * LOG2_E
        for d in (1, 2, 4, 8, 16, 32):
            b = b + jnp.where(rc >= d, pltpu.roll(b, d, 0), 0.0)

        bq_ref[rows, :] = (qs * jnp.exp2(b)).astype(BF16)
        qf_ref[n_lvl, rows, :] = qs.astype(BF16)
        kf_ref[n_lvl, rows, :] = key.astype(BF16)

        r_m = jnp.where(rc >= 1, pltpu.roll(b, 1, 0), 0.0)
        for m in (1, 2, 4, 8, 16, 32):
            if m > 1:
                half = m // 2
                r_m = jnp.where((rc & half) != 0, pltpu.roll(r_m, half, 0), r_m)
            e_m = pltpu.roll(r_m, grp - m, 0)
            lvl = HGRN_LEVELS.index(m)
            qf_ref[lvl, rows, :] = (qs * jnp.exp2(b - r_m)).astype(BF16)
            kf_ref[lvl, rows, :] = (key * jnp.exp2(jnp.minimum(e_m - b, 0.0))).astype(BF16)
        b3 = b.reshape(grp // c_len, c_len, hd)
        b_end = jnp.broadcast_to(b3[:, c_len - 1:c_len, :], b3.shape).reshape(grp, hd)
        kd_ref[rows, :] = (key * jnp.exp2(b_end - b)).astype(BF16)
        dec_ref[rows, :] = jnp.exp2(b_end)

    ti = lax.broadcasted_iota(jnp.int32, (c_len, c_len), 0)
    si = lax.broadcasted_iota(jnp.int32, (c_len, c_len), 1)
    masks = [((ti // m) % 2 == 1) & (si // m == ti // m - 1) for m in HGRN_LEVELS] + [ti == si]
    g = g_ref[...]

    def chunk(ci, st):
        rows = slice(ci * c_len, (ci + 1) * c_len)
        att = jnp.zeros((c_len, c_len), F32)
        for lvl in range(n_lvl + 1):
            att = jnp.where(masks[lvl], _dot_nt(qf_ref[lvl, rows, :], kf_ref[lvl, rows, :]), att)
        vc = v_ref[rows, :]
        o = _dot(att.astype(BF16), vc) + _dot_nt(bq_ref[rows, :], st.astype(BF16))
        o_ref[rows, :] = (_rms(o, g) * _silu(og_ref[rows, :])).astype(o_ref.dtype)
        return st * dec_ref[ci * c_len:ci * c_len + 1, :] + _dot_tn(vc, kd_ref[rows, :])

    n_groups = s_len // grp
    per_group = grp // c_len
    st = jnp.zeros((hd, hd), F32)
    prepare(0)
    for gi in range(n_groups):
        if gi + 1 < n_groups:
            prepare(gi + 1)
        for ci in range(gi * per_group, (gi + 1) * per_group):
            st = chunk(ci, st)


def hgrn_recurrence(z, lb, norm_g, batch):
    t = z.shape[0]
    s = t // batch
    hd = HGRN_HEAD_DIM
    n_lvl = len(HGRN_LEVELS)

    def part(p):
        return pl.BlockSpec((s, hd), lambda b, h: (b, p * HGRN_HEADS + h))

    return pl.pallas_call(
        _hgrn_kernel,
        grid=(batch, HGRN_HEADS),
        in_specs=[pl.BlockSpec((1, hd), lambda b, h: (0, h)), pl.BlockSpec((1, hd), lambda b, h: (0, 0)),
                  part(0), part(1), part(2), part(3)],
        out_specs=pl.BlockSpec((s, hd), lambda b, h: (b, h)),
        out_shape=jax.ShapeDtypeStruct((t, HGRN_HEADS * hd), BF16),
        scratch_shapes=[
            pltpu.VMEM((n_lvl + 1, s, hd), BF16),
            pltpu.VMEM((n_lvl + 1, s, hd), BF16),
            pltpu.VMEM((s, hd), BF16),
            pltpu.VMEM((s, hd), BF16),
            pltpu.VMEM((s, hd), F32),
            pltpu.VMEM((s, hd), BF16),
        ],
        compiler_params=_params("parallel", "parallel"),
        name="hgrn_recurrence",
    )(lb.reshape(1, -1), norm_g.reshape(1, hd), z, z, z, z)


def _sortable(score):
    score = jnp.where(score == 0.0, 0.0, score)
    bits = lax.bitcast_convert_type(score, jnp.int32)
    return bits ^ ((bits >> 31) & jnp.int32(0x7FFFFFFF))


def _dsa_select(c, qi_ref, ki_ref, wi_ref, qsel_ref, kib_ref, wt_ref, sc_ref, keys_ref, mask_ref, n_sel):
    kv_len = (c + 1) * TQ
    s_len = mask_ref.shape[1]
    if kv_len <= n_sel:
        mask_ref[:, 0:kv_len] = jnp.zeros((TQ, kv_len), mask_ref.dtype)
        if kv_len < s_len:
            mask_ref[:, kv_len:] = jnp.full((TQ, s_len - kv_len), NEG_INF, mask_ref.dtype)
        return
    lo = lax.broadcasted_iota(jnp.int32, (TQ, LANES), 1) < HEAD_DIM
    for p in range(IDX_HEADS // 2):
        qp = qi_ref[:, p * LANES:(p + 1) * LANES]
        qsel_ref[2 * p] = _head_select(qp, lo, True)
        qsel_ref[2 * p + 1] = _head_select(qp, lo, False)
    wt_ref[...] = wi_ref[...].T * (IDX_HEADS ** -0.5 * IDX_DIM ** -0.5)
    ki = ki_ref[0:kv_len, :]
    lo_k = lax.broadcasted_iota(jnp.int32, (kv_len, LANES), 1) < IDX_DIM
    kib_ref[0:kv_len, :] = jnp.where(lo_k, ki, pltpu.roll(ki, IDX_DIM, 1)).astype(BF16)
    sc_ref[0:kv_len, :] = jnp.zeros((kv_len, TQ), F32)

    def head_pair(hp, carry):
        kib = kib_ref[0:kv_len, :]
        acc = sc_ref[0:kv_len, :]
        for hh in (2 * hp, 2 * hp + 1):
            logit = _dot_nt(kib, qsel_ref[hh])
            acc = acc + jnp.maximum(logit, 0.0) * wt_ref[pl.ds(IDX_DIM + hh, 1), :]
        sc_ref[0:kv_len, :] = acc
        return carry

    lax.fori_loop(0, IDX_HEADS // 2, head_pair, 0)

    key_i = lax.broadcasted_iota(jnp.int32, (TQ, TQ), 0)
    qry_i = lax.broadcasted_iota(jnp.int32, (TQ, TQ), 1)
    if c > 0:
        keys_ref[0:c * TQ, :] = _sortable(sc_ref[0:c * TQ, :])
    keys_ref[c * TQ:kv_len, :] = _sortable(jnp.where(key_i <= qry_i, sc_ref[c * TQ:kv_len, :], NEG_INF))

    def count(pred):
        return jnp.sum(pred(keys_ref[0:kv_len, :]).astype(jnp.int32), axis=0, keepdims=True)

    def search(it, thr):
        trial = thr ^ (jnp.int32(1) << (31 - it))
        return jnp.where(count(lambda kj: kj >= trial) >= n_sel, trial, thr)

    thr = lax.fori_loop(0, 32, search, jnp.full((1, TQ), jnp.iinfo(jnp.int32).min, jnp.int32))
    need = (n_sel - count(lambda kj: kj > thr)).astype(F32)
    earlier = (qry_i < key_i).astype(BF16)
    base = jnp.zeros((1, TQ), F32)
    for j in range(c + 1):
        kj = keys_ref[j * TQ:(j + 1) * TQ, :]
        eq = kj == thr
        eq_f = jnp.where(eq, 1.0, 0.0)
        rank = base + _dot(earlier, eq_f.astype(BF16))
        sel = (kj > thr) | (eq & (rank < need))
        mask_ref[:, j * TQ:(j + 1) * TQ] = jnp.where(sel, 0.0, NEG_INF).T.astype(mask_ref.dtype)
        base = base + jnp.sum(eq_f, axis=0, keepdims=True)
    if kv_len < s_len:
        mask_ref[:, kv_len:] = jnp.full((TQ, s_len - kv_len), NEG_INF, mask_ref.dtype)


def _dsa_select_kernel(qi_ref, ki_ref, wi_ref, mask_ref, qsel_ref, kib_ref, wt_ref, sc_ref, keys_ref, *, n_sel, nq):
    i = pl.program_id(1)
    for c in range(nq):
        pl.when(i == c)(functools.partial(
            _dsa_select, c, qi_ref, ki_ref, wi_ref, qsel_ref, kib_ref, wt_ref, sc_ref, keys_ref, mask_ref, n_sel))


def _dsa_attend_kernel(tab_ref, q_ref, kpair_ref, vpair_ref, mask_ref, bias_ref, o_ref, k_ref, v_ref):
    g = pl.program_id(1)
    k_ref[...] = _dup_head(kpair_ref[...], g % 2 == 1)
    v_ref[...] = _dup_head(vpair_ref[...], g % 2 == 1)
    last = NUM_BUCKETS - 1
    far_bias = jnp.concatenate(
        [jnp.full((TQ, 1), tab_ref[last, 4 * g + hh] * LOG2_E, F32) for hh in range(4)], axis=0)

    def scores(c):
        rows = slice(c * TQ, (c + 1) * TQ)
        kv_len = (c + 1) * TQ
        s = _causal_scores(_stack_heads(q_ref[rows, :]), k_ref, kv_len, bias_ref, far_bias)
        return s + jnp.concatenate([mask_ref[rows, 0:kv_len].astype(F32)] * 4, axis=0)

    def finish(c, p):
        v_ext = _paired_values(v_ref[0:(c + 1) * TQ, :])
        outs = []
        for pr in range(2):
            r0 = 2 * pr * TQ
            p_both = jnp.concatenate([p[r0:r0 + TQ], p[r0 + TQ:r0 + 2 * TQ]], axis=1)
            res = _dot(p_both, v_ext)
            outs.append(res[:, :LANES] / res[:, LANES:])
        o_ref[c * TQ:(c + 1) * TQ, :] = jnp.concatenate(outs, axis=1).astype(o_ref.dtype)

    _softmax_pipeline(q_ref.shape[0] // TQ, scores, finish)


def dsa_attention(q, kv, qi, z_idx, table, bias_causal, batch):
    t = q.shape[0]
    s = t // batch
    nq = s // TQ
    n_sel = min(TOPK_MAX, s // 4)
    mask = pl.pallas_call(
        functools.partial(_dsa_select_kernel, n_sel=n_sel, nq=nq),
        grid=(batch, nq),
        in_specs=[
            pl.BlockSpec((TQ, IDX_HEADS * IDX_DIM), lambda b, i: (b * nq + i, 0)),
            pl.BlockSpec((s, LANES), lambda b, i: (b, 0)),
            pl.BlockSpec((TQ, LANES), lambda b, i: (b * nq + i, 0)),
        ],
        out_specs=pl.BlockSpec((TQ, s), lambda b, i: (b * nq + i, 0)),
        out_shape=jax.ShapeDtypeStruct((t, s), BF16),
        scratch_shapes=[
            pltpu.VMEM((IDX_HEADS, TQ, LANES), BF16),
            pltpu.VMEM((s, LANES), BF16),
            pltpu.VMEM((LANES, TQ), F32),
            pltpu.VMEM((s, TQ), F32),
            pltpu.VMEM((s, TQ), jnp.int32),
        ],
        compiler_params=_params("parallel", "parallel"),
        name="dsa_select",
    )(qi, z_idx, z_idx)
    return pl.pallas_call(
        _dsa_attend_kernel,
        grid=(batch, N_KV_HEADS),
        in_specs=[
            pl.BlockSpec(memory_space=pltpu.SMEM),
            pl.BlockSpec((s, 4 * HEAD_DIM), lambda b, g: (b, g)),
            pl.BlockSpec((s, LANES), lambda b, g: (b, g // 2)),
            pl.BlockSpec((s, LANES), lambda b, g: (b, N_KV_HEADS // 2 + g // 2)),
            pl.BlockSpec((s, s), lambda b, g: (b, 0)),
            pl.BlockSpec((4, TQ, 2 * TQ), lambda b, g: (g, 0, 0)),
        ],
        out_specs=pl.BlockSpec((s, 4 * HEAD_DIM), lambda b, g: (b, g)),
        out_shape=jax.ShapeDtypeStruct((t, N_HEADS * HEAD_DIM), BF16),
        scratch_shapes=[pltpu.VMEM((s, LANES), BF16), pltpu.VMEM((s, LANES), BF16)],
        compiler_params=_params("parallel", "parallel"),
        name="dsa_attend",
    )(table, q, kv, kv, mask, bias_causal)


def _lower_bound_kernel(x_ref, o_ref):
    x = x_ref[...]
    e = jnp.exp(x - jnp.max(x, axis=0, keepdims=True))
    soft = e / jnp.sum(e, axis=0, keepdims=True)
    run = soft[0:1]
    o_ref[0:1, :] = run - soft[0:1]
    for r in range(1, x.shape[0]):
        run = run + soft[r:r + 1]
        o_ref[r:r + 1, :] = run - soft[0:1]


def hgrn_lower_bounds(logits):
    return pl.pallas_call(
        _lower_bound_kernel,
        out_shape=jax.ShapeDtypeStruct(logits.shape, F32),
        name="hgrn_lower_bounds",
    )(logits)


def kernel(x, rel_bias_table, hgrn_lb_logits, norm_g, ffn_w_up, ffn_conv, ffn_w_down, swa_w_in, swa_w_out,
           swa_sinks, diff_w_in, diff_w_out, diff_lambda, diff_subln_g, hgrn_w_in, hgrn_w_out, hgrn_norm_g,
           dsa_w_in, dsa_w_out):
    batch, seq, d = x.shape
    t = batch * seq
    depth = norm_g.shape[0]
    n_mixers = 4
    bias_causal, bias_window = bias_tiles(rel_bias_table)
    lb_all = hgrn_lower_bounds(hgrn_lb_logits)
    kv = N_KV_HEADS * HEAD_DIM
    q_scale = HEAD_DIM ** -0.5 * LOG2_E

    x2 = x.reshape(t, d)
    h = None
    for i in range(depth):
        kind, j = i % n_mixers, i // n_mixers
        if kind == 0:
            w = swa_w_in[j]
            src, g0 = (x2, norm_g[i, 0]) if h is None else (h, None)
            if g0 is None:
                q, w_out = matmul(src, w, BF16, n_cols=d, scale=q_scale, cast_src=swa_w_out[j:j + 1])
            else:
                q = matmul(src, w, BF16, n_cols=d, scale=q_scale, norm_g=g0)
                w_out = swa_w_out[j:j + 1].astype(BF16)
            kvd = matmul(src, w, BF16, col0=d, n_cols=2 * kv, dup_heads=True, norm_g=g0)
            o = swa_attention(q, kvd, swa_sinks[j], bias_window, batch)
        elif kind == 1:
            w = diff_w_in[j]
            q, w_out = matmul(h, w, BF16, n_cols=d, scale=q_scale, cast_src=diff_w_out[j:j + 1])
            kvp = matmul(h, w, BF16, col0=d)
            o = diff_attention(q, kvp, rel_bias_table, diff_lambda[j], diff_subln_g[j], bias_causal, batch, i)
        elif kind == 2:
            z, w_out = matmul(h, hgrn_w_in[j], F32, cast_src=hgrn_w_out[j:j + 1])
            o = hgrn_recurrence(z, lb_all[i], hgrn_norm_g[j], batch)
        else:
            w = dsa_w_in[j]
            n_qi = IDX_HEADS * IDX_DIM
            q, w_out = matmul(h, w, BF16, n_cols=d, scale=q_scale, cast_src=dsa_w_out[j:j + 1])
            kvd = matmul(h, w, BF16, col0=d, n_cols=2 * kv)
            qi = matmul(h, w, BF16, col0=d + 2 * kv, n_cols=n_qi)
            w_idx = jnp.pad(w[:, d + 2 * kv + n_qi:], ((0, 0), (0, LANES - IDX_DIM - IDX_HEADS)))
            z_idx = matmul(h, w_idx, F32)
            o = dsa_attention(q, kvd, qi, z_idx, rel_bias_table, bias_causal, batch)
        x2, h = proj_residual(o, w_out, 0, x2, norm_g[i, 1], norm_g[i, 2])
        a, w_down = ffn_up(h, ffn_w_up, ffn_conv, ffn_w_down, i, batch)
        x2, h = proj_residual(a, w_down, 0, x2, norm_g[i, 3], norm_g[(i + 1) % depth, 0])
    return x2.reshape(batch, seq, d)
```
